```python
import jax, jax.numpy as jnp
from jax import lax
import numpy as np

D_MODEL = 2048
BATCH = 4
SEQ = 4096
DEPTH = 2

N_MIXERS = 2
N_POOL_LAYERS = (DEPTH + 1) // 2
N_RWKV_LAYERS = DEPTH // 2
D_FF = 5632
POOL_WINDOWS = (2, 4, 8, 16)
N_POOL_GROUPS = len(POOL_WINDOWS)
POOL_GROUP = D_MODEL // N_POOL_GROUPS
HEAD_SIZE = 64
N_HEADS = D_MODEL // HEAD_SIZE
D_DECAY_LORA = 96
D_AAA_LORA = 96
D_GATE_LORA = 256
N_DIRS = 2
RMS_EPS = 1e-6
GN_EPS = 64e-5

kernel_name = 'hybrid_pool_rwkv7_macaron_encoder'


def rms_norm(x, g):
    xf = x.astype(jnp.float32)
    y = xf * lax.rsqrt(jnp.mean(xf * xf, axis=-1, keepdims=True) + RMS_EPS)
    return (y * g.astype(jnp.float32)).astype(x.dtype)


def swiglu(x, w_gate, w_up, w_down):
    return (jax.nn.silu(x @ w_gate) * (x @ w_up)) @ w_down


def pool_mixer(x, w_grp, scale):
    B, S, D = x.shape
    xf = x.astype(jnp.float32).reshape(B, S, N_POOL_GROUPS, POOL_GROUP)
    cs = jnp.concatenate([jnp.zeros_like(xf[:, :1]), jnp.cumsum(xf, axis=1)], axis=1)
    t = jnp.arange(S)
    outs = []
    for g, w in enumerate(POOL_WINDOWS):
        lo = jnp.clip(t - w // 2, 0, S)
        hi = jnp.clip(t - w // 2 + w, 0, S)
        csg = cs[:, :, g]
        win_sum = csg[:, hi] - csg[:, lo]
        cnt = (hi - lo).astype(jnp.float32)[None, :, None]
        outs.append(win_sum / cnt - xf[:, :, g])
    pooled = jnp.stack(outs, axis=2).astype(x.dtype)
    y = jnp.einsum('bsgc,gcd->bsgd', pooled, w_grp).reshape(B, S, D)
    return y * scale


def centred_shift_delta(x):
    zero = jnp.zeros_like(x[:, :1])
    prev = jnp.concatenate([zero, x[:, :-1]], axis=1)
    nxt = jnp.concatenate([x[:, 1:], zero], axis=1)
    return 0.5 * (prev + nxt) - x


def wkv7_scan(r, decay, k, v, a_vec, b, reverse):
    B, S, H, N = r.shape

    def step(state, inp):
        r_t, w_t, k_t, v_t, a_t, b_t = inp
        sa = jnp.einsum('bhij,bhj->bhi', state, a_t)
        state = (state * w_t[:, :, None, :] + sa[..., None] * b_t[:, :, None, :]
                 + v_t[..., None] * k_t[:, :, None, :])
        y = jnp.einsum('bhij,bhj->bhi', state, r_t)
        return state, y

    s0 = jnp.zeros((B, H, N, N), jnp.float32)
    xs = tuple(jnp.moveaxis(z, 1, 0) for z in (r, decay, k, v, a_vec, b))
    _, ys = lax.scan(step, s0, xs, reverse=reverse)
    return jnp.moveaxis(ys, 0, 1)


def rwkv7_bidir_mixer(x, mu, w_r, w_k, w_v, w_o, w0, w1, w2, a0, a1, a2, g1, g2,
                      k_k, k_a, r_k, ln_w, ln_b):
    B, S, D = x.shape
    f32 = jnp.float32
    heads = lambda z: z.astype(f32).reshape(B, S, N_HEADS, HEAD_SIZE)
    xx = centred_shift_delta(x)
    xr, xw, xk, xv, xa, xg = [x + xx * mu[i] for i in range(6)]
    r = xr @ w_r
    k = xk @ w_k
    v = xv @ w_v
    g = jax.nn.sigmoid(xg @ g1) @ g2
    kk = heads(k * k_k)
    kk = kk / jnp.maximum(jnp.sqrt(jnp.sum(kk * kk, axis=-1, keepdims=True)), 1e-12)
    rh, vh = heads(r), heads(v)
    y_sum = jnp.zeros((B, S, N_HEADS, HEAD_SIZE), f32)
    bonus = jnp.zeros((B, S, N_HEADS, HEAD_SIZE), f32)
    for d, rev in enumerate((False, True)):
        w_log = -jax.nn.softplus(-(w0[d] + jnp.tanh(xw @ w1[d]) @ w2[d]).astype(f32)) - 0.5
        decay = jnp.exp(-jnp.exp(w_log))
        a = jax.nn.sigmoid((a0[d] + (xa @ a1[d]) @ a2[d]).astype(f32))
        kd = heads(k.astype(f32) * (1.0 + (a - 1.0) * k_a.astype(f32)))
        ah = heads(a)
        y_sum = y_sum + wkv7_scan(rh, heads(decay), kd, vh, -kk, kk * ah, rev)
        bonus = bonus + jnp.sum(rh * kd * r_k.astype(f32), axis=-1, keepdims=True) * vh
    mean = jnp.mean(y_sum, axis=-1, keepdims=True)
    var = jnp.mean(jnp.square(y_sum - mean), axis=-1, keepdims=True)
    gn = ((y_sum - mean) * lax.rsqrt(var + GN_EPS)).reshape(B, S, D)
    gn = gn * ln_w.astype(f32) + ln_b.astype(f32)
    out = (gn + bonus.reshape(B, S, D)).astype(x.dtype) * g
    return out @ w_o


def setup_inputs(seed: int = 0) -> dict:
    key = jax.random.key(seed)
    ks = iter(jax.random.split(key, 48))
    nrm = lambda shape, s: jax.random.normal(next(ks), shape, jnp.float32) * s
    D, F, L, NP, NR = D_MODEL, D_FF, DEPTH, N_POOL_LAYERS, N_RWKV_LAYERS
    base_w0 = -6.5 + 5.0 * jnp.linspace(0.0, 1.0, D) ** 0.85
    return {
        'x': nrm((BATCH, SEQ, D), 1.0),
        'ffn1_norm': 1.0 + nrm((L, D), 0.05),
        'ffn1_gate': nrm((L, D, F), D ** -0.5),
        'ffn1_up': nrm((L, D, F), D ** -0.5),
        'ffn1_down': nrm((L, F, D), F ** -0.5),
        'mix_norm': 1.0 + nrm((L, D), 0.05),
        'ffn2_norm': 1.0 + nrm((L, D), 0.05),
        'ffn2_gate': nrm((L, D, F), D ** -0.5),
        'ffn2_up': nrm((L, D, F), D ** -0.5),
        'ffn2_down': nrm((L, F, D), F ** -0.5),
        'pool_w': nrm((NP, N_POOL_GROUPS, POOL_GROUP, POOL_GROUP), POOL_GROUP ** -0.5),
        'pool_scale': 1.0 + nrm((NP, D), 0.1),
        'rwkv_mu': jax.random.uniform(next(ks), (NR, 6, D), jnp.float32),
        'rwkv_wr': nrm((NR, D, D), D ** -0.5),
        'rwkv_wk': nrm((NR, D, D), D ** -0.5),
        'rwkv_wv': nrm((NR, D, D), D ** -0.5),
        'rwkv_wo': nrm((NR, D, D), D ** -0.5),
        'rwkv_w0': base_w0 + nrm((NR, N_DIRS, D), 0.1),
        'rwkv_w1': nrm((NR, N_DIRS, D, D_DECAY_LORA), 0.3 * D ** -0.5),
        'rwkv_w2': nrm((NR, N_DIRS, D_DECAY_LORA, D), 0.3 * D_DECAY_LORA ** -0.5),
        'rwkv_a0': nrm((NR, N_DIRS, D), 0.1),
        'rwkv_a1': nrm((NR, N_DIRS, D, D_AAA_LORA), D ** -0.5),
        'rwkv_a2': nrm((NR, N_DIRS, D_AAA_LORA, D), 0.3 * D_AAA_LORA ** -0.5),
        'rwkv_g1': nrm((NR, D, D_GATE_LORA), D ** -0.5),
        'rwkv_g2': nrm((NR, D_GATE_LORA, D), D_GATE_LORA ** -0.5),
        'rwkv_kk': 0.85 + nrm((NR, D), 0.05),
        'rwkv_ka': 1.0 + nrm((NR, D), 0.05),
        'rwkv_rk': nrm((NR, N_HEADS, HEAD_SIZE), 0.1),
        'rwkv_lnw': 1.0 + nrm((NR, D), 0.05),
        'rwkv_lnb': nrm((NR, D), 0.02),
        'final_norm': 1.0 + nrm((D,), 0.05),
    }


def reference(x, ffn1_norm, ffn1_gate, ffn1_up, ffn1_down, mix_norm,
              ffn2_norm, ffn2_gate, ffn2_up, ffn2_down, pool_w, pool_scale,
              rwkv_mu, rwkv_wr, rwkv_wk, rwkv_wv, rwkv_wo, rwkv_w0, rwkv_w1, rwkv_w2,
              rwkv_a0, rwkv_a1, rwkv_a2, rwkv_g1, rwkv_g2, rwkv_kk, rwkv_ka, rwkv_rk,
              rwkv_lnw, rwkv_lnb, final_norm):
    h = x
    for i in range(DEPTH):
        h = h + 0.5 * swiglu(rms_norm(h, ffn1_norm[i]), ffn1_gate[i], ffn1_up[i], ffn1_down[i])
        hn = rms_norm(h, mix_norm[i])
        j = i // N_MIXERS
        if i % N_MIXERS == 0:
            m = pool_mixer(hn, pool_w[j], pool_scale[j])
        else:
            m = rwkv7_bidir_mixer(hn, rwkv_mu[j], rwkv_wr[j], rwkv_wk[j], rwkv_wv[j], rwkv_wo[j],
                                  rwkv_w0[j], rwkv_w1[j], rwkv_w2[j], rwkv_a0[j], rwkv_a1[j],
                                  rwkv_a2[j], rwkv_g1[j], rwkv_g2[j], rwkv_kk[j], rwkv_ka[j],
                                  rwkv_rk[j], rwkv_lnw[j], rwkv_lnb[j])
        h = h + m
        h = h + 0.5 * swiglu(rms_norm(h, ffn2_norm[i]), ffn2_gate[i], ffn2_up[i], ffn2_down[i])
    return rms_norm(h, final_norm)
```

```python
import functools
import math

import jax
import jax.numpy as jnp
from jax import lax
from jax.experimental import pallas as pl
from jax.experimental.pallas import tpu as pltpu

F32 = jnp.float32
BF16 = jnp.bfloat16

RMS_EPS = 1e-6
GN_EPS = 64e-5
HEAD = 64
PAIR = 2 * HEAD
CHUNK = 64
POOL_WINDOWS = (2, 4, 8, 16)
POOL_HALO = 16
SHIFT_HALO = 8
VMEM_LIMIT = 56 * 1024 * 1024


def _cparams(sem):
    return pltpu.CompilerParams(dimension_semantics=sem, vmem_limit_bytes=VMEM_LIMIT)


def _rms(x, g):
    return x * lax.rsqrt(jnp.mean(x * x, axis=-1, keepdims=True) + RMS_EPS) * g


def _dot(a, b):
    return jnp.dot(a, b, preferred_element_type=F32)


def _dot_nt(a, b):
    return lax.dot_general(a, b, (((1,), (1,)), ((), ())), preferred_element_type=F32)


def _dot_tn(a, b):
    return lax.dot_general(a, b, (((0,), (0,)), ((), ())), preferred_element_type=F32)


def _split2(x):
    hi = x.astype(BF16)
    lo = (x - hi.astype(F32)).astype(BF16)
    return hi, lo


def _split3(x):
    hi = x.astype(BF16)
    r1 = x - hi.astype(F32)
    mid = r1.astype(BF16)
    lo = (r1 - mid.astype(F32)).astype(BF16)
    return hi, mid, lo


def _dot2(x, m_bf16):
    hi, lo = _split2(x)
    return _dot(hi, m_bf16) + _dot(lo, m_bf16)


def _ffn_body(x_ref, g_ref, wg_ref, wu_ref, wd_ref, fg_ref, o_ref, n_ref, acc_ref, *, final_norm):
    f = pl.program_id(1)

    @pl.when(f == 0)
    def _():
        n_ref[...] = _rms(x_ref[...], g_ref[...]).astype(BF16)
        acc_ref[...] = jnp.zeros_like(acc_ref)

    n = n_ref[...]
    gate = _dot(n, wg_ref[...])
    up = _dot(n, wu_ref[...])
    act = (gate * jax.nn.sigmoid(gate) * up).astype(BF16)
    acc_ref[...] += _dot(act, wd_ref[...])

    @pl.when(f == pl.num_programs(1) - 1)
    def _():
        y = x_ref[...] + 0.5 * acc_ref[...]
        if final_norm:
            y = _rms(y, fg_ref[...])
        o_ref[...] = y


def _ffn(h, norm_g, wg, wu, wd, final_g, *, final_norm, tm, tf):
    t, d = h.shape
    f = wg.shape[1]
    return pl.pallas_call(
        functools.partial(_ffn_body, final_norm=final_norm),
        grid=(t // tm, f // tf),
        in_specs=[
            pl.BlockSpec((tm, d), lambda i, j: (i, 0)),
            pl.BlockSpec((1, d), lambda i, j: (0, 0)),
            pl.BlockSpec((d, tf), lambda i, j: (0, j)),
            pl.BlockSpec((d, tf), lambda i, j: (0, j)),
            pl.BlockSpec((tf, d), lambda i, j: (j, 0)),
            pl.BlockSpec((1, d), lambda i, j: (0, 0)),
        ],
        out_specs=pl.BlockSpec((tm, d), lambda i, j: (i, 0)),
        out_shape=jax.ShapeDtypeStruct((t, d), F32),
        scratch_shapes=[pltpu.VMEM((tm, d), BF16), pltpu.VMEM((tm, d), F32)],
        compiler_params=_cparams(("parallel", "arbitrary")),
        name="ffn",
    )(h, norm_g, wg, wu, wd, final_g)


def _pool_body(x_ref, xp_ref, xn_ref, g_ref, w_ref, sc_ref, o_ref, ext_ref, *, seq):
    i = pl.program_id(1)
    ts = x_ref.shape[0]
    cg = w_ref.shape[1]
    g = g_ref[...]
    first = i == 0
    last = i == pl.num_programs(1) - 1
    ext_ref[0:POOL_HALO, :] = jnp.where(first, 0.0, _rms(xp_ref[...], g))
    ext_ref[POOL_HALO:POOL_HALO + ts, :] = _rms(x_ref[...], g)
    ext_ref[POOL_HALO + ts:POOL_HALO + ts + POOL_HALO, :] = jnp.where(last, 0.0, _rms(xn_ref[...], g))

    t_glob = i * ts + lax.broadcasted_iota(jnp.int32, (ts, 1), 0)
    for gi, w in enumerate(POOL_WINDOWS):
        lanes = slice(gi * cg, (gi + 1) * cg)
        acc = ext_ref[POOL_HALO - w // 2:POOL_HALO - w // 2 + ts, lanes]
        for o in range(1, w):
            acc = acc + ext_ref[POOL_HALO - w // 2 + o:POOL_HALO - w // 2 + o + ts, lanes]
        lo = jnp.clip(t_glob - w // 2, 0, seq)
        hi = jnp.clip(t_glob - w // 2 + w, 0, seq)
        cnt = (hi - lo).astype(F32)
        pooled = acc / cnt - ext_ref[POOL_HALO:POOL_HALO + ts, lanes]
        y = _dot(pooled.astype(BF16), w_ref[gi])
        o_ref[:, lanes] = x_ref[:, lanes] + y * sc_ref[:, lanes]


def _pool(h, norm_g, w_grp, scale, *, ts):
    b, s, d = h.shape
    ng, cg, _ = w_grp.shape
    nb = ts // POOL_HALO
    last_blk = s // POOL_HALO - 1
    return pl.pallas_call(
        functools.partial(_pool_body, seq=s),
        grid=(b, s // ts),
        in_specs=[
            pl.BlockSpec((None, ts, d), lambda bi, i: (bi, i, 0)),
            pl.BlockSpec((None, POOL_HALO, d), lambda bi, i: (bi, jnp.maximum(i * nb - 1, 0), 0)),
            pl.BlockSpec((None, POOL_HALO, d), lambda bi, i: (bi, jnp.minimum((i + 1) * nb, last_blk), 0)),
            pl.BlockSpec((1, d), lambda bi, i: (0, 0)),
            pl.BlockSpec((ng, cg, cg), lambda bi, i: (0, 0, 0)),
            pl.BlockSpec((1, d), lambda bi, i: (0, 0)),
        ],
        out_specs=pl.BlockSpec((None, ts, d), lambda bi, i: (bi, i, 0)),
        out_shape=jax.ShapeDtypeStruct((b, s, d), F32),
        scratch_shapes=[pltpu.VMEM((ts + 2 * POOL_HALO, d), F32)],
        compiler_params=_cparams(("parallel", "parallel")),
        name="pool",
    )(h, h, h, norm_g, w_grp, scale)


def _norm_and_shift(x_ref, xp_ref, xn_ref, g, i, n_tiles):
    tm = x_ref.shape[0]
    hn = _rms(x_ref[...], g)
    prev_row = jnp.where(i == 0, 0.0, _rms(xp_ref[...], g)[SHIFT_HALO - 1:SHIFT_HALO])
    next_row = jnp.where(i == n_tiles - 1, 0.0, _rms(xn_ref[...], g)[0:1])
    row = lax.broadcasted_iota(jnp.int32, (tm, 1), 0)
    prev = jnp.where(row == 0, prev_row, pltpu.roll(hn, 1, axis=0))
    nxt = jnp.where(row == tm - 1, next_row, pltpu.roll(hn, tm - 1, axis=0))
    return hn, 0.5 * (prev + nxt) - hn


def _shift_specs(s, d, tm):
    nb = tm // SHIFT_HALO
    last_blk = s // SHIFT_HALO - 1
    return [
        pl.BlockSpec((None, tm, d), lambda bi, i, *_: (bi, i, 0)),
        pl.BlockSpec((None, SHIFT_HALO, d), lambda bi, i, *_: (bi, jnp.maximum(i * nb - 1, 0), 0)),
        pl.BlockSpec((None, SHIFT_HALO, d), lambda bi, i, *_: (bi, jnp.minimum((i + 1) * nb, last_blk), 0)),
    ]


def _rkv_body(x_ref, xp_ref, xn_ref, g_ref, mu_ref, w_ref, o_ref, hn_ref, xx_ref):
    j = pl.program_id(2)

    @pl.when(j == 0)
    def _():
        hn, xx = _norm_and_shift(x_ref, xp_ref, xn_ref, g_ref[...], pl.program_id(1), pl.num_programs(1))
        hn_ref[...] = hn
        xx_ref[...] = xx

    xm = (hn_ref[...] + xx_ref[...] * mu_ref[...]).astype(BF16)
    o_ref[...] = _dot(xm, w_ref[...])


def _rwkv_rkv(h, norm_g, mu3, w3, *, tm):
    b, s, d = h.shape
    return pl.pallas_call(
        _rkv_body,
        grid=(b, s // tm, 3),
        in_specs=_shift_specs(s, d, tm) + [
            pl.BlockSpec((1, d), lambda bi, i, j: (0, 0)),
            pl.BlockSpec((None, 1, d), lambda bi, i, j: (j, 0, 0)),
            pl.BlockSpec((None, d, d), lambda bi, i, j: (j, 0, 0)),
        ],
        out_specs=pl.BlockSpec((None, None, tm, d), lambda bi, i, j: (j, bi, i, 0)),
        out_shape=jax.ShapeDtypeStruct((3, b, s, d), F32),
        scratch_shapes=[pltpu.VMEM((tm, d), F32), pltpu.VMEM((tm, d), F32)],
        compiler_params=_cparams(("parallel", "parallel", "arbitrary")),
        name="rwkv_rkv",
    )(h, h, h, norm_g, mu3, w3)


def _lora_body(x_ref, xp_ref, xn_ref, g_ref, mu_ref, w1_ref, w2_ref, w0_ref, a1_ref, a2_ref, a0_ref,
               g1_ref, g2_ref, lw_ref, alr_ref, gate_ref):
    hn, xx = _norm_and_shift(x_ref, xp_ref, xn_ref, g_ref[...], pl.program_id(1), pl.num_programs(1))
    xw = (hn + xx * mu_ref[0]).astype(BF16)
    xa = (hn + xx * mu_ref[1]).astype(BF16)
    xg = (hn + xx * mu_ref[2]).astype(BF16)
    neg_rate = -math.exp(-0.5)
    for dr in range(2):
        z = w0_ref[dr] + _dot(jnp.tanh(_dot(xw, w1_ref[dr])).astype(BF16), w2_ref[dr])
        lw_ref[dr] = neg_rate * jax.nn.sigmoid(z)
        alr_ref[dr] = jax.nn.sigmoid(a0_ref[dr] + _dot(_dot(xa, a1_ref[dr]).astype(BF16), a2_ref[dr]))
    gate_ref[...] = _dot(jax.nn.sigmoid(_dot(xg, g1_ref[...])).astype(BF16), g2_ref[...])


def _rwkv_lora(h, norm_g, mu3, w1, w2, w0, a1, a2, a0, g1, g2, *, tm):
    b, s, d = h.shape
    lw_, la_, lg_ = w1.shape[2], a1.shape[2], g1.shape[1]
    full = lambda *shape: pl.BlockSpec(shape, lambda bi, i: (0,) * len(shape))
    return pl.pallas_call(
        _lora_body,
        grid=(b, s // tm),
        in_specs=_shift_specs(s, d, tm) + [
            full(1, d), full(3, 1, d),
            full(2, d, lw_), full(2, lw_, d), full(2, 1, d),
            full(2, d, la_), full(2, la_, d), full(2, 1, d),
            full(d, lg_), full(lg_, d),
        ],
        out_specs=[
            pl.BlockSpec((2, None, tm, d), lambda bi, i: (0, bi, i, 0)),
            pl.BlockSpec((2, None, tm, d), lambda bi, i: (0, bi, i, 0)),
            pl.BlockSpec((None, tm, d), lambda bi, i: (bi, i, 0)),
        ],
        out_shape=[
            jax.ShapeDtypeStruct((2, b, s, d), F32),
            jax.ShapeDtypeStruct((2, b, s, d), F32),
            jax.ShapeDtypeStruct((b, s, d), F32),
        ],
        compiler_params=_cparams(("parallel", "parallel")),
        name="rwkv_lora",
    )(h, h, h, norm_g, mu3, w1, w2, w0, a1, a2, a0, g1, g2)


def _stack(x, lane_head):
    return jnp.concatenate([jnp.where(lane_head == 0, x, 0.0), jnp.where(lane_head == 1, x, 0.0)], axis=0)


def _wkv_body(r_ref, k_ref, v_ref, lw_ref, alr_ref, kkp_ref, kap_ref, rkp_ref, y_ref, bonus_ref, s_ref):
    c = pl.program_id(2)
    fwd = pl.program_id(1) == 0
    ch, d = r_ref.shape
    rows = 2 * ch

    @pl.when(c == 0)
    def _():
        s_ref[...] = jnp.zeros_like(s_ref)

    row = lax.broadcasted_iota(jnp.int32, (rows, rows), 0)
    col = lax.broadcasted_iota(jnp.int32, (rows, rows), 1)
    t_loc, s_loc = row % ch, col % ch
    same_head = (row // ch) == (col // ch)
    before = (t_loc - s_loc) * jnp.where(fwd, 1, -1) > 0
    strict = same_head & before
    incl = same_head & (before | (s_loc == t_loc))
    eye = row == col
    tri = incl[:ch, :ch].astype(BF16)
    lane_head = lax.broadcasted_iota(jnp.int32, (ch, PAIR), 1) // HEAD
    li = lax.broadcasted_iota(jnp.int32, (PAIR, PAIR), 0) // HEAD
    lj = lax.broadcasted_iota(jnp.int32, (PAIR, PAIR), 1) // HEAD
    group = (li == lj).astype(BF16)

    for p in range(d // PAIR):
        sl = slice(p * PAIR, (p + 1) * PAIR)
        r, k, v, lw, alr = r_ref[:, sl], k_ref[:, sl], v_ref[:, sl], lw_ref[:, sl], alr_ref[:, sl]
        h3 = _split3(lw)
        cl = _dot(tri, h3[0]) + _dot(tri, h3[1]) + _dot(tri, h3[2])
        cl_end = jnp.where(fwd, cl[ch - 1:ch], cl[0:1])

        kk_raw = k * kkp_ref[:, sl]
        kk = kk_raw / jnp.maximum(jnp.sqrt(_dot2(kk_raw * kk_raw, group)), 1e-12)
        kd = k * (1.0 + (alr - 1.0) * kap_ref[:, sl])
        bv = kk * alr
        bonus_ref[:, sl] = _dot2(r * kd * rkp_ref[:, sl], group) * v

        e_neg = jnp.exp(-cl)
        e_end = jnp.exp(cl_end - cl)
        a_t = _stack(-kk * jnp.exp(cl - lw), lane_head)
        r_t = _stack(r * jnp.exp(cl), lane_head)
        b_t = _stack(bv * e_neg, lane_head)
        k_t = _stack(kd * e_neg, lane_head)
        b_e = _stack(bv * e_end, lane_head).astype(BF16)
        k_e = _stack(kd * e_end, lane_head).astype(BF16)
        v_s = _stack(v, lane_head).astype(BF16)

        aa = _dot_nt(jnp.concatenate([a_t, r_t], axis=0).astype(BF16),
                     jnp.concatenate([b_t, k_t], axis=0).astype(BF16))
        a_ab = jnp.where(strict, aa[:rows, :rows], 0.0)
        a_ak = jnp.where(strict, aa[:rows, rows:], 0.0).astype(BF16)
        a_rb = jnp.where(incl, aa[rows:, :rows], 0.0).astype(BF16)
        a_rk = jnp.where(incl, aa[rows:, rows:], 0.0).astype(BF16)

        n_pow = a_ab
        inv = jnp.where(eye, 1.0, 0.0) + a_ab
        steps = max(1, (ch - 1).bit_length())
        for it in range(1, steps):
            nb = n_pow.astype(BF16)
            if it == 1:
                n_pow = _dot(nb, nb)
            else:
                both = _dot(jnp.concatenate([inv, n_pow], axis=0).astype(BF16), nb)
                inv = inv + both[:rows]
                n_pow = both[rows:]
        if steps > 1:
            inv = inv + _dot(inv.astype(BF16), n_pow.astype(BF16))
        inv = inv.astype(BF16)

        z = _dot(a_ak, v_s)
        qp = _dot(inv, jnp.concatenate([a_t, z], axis=1).astype(BF16)).astype(BF16)
        ry = _dot(a_rb, qp)
        r_hat = r_t + ry[:, :PAIR]
        y_in = ry[:, PAIR:] + _dot(a_rk, v_s)
        gh = _dot_tn(b_e, qp)
        g_mat = jnp.where(eye, jnp.exp(cl_end), 0.0) + gh[:, :PAIR]
        h_mat = gh[:, PAIR:] + _dot_tn(k_e, v_s)

        s_old = s_ref[p].astype(BF16)
        y_bd = _dot(r_hat.astype(BF16), s_old) + y_in
        y_ref[:, sl] = y_bd[:ch] + y_bd[ch:]
        s_ref[p] = _dot(g_mat.astype(BF16), s_old) + h_mat


def _wkv(rkv, lw, alr, kkp, kap, rkp):
    _, b, s, d = rkv.shape
    nc = s // CHUNK
    assert 2 * CHUNK == PAIR

    def tok(bi, dr, c):
        return jnp.where(dr == 0, c, nc - 1 - c)

    rkv_spec = lambda j: pl.BlockSpec((None, None, CHUNK, d), lambda bi, dr, c: (j, bi, tok(bi, dr, c), 0))
    dir_spec = pl.BlockSpec((None, None, CHUNK, d), lambda bi, dr, c: (dr, bi, tok(bi, dr, c), 0))
    par_spec = pl.BlockSpec((1, d), lambda bi, dr, c: (0, 0))
    return pl.pallas_call(
        _wkv_body,
        grid=(b, 2, nc),
        in_specs=[rkv_spec(0), rkv_spec(1), rkv_spec(2), dir_spec, dir_spec, par_spec, par_spec, par_spec],
        out_specs=[dir_spec, dir_spec],
        out_shape=[jax.ShapeDtypeStruct((2, b, s, d), F32), jax.ShapeDtypeStruct((2, b, s, d), F32)],
        scratch_shapes=[pltpu.VMEM((d // PAIR, PAIR, PAIR), F32)],
        compiler_params=_cparams(("parallel", "parallel", "arbitrary")),
        name="wkv",
    )(rkv, rkv, rkv, lw, alr, kkp, kap, rkp)


def _out_body(x_ref, y_ref, bonus_ref, gate_ref, lnw_ref, lnb_ref, wo_ref, o_ref, m_ref):
    d = x_ref.shape[1]
    li = lax.broadcasted_iota(jnp.int32, (PAIR, PAIR), 0) // HEAD
    lj = lax.broadcasted_iota(jnp.int32, (PAIR, PAIR), 1) // HEAD
    avg = jnp.where(li == lj, 1.0 / HEAD, 0.0).astype(BF16)
    for p in range(d // PAIR):
        sl = slice(p * PAIR, (p + 1) * PAIR)
        y = y_ref[0, :, sl] + y_ref[1, :, sl]
        h3 = _split3(y)
        mean = _dot(h3[0], avg) + _dot(h3[1], avg) + _dot(h3[2], avg)
        dev = y - mean
        var = _dot2(dev * dev, avg)
        gn = dev * lax.rsqrt(var + GN_EPS) * lnw_ref[:, sl] + lnb_ref[:, sl]
        out = (gn + bonus_ref[0, :, sl] + bonus_ref[1, :, sl]) * gate_ref[:, sl]
        m_ref[:, sl] = out.astype(BF16)
    o_ref[...] = x_ref[...] + _dot(m_ref[...], wo_ref[...])


def _rwkv_out(h, y, bonus, gate, lnw, lnb, wo, *, tm):
    b, s, d = h.shape
    tok = pl.BlockSpec((None, tm, d), lambda bi, i: (bi, i, 0))
    both = pl.BlockSpec((2, None, tm, d), lambda bi, i: (0, bi, i, 0))
    par = pl.BlockSpec((1, d), lambda bi, i: (0, 0))
    return pl.pallas_call(
        _out_body,
        grid=(b, s // tm),
        in_specs=[tok, both, both, tok, par, par, pl.BlockSpec((d, d), lambda bi, i: (0, 0))],
        out_specs=tok,
        out_shape=jax.ShapeDtypeStruct((b, s, d), F32),
        scratch_shapes=[pltpu.VMEM((tm, d), BF16)],
        compiler_params=_cparams(("parallel", "parallel")),
        name="rwkv_out",
    )(h, y, bonus, gate, lnw, lnb, wo)


def _pad_lora(w_in, w_out):
    r = w_in.shape[-1]
    pad = (-r) % 128
    w_in = jnp.pad(w_in, [(0, 0)] * (w_in.ndim - 1) + [(0, pad)])
    w_out = jnp.pad(w_out, [(0, 0)] * (w_out.ndim - 2) + [(0, pad), (0, 0)])
    return w_in.astype(BF16), w_out.astype(BF16)


def _tile(n, want):
    t = min(n, want)
    while n % t:
        t //= 2
    return t


def _rwkv_mixer(h, norm_g, mu, w_r, w_k, w_v, w_o, w0, w1, w2, a0, a1, a2, g1, g2, k_k, k_a, r_k, ln_w, ln_b):
    b, s, d = h.shape
    row = lambda z: z.reshape(1, d)
    tm = _tile(s, 512)
    rkv = _rwkv_rkv(h, norm_g, mu[jnp.array([0, 2, 3])].reshape(3, 1, d),
                    jnp.stack([w_r, w_k, w_v]).astype(BF16), tm=tm)
    w1p, w2p = _pad_lora(w1, w2)
    a1p, a2p = _pad_lora(a1, a2)
    lw, alr, gate = _rwkv_lora(h, norm_g, mu[jnp.array([1, 4, 5])].reshape(3, 1, d),
                               w1p, w2p, w0.reshape(2, 1, d), a1p, a2p, a0.reshape(2, 1, d),
                               g1.astype(BF16), g2.astype(BF16), tm=_tile(s, 256))
    y, bonus = _wkv(rkv, lw, alr, row(k_k), row(k_a), row(r_k))
    return _rwkv_out(h, y, bonus, gate, row(ln_w), row(ln_b), w_o.astype(BF16), tm=_tile(s, 256))


def kernel(x, ffn1_norm, ffn1_gate, ffn1_up, ffn1_down, mix_norm, ffn2_norm, ffn2_gate, ffn2_up, ffn2_down, pool_w, pool_scale, rwkv_mu, rwkv_wr, rwkv_wk, rwkv_wv, rwkv_wo, rwkv_w0, rwkv_w1, rwkv_w2, rwkv_a0, rwkv_a1, rwkv_a2, rwkv_g1, rwkv_g2, rwkv_kk, rwkv_ka, rwkv_rk, rwkv_lnw, rwkv_lnb, final_norm):
    b, s, d = x.shape
    depth = ffn1_norm.shape[0]
    f = ffn1_gate.shape[2]
    row = lambda z: z.reshape(1, d)
    tm_ffn, tf = _tile(b * s, 512), _tile(f, 512)

    def ffn(h, norm_g, wg, wu, wd, final):
        out = _ffn(h.reshape(b * s, d), row(norm_g), wg.astype(BF16), wu.astype(BF16), wd.astype(BF16),
                   row(final_norm), final_norm=final, tm=tm_ffn, tf=tf)
        return out.reshape(b, s, d)

    h = x
    for i in range(depth):
        h = ffn(h, ffn1_norm[i], ffn1_gate[i], ffn1_up[i], ffn1_down[i], False)
        j = i // 2
        if i % 2 == 0:
            h = _pool(h, row(mix_norm[i]), pool_w[j].astype(BF16), row(pool_scale[j]), ts=_tile(s, 512))
        else:
            h = _rwkv_mixer(h, row(mix_norm[i]), rwkv_mu[j], rwkv_wr[j], rwkv_wk[j], rwkv_wv[j], rwkv_wo[j],
                            rwkv_w0[j], rwkv_w1[j], rwkv_w2[j], rwkv_a0[j], rwkv_a1[j], rwkv_a2[j],
                            rwkv_g1[j], rwkv_g2[j], rwkv_kk[j], rwkv_ka[j], rwkv_rk[j],
                            rwkv_lnw[j], rwkv_lnb[j])
        h = ffn(h, ffn2_norm[i], ffn2_gate[i], ffn2_up[i], ffn2_down[i], i == depth - 1)
    return h
```

```python
import functools
import math

import jax
import jax.numpy as jnp
from jax import lax
from jax.experimental import pallas as pl
from jax.experimental.pallas import tpu as pltpu

F32 = jnp.float32
BF16 = jnp.bfloat16

RMS_EPS = 1e-6
GN_EPS = 64e-5
HEAD = 64
PAIR = 2 * HEAD
CHUNK = 64
POOL_WINDOWS = (2, 4, 8, 16)
POOL_HALO = 16
SHIFT_HALO = 8
INV_STEPS = (CHUNK - 1).bit_length()
WKV_GROUP = 16
_DONE = object()
VMEM_LIMIT = 56 * 1024 * 1024


def _cparams(sem):
    return pltpu.CompilerParams(dimension_semantics=sem, vmem_limit_bytes=VMEM_LIMIT)


def _rms(x, g):
    return x * lax.rsqrt(jnp.mean(x * x, axis=-1, keepdims=True) + RMS_EPS) * g


def _dot(a, b):
    return jnp.dot(a, b, preferred_element_type=F32)


def _dot_nt(a, b):
    return lax.dot_general(a, b, (((1,), (1,)), ((), ())), preferred_element_type=F32)


def _dot_tn(a, b):
    return lax.dot_general(a, b, (((0,), (0,)), ((), ())), preferred_element_type=F32)


def _split2(x):
    hi = x.astype(BF16)
    lo = (x - hi.astype(F32)).astype(BF16)
    return hi, lo


def _split3(x):
    hi = x.astype(BF16)
    r1 = x - hi.astype(F32)
    mid = r1.astype(BF16)
    lo = (r1 - mid.astype(F32)).astype(BF16)
    return hi, mid, lo


def _dot2(x, m_bf16):
    hi, lo = _split2(x)
    return _dot(hi, m_bf16) + _dot(lo, m_bf16)


def _ffn_body(x_ref, g_ref, wg_ref, wu_ref, wd_ref, fg_ref, o_ref, n_ref, acc_ref, *, final_norm):
    f = pl.program_id(1)

    @pl.when(f == 0)
    def _():
        n_ref[...] = _rms(x_ref[...], g_ref[...]).astype(BF16)
        acc_ref[...] = jnp.zeros_like(acc_ref)

    n = n_ref[...]
    gate = _dot(n, wg_ref[...])
    up = _dot(n, wu_ref[...])
    act = (gate * jax.nn.sigmoid(gate) * up).astype(BF16)
    acc_ref[...] += _dot(act, wd_ref[...])

    @pl.when(f == pl.num_programs(1) - 1)
    def _():
        y = x_ref[...] + 0.5 * acc_ref[...]
        if final_norm:
            y = _rms(y, fg_ref[...])
        o_ref[...] = y


def _ffn(h, norm_g, wg, wu, wd, final_g, *, final_norm, tm, tf):
    t, d = h.shape
    f = wg.shape[1]
    return pl.pallas_call(
        functools.partial(_ffn_body, final_norm=final_norm),
        grid=(t // tm, f // tf),
        in_specs=[
            pl.BlockSpec((tm, d), lambda i, j: (i, 0)),
            pl.BlockSpec((1, d), lambda i, j: (0, 0)),
            pl.BlockSpec((d, tf), lambda i, j: (0, j)),
            pl.BlockSpec((d, tf), lambda i, j: (0, j)),
            pl.BlockSpec((tf, d), lambda i, j: (j, 0)),
            pl.BlockSpec((1, d), lambda i, j: (0, 0)),
        ],
        out_specs=pl.BlockSpec((tm, d), lambda i, j: (i, 0)),
        out_shape=jax.ShapeDtypeStruct((t, d), F32),
        scratch_shapes=[pltpu.VMEM((tm, d), BF16), pltpu.VMEM((tm, d), F32)],
        compiler_params=_cparams(("parallel", "arbitrary")),
        name="ffn",
    )(h, norm_g, wg, wu, wd, final_g)


def _pool_body(x_ref, xp_ref, xn_ref, g_ref, w_ref, sc_ref, o_ref, ext_ref, *, seq):
    i = pl.program_id(1)
    ts = x_ref.shape[0]
    cg = w_ref.shape[1]
    g = g_ref[...]
    first = i == 0
    last = i == pl.num_programs(1) - 1
    ext_ref[0:POOL_HALO, :] = jnp.where(first, 0.0, _rms(xp_ref[...], g))
    ext_ref[POOL_HALO:POOL_HALO + ts, :] = _rms(x_ref[...], g)
    ext_ref[POOL_HALO + ts:POOL_HALO + ts + POOL_HALO, :] = jnp.where(last, 0.0, _rms(xn_ref[...], g))

    t_glob = i * ts + lax.broadcasted_iota(jnp.int32, (ts, 1), 0)
    for gi, w in enumerate(POOL_WINDOWS):
        lanes = slice(gi * cg, (gi + 1) * cg)
        acc = ext_ref[POOL_HALO - w // 2:POOL_HALO - w // 2 + ts, lanes]
        for o in range(1, w):
            acc = acc + ext_ref[POOL_HALO - w // 2 + o:POOL_HALO - w // 2 + o + ts, lanes]
        lo = jnp.clip(t_glob - w // 2, 0, seq)
        hi = jnp.clip(t_glob - w // 2 + w, 0, seq)
        cnt = (hi - lo).astype(F32)
        pooled = acc / cnt - ext_ref[POOL_HALO:POOL_HALO + ts, lanes]
        y = _dot(pooled.astype(BF16), w_ref[gi])
        o_ref[:, lanes] = x_ref[:, lanes] + y * sc_ref[:, lanes]


def _pool(h, norm_g, w_grp, scale, *, ts):
    b, s, d = h.shape
    ng, cg, _ = w_grp.shape
    nb = ts // POOL_HALO
    last_blk = s // POOL_HALO - 1
    return pl.pallas_call(
        functools.partial(_pool_body, seq=s),
        grid=(b, s // ts),
        in_specs=[
            pl.BlockSpec((None, ts, d), lambda bi, i: (bi, i, 0)),
            pl.BlockSpec((None, POOL_HALO, d), lambda bi, i: (bi, jnp.maximum(i * nb - 1, 0), 0)),
            pl.BlockSpec((None, POOL_HALO, d), lambda bi, i: (bi, jnp.minimum((i + 1) * nb, last_blk), 0)),
            pl.BlockSpec((1, d), lambda bi, i: (0, 0)),
            pl.BlockSpec((ng, cg, cg), lambda bi, i: (0, 0, 0)),
            pl.BlockSpec((1, d), lambda bi, i: (0, 0)),
        ],
        out_specs=pl.BlockSpec((None, ts, d), lambda bi, i: (bi, i, 0)),
        out_shape=jax.ShapeDtypeStruct((b, s, d), F32),
        scratch_shapes=[pltpu.VMEM((ts + 2 * POOL_HALO, d), F32)],
        compiler_params=_cparams(("parallel", "parallel")),
        name="pool",
    )(h, h, h, norm_g, w_grp, scale)


def _norm_and_shift(x_ref, xp_ref, xn_ref, g, i, n_tiles):
    tm = x_ref.shape[0]
    hn = _rms(x_ref[...], g)
    prev_row = jnp.where(i == 0, 0.0, _rms(xp_ref[...], g)[SHIFT_HALO - 1:SHIFT_HALO])
    next_row = jnp.where(i == n_tiles - 1, 0.0, _rms(xn_ref[...], g)[0:1])
    row = lax.broadcasted_iota(jnp.int32, (tm, 1), 0)
    prev = jnp.where(row == 0, prev_row, pltpu.roll(hn, 1, axis=0))
    nxt = jnp.where(row == tm - 1, next_row, pltpu.roll(hn, tm - 1, axis=0))
    return hn, 0.5 * (prev + nxt) - hn


def _shift_specs(s, d, tm):
    nb = tm // SHIFT_HALO
    last_blk = s // SHIFT_HALO - 1
    return [
        pl.BlockSpec((None, tm, d), lambda bi, i, *_: (bi, i, 0)),
        pl.BlockSpec((None, SHIFT_HALO, d), lambda bi, i, *_: (bi, jnp.maximum(i * nb - 1, 0), 0)),
        pl.BlockSpec((None, SHIFT_HALO, d), lambda bi, i, *_: (bi, jnp.minimum((i + 1) * nb, last_blk), 0)),
    ]


def _rkv_body(x_ref, xp_ref, xn_ref, g_ref, mu_ref, w_ref, o_ref, hn_ref, xx_ref):
    j = pl.program_id(2)

    @pl.when(j == 0)
    def _():
        hn, xx = _norm_and_shift(x_ref, xp_ref, xn_ref, g_ref[...], pl.program_id(1), pl.num_programs(1))
        hn_ref[...] = hn
        xx_ref[...] = xx

    xm = (hn_ref[...] + xx_ref[...] * mu_ref[...]).astype(BF16)
    o_ref[...] = _dot(xm, w_ref[...])


def _rwkv_rkv(h, norm_g, mu3, w3, *, tm):
    b, s, d = h.shape
    return pl.pallas_call(
        _rkv_body,
        grid=(b, s // tm, 3),
        in_specs=_shift_specs(s, d, tm) + [
            pl.BlockSpec((1, d), lambda bi, i, j: (0, 0)),
            pl.BlockSpec((None, 1, d), lambda bi, i, j: (j, 0, 0)),
            pl.BlockSpec((None, d, d), lambda bi, i, j: (j, 0, 0)),
        ],
        out_specs=pl.BlockSpec((None, None, tm, d), lambda bi, i, j: (j, bi, i, 0)),
        out_shape=jax.ShapeDtypeStruct((3, b, s, d), F32),
        scratch_shapes=[pltpu.VMEM((tm, d), F32), pltpu.VMEM((tm, d), F32)],
        compiler_params=_cparams(("parallel", "parallel", "arbitrary")),
        name="rwkv_rkv",
    )(h, h, h, norm_g, mu3, w3)


def _lora_body(x_ref, xp_ref, xn_ref, g_ref, mu_ref, w1_ref, w2_ref, w0_ref, a1_ref, a2_ref, a0_ref,
               g1_ref, g2_ref, lw_ref, alr_ref, gate_ref):
    hn, xx = _norm_and_shift(x_ref, xp_ref, xn_ref, g_ref[...], pl.program_id(1), pl.num_programs(1))
    xw = (hn + xx * mu_ref[0]).astype(BF16)
    xa = (hn + xx * mu_ref[1]).astype(BF16)
    xg = (hn + xx * mu_ref[2]).astype(BF16)
    neg_rate = -math.exp(-0.5)
    for dr in range(2):
        z = w0_ref[dr] + _dot(jnp.tanh(_dot(xw, w1_ref[dr])).astype(BF16), w2_ref[dr])
        lw_ref[dr] = neg_rate * jax.nn.sigmoid(z)
        alr_ref[dr] = jax.nn.sigmoid(a0_ref[dr] + _dot(_dot(xa, a1_ref[dr]).astype(BF16), a2_ref[dr]))
    gate_ref[...] = _dot(jax.nn.sigmoid(_dot(xg, g1_ref[...])).astype(BF16), g2_ref[...])


def _rwkv_lora(h, norm_g, mu3, w1, w2, w0, a1, a2, a0, g1, g2, *, tm):
    b, s, d = h.shape
    lw_, la_, lg_ = w1.shape[2], a1.shape[2], g1.shape[1]
    full = lambda *shape: pl.BlockSpec(shape, lambda bi, i: (0,) * len(shape))
    return pl.pallas_call(
        _lora_body,
        grid=(b, s // tm),
        in_specs=_shift_specs(s, d, tm) + [
            full(1, d), full(3, 1, d),
            full(2, d, lw_), full(2, lw_, d), full(2, 1, d),
            full(2, d, la_), full(2, la_, d), full(2, 1, d),
            full(d, lg_), full(lg_, d),
        ],
        out_specs=[
            pl.BlockSpec((2, None, tm, d), lambda bi, i: (0, bi, i, 0)),
            pl.BlockSpec((2, None, tm, d), lambda bi, i: (0, bi, i, 0)),
            pl.BlockSpec((None, tm, d), lambda bi, i: (bi, i, 0)),
        ],
        out_shape=[
            jax.ShapeDtypeStruct((2, b, s, d), F32),
            jax.ShapeDtypeStruct((2, b, s, d), F32),
            jax.ShapeDtypeStruct((b, s, d), F32),
        ],
        compiler_params=_cparams(("parallel", "parallel")),
        name="rwkv_lora",
    )(h, h, h, norm_g, mu3, w1, w2, w0, a1, a2, a0, g1, g2)


def _stack(x, lane_head):
    return jnp.concatenate([jnp.where(lane_head == 0, x, 0.0), jnp.where(lane_head == 1, x, 0.0)], axis=0)


def _wkv_body(r_ref, k_ref, v_ref, lw_ref, alr_ref, kkp_ref, kap_ref, rkp_ref, y_ref, bonus_ref, s_ref):
    c = pl.program_id(2)
    fwd = pl.program_id(1) == 0
    ch, d = r_ref.shape
    rows = 2 * ch

    @pl.when(c == 0)
    def _():
        s_ref[...] = jnp.zeros_like(s_ref)

    row = lax.broadcasted_iota(jnp.int32, (rows, rows), 0)
    col = lax.broadcasted_iota(jnp.int32, (rows, rows), 1)
    t_loc, s_loc = row % ch, col % ch
    same_head = (row // ch) == (col // ch)
    before = (t_loc - s_loc) * jnp.where(fwd, 1, -1) > 0
    strict = same_head & before
    incl = same_head & (before | (s_loc == t_loc))
    eye = row == col
    tri = incl[:ch, :ch].astype(BF16)
    lane_head = lax.broadcasted_iota(jnp.int32, (ch, PAIR), 1) // HEAD
    li = lax.broadcasted_iota(jnp.int32, (PAIR, PAIR), 0) // HEAD
    lj = lax.broadcasted_iota(jnp.int32, (PAIR, PAIR), 1) // HEAD
    group = (li == lj).astype(BF16)

    def pair_steps(p):
        sl = slice(p * PAIR, (p + 1) * PAIR)
        r, k, v, lw, alr = r_ref[:, sl], k_ref[:, sl], v_ref[:, sl], lw_ref[:, sl], alr_ref[:, sl]
        h3 = _split3(lw)
        cl = _dot(tri, h3[0]) + _dot(tri, h3[1]) + _dot(tri, h3[2])
        kk_raw = k * kkp_ref[:, sl]
        kk_ss = _dot2(kk_raw * kk_raw, group)
        kd = k * (1.0 + (alr - 1.0) * kap_ref[:, sl])
        rk_sum = _dot2(r * kd * rkp_ref[:, sl], group)
        yield
        cl_end = jnp.where(fwd, cl[ch - 1:ch], cl[0:1])
        kk = kk_raw / jnp.maximum(jnp.sqrt(kk_ss), 1e-12)
        bv = kk * alr
        bonus_ref[:, sl] = rk_sum * v

        e_neg = jnp.exp(-cl)
        e_end = jnp.exp(cl_end - cl)
        a_t = _stack(-kk * jnp.exp(cl - lw), lane_head)
        r_t = _stack(r * jnp.exp(cl), lane_head)
        b_t = _stack(bv * e_neg, lane_head)
        k_t = _stack(kd * e_neg, lane_head)
        b_e = _stack(bv * e_end, lane_head).astype(BF16)
        k_e = _stack(kd * e_end, lane_head).astype(BF16)
        v_s = _stack(v, lane_head).astype(BF16)

        aa = _dot_nt(jnp.concatenate([a_t, r_t], axis=0).astype(BF16),
                     jnp.concatenate([b_t, k_t], axis=0).astype(BF16))
        yield
        a_ab = jnp.where(strict, aa[:rows, :rows], 0.0)
        a_ak = jnp.where(strict, aa[:rows, rows:], 0.0).astype(BF16)
        a_rb = jnp.where(incl, aa[rows:, :rows], 0.0).astype(BF16)
        a_rk = jnp.where(incl, aa[rows:, rows:], 0.0).astype(BF16)
        z = _dot(a_ak, v_s)
        y_in = _dot(a_rk, v_s)
        h_mat = _dot_tn(k_e, v_s)

        n_pow = a_ab
        inv = jnp.where(eye, 1.0, 0.0) + a_ab
        for it in range(1, INV_STEPS):
            nb = n_pow.astype(BF16)
            if it == 1:
                n_pow = _dot(nb, nb)
            else:
                both = _dot(jnp.concatenate([inv, n_pow], axis=0).astype(BF16), nb)
                inv = inv + both[:rows]
                n_pow = both[rows:]
            yield
        if INV_STEPS > 1:
            inv = inv + _dot(inv.astype(BF16), n_pow.astype(BF16))
            yield
        qp = _dot(inv.astype(BF16), jnp.concatenate([a_t, z], axis=1).astype(BF16)).astype(BF16)
        yield
        ry = _dot(a_rb, qp)
        gh = _dot_tn(b_e, qp)
        yield
        r_hat = r_t + ry[:, :PAIR]
        y_in = ry[:, PAIR:] + y_in
        g_mat = jnp.where(eye, jnp.exp(cl_end), 0.0) + gh[:, :PAIR]
        h_mat = gh[:, PAIR:] + h_mat
        s_old = s_ref[p].astype(BF16)
        y_bd = _dot(r_hat.astype(BF16), s_old) + y_in
        s_new = _dot(g_mat.astype(BF16), s_old) + h_mat
        yield
        y_ref[:, sl] = y_bd[:ch] + y_bd[ch:]
        s_ref[p] = s_new

    n_pairs = d // PAIR
    for g0 in range(0, n_pairs, WKV_GROUP):
        live = [pair_steps(p) for p in range(g0, min(g0 + WKV_GROUP, n_pairs))]
        while live:
            live = [g for g in live if next(g, _DONE) is not _DONE]


def _wkv(rkv, lw, alr, kkp, kap, rkp):
    _, b, s, d = rkv.shape
    nc = s // CHUNK
    assert 2 * CHUNK == PAIR

    def tok(bi, dr, c):
        return jnp.where(dr == 0, c, nc - 1 - c)

    rkv_spec = lambda j: pl.BlockSpec((None, None, CHUNK, d), lambda bi, dr, c: (j, bi, tok(bi, dr, c), 0))
    dir_spec = pl.BlockSpec((None, None, CHUNK, d), lambda bi, dr, c: (dr, bi, tok(bi, dr, c), 0))
    par_spec = pl.BlockSpec((1, d), lambda bi, dr, c: (0, 0))
    return pl.pallas_call(
        _wkv_body,
        grid=(b, 2, nc),
        in_specs=[rkv_spec(0), rkv_spec(1), rkv_spec(2), dir_spec, dir_spec, par_spec, par_spec, par_spec],
        out_specs=[dir_spec, dir_spec],
        out_shape=[jax.ShapeDtypeStruct((2, b, s, d), F32), jax.ShapeDtypeStruct((2, b, s, d), F32)],
        scratch_shapes=[pltpu.VMEM((d // PAIR, PAIR, PAIR), F32)],
        compiler_params=_cparams(("parallel", "parallel", "arbitrary")),
        name="wkv",
    )(rkv, rkv, rkv, lw, alr, kkp, kap, rkp)


def _out_body(x_ref, y_ref, bonus_ref, gate_ref, lnw_ref, lnb_ref, wo_ref, o_ref, m_ref):
    d = x_ref.shape[1]
    li = lax.broadcasted_iota(jnp.int32, (PAIR, PAIR), 0) // HEAD
    lj = lax.broadcasted_iota(jnp.int32, (PAIR, PAIR), 1) // HEAD
    avg = jnp.where(li == lj, 1.0 / HEAD, 0.0).astype(BF16)
    for p in range(d // PAIR):
        sl = slice(p * PAIR, (p + 1) * PAIR)
        y = y_ref[0, :, sl] + y_ref[1, :, sl]
        h3 = _split3(y)
        mean = _dot(h3[0], avg) + _dot(h3[1], avg) + _dot(h3[2], avg)
        dev = y - mean
        var = _dot2(dev * dev, avg)
        gn = dev * lax.rsqrt(var + GN_EPS) * lnw_ref[:, sl] + lnb_ref[:, sl]
        out = (gn + bonus_ref[0, :, sl] + bonus_ref[1, :, sl]) * gate_ref[:, sl]
        m_ref[:, sl] = out.astype(BF16)
    o_ref[...] = x_ref[...] + _dot(m_ref[...], wo_ref[...])


def _rwkv_out(h, y, bonus, gate, lnw, lnb, wo, *, tm):
    b, s, d = h.shape
    tok = pl.BlockSpec((None, tm, d), lambda bi, i: (bi, i, 0))
    both = pl.BlockSpec((2, None, tm, d), lambda bi, i: (0, bi, i, 0))
    par = pl.BlockSpec((1, d), lambda bi, i: (0, 0))
    return pl.pallas_call(
        _out_body,
        grid=(b, s // tm),
        in_specs=[tok, both, both, tok, par, par, pl.BlockSpec((d, d), lambda bi, i: (0, 0))],
        out_specs=tok,
        out_shape=jax.ShapeDtypeStruct((b, s, d), F32),
        scratch_shapes=[pltpu.VMEM((tm, d), BF16)],
        compiler_params=_cparams(("parallel", "parallel")),
        name="rwkv_out",
    )(h, y, bonus, gate, lnw, lnb, wo)


def _pad_lora(w_in, w_out):
    r = w_in.shape[-1]
    pad = (-r) % 128
    w_in = jnp.pad(w_in, [(0, 0)] * (w_in.ndim - 1) + [(0, pad)])
    w_out = jnp.pad(w_out, [(0, 0)] * (w_out.ndim - 2) + [(0, pad), (0, 0)])
    return w_in.astype(BF16), w_out.astype(BF16)


def _tile(n, want):
    t = min(n, want)
    while n % t:
        t //= 2
    return t


def _rwkv_mixer(h, norm_g, mu, w_r, w_k, w_v, w_o, w0, w1, w2, a0, a1, a2, g1, g2, k_k, k_a, r_k, ln_w, ln_b):
    b, s, d = h.shape
    row = lambda z: z.reshape(1, d)
    tm = _tile(s, 512)
    rkv = _rwkv_rkv(h, norm_g, mu[jnp.array([0, 2, 3])].reshape(3, 1, d),
                    jnp.stack([w_r, w_k, w_v]).astype(BF16), tm=tm)
    w1p, w2p = _pad_lora(w1, w2)
    a1p, a2p = _pad_lora(a1, a2)
    lw, alr, gate = _rwkv_lora(h, norm_g, mu[jnp.array([1, 4, 5])].reshape(3, 1, d),
                               w1p, w2p, w0.reshape(2, 1, d), a1p, a2p, a0.reshape(2, 1, d),
                               g1.astype(BF16), g2.astype(BF16), tm=_tile(s, 256))
    y, bonus = _wkv(rkv, lw, alr, row(k_k), row(k_a), row(r_k))
    return _rwkv_out(h, y, bonus, gate, row(ln_w), row(ln_b), w_o.astype(BF16), tm=_tile(s, 256))


def kernel(x, ffn1_norm, ffn1_gate, ffn1_up, ffn1_down, mix_norm, ffn2_norm, ffn2_gate, ffn2_up, ffn2_down, pool_w, pool_scale, rwkv_mu, rwkv_wr, rwkv_wk, rwkv_wv, rwkv_wo, rwkv_w0, rwkv_w1, rwkv_w2, rwkv_a0, rwkv_a1, rwkv_a2, rwkv_g1, rwkv_g2, rwkv_kk, rwkv_ka, rwkv_rk, rwkv_lnw, rwkv_lnb, final_norm):
    b, s, d = x.shape
    depth = ffn1_norm.shape[0]
    f = ffn1_gate.shape[2]
    row = lambda z: z.reshape(1, d)
    tm_ffn, tf = _tile(b * s, 512), _tile(f, 512)

    def ffn(h, norm_g, wg, wu, wd, final):
        out = _ffn(h.reshape(b * s, d), row(norm_g), wg.astype(BF16), wu.astype(BF16), wd.astype(BF16),
                   row(final_norm), final_norm=final, tm=tm_ffn, tf=tf)
        return out.reshape(b, s, d)

    h = x
    for i in range(depth):
        h = ffn(h, ffn1_norm[i], ffn1_gate[i], ffn1_up[i], ffn1_down[i], False)
        j = i // 2
        if i % 2 == 0:
            h = _pool(h, row(mix_norm[i]), pool_w[j].astype(BF16), row(pool_scale[j]), ts=_tile(s, 512))
        else:
            h = _rwkv_mixer(h, row(mix_norm[i]), rwkv_mu[j], rwkv_wr[j], rwkv_wk[j], rwkv_wv[j], rwkv_wo[j],
                            rwkv_w0[j], rwkv_w1[j], rwkv_w2[j], rwkv_a0[j], rwkv_a1[j], rwkv_a2[j],
                            rwkv_g1[j], rwkv_g2[j], rwkv_kk[j], rwkv_ka[j], rwkv_rk[j],
                            rwkv_lnw[j], rwkv_lnb[j])
        h = ffn(h, ffn2_norm[i], ffn2_gate[i], ffn2_up[i], ffn2_down[i], i == depth - 1)
    return h
```

```python
import functools
import math

import jax
import jax.numpy as jnp
from jax import lax
from jax.experimental import pallas as pl
from jax.experimental.pallas import tpu as pltpu

F32 = jnp.float32
BF16 = jnp.bfloat16

RMS_EPS = 1e-6
GN_EPS = 64e-5
HEAD = 64
PAIR = 2 * HEAD
CHUNK = 64
POOL_WINDOWS = (2, 4, 8, 16)
POOL_HALO = 16
SHIFT_HALO = 8
INV_STEPS = (CHUNK - 1).bit_length()
WKV_GROUP = 16
_DONE = object()
FFN_DOWN_COLS = 512
VMEM_LIMIT = 60 * 1024 * 1024


def _cparams(sem):
    return pltpu.CompilerParams(dimension_semantics=sem, vmem_limit_bytes=VMEM_LIMIT)


def _rms(x, g):
    return x * lax.rsqrt(jnp.mean(x * x, axis=-1, keepdims=True) + RMS_EPS) * g


def _dot(a, b):
    return jnp.dot(a, b, preferred_element_type=F32)


def _dot_nt(a, b):
    return lax.dot_general(a, b, (((1,), (1,)), ((), ())), preferred_element_type=F32)


def _dot_tn(a, b):
    return lax.dot_general(a, b, (((0,), (0,)), ((), ())), preferred_element_type=F32)


def _split2(x):
    hi = x.astype(BF16)
    lo = (x - hi.astype(F32)).astype(BF16)
    return hi, lo


def _split3(x):
    hi = x.astype(BF16)
    r1 = x - hi.astype(F32)
    mid = r1.astype(BF16)
    lo = (r1 - mid.astype(F32)).astype(BF16)
    return hi, mid, lo


def _dot2(x, m_bf16):
    hi, lo = _split2(x)
    return _dot(hi, m_bf16) + _dot(lo, m_bf16)


def _ffn_body(x_ref, g_ref, wg_ref, wu_ref, wd_ref, fg_ref, o_ref, n_ref, *, final_norm):
    f = pl.program_id(1)
    d = o_ref.shape[1]

    @pl.when(f == 0)
    def _():
        x = x_ref[...]
        n_ref[...] = _rms(x, g_ref[...]).astype(BF16)
        o_ref[...] = x

    n = n_ref[...]
    gate = _dot(n, wg_ref[...])
    up = _dot(n, wu_ref[...])
    act = (0.5 * gate * jax.nn.sigmoid(gate) * up).astype(BF16)
    for c0 in range(0, d, FFN_DOWN_COLS):
        cols = slice(c0, c0 + FFN_DOWN_COLS)
        o_ref[:, cols] += _dot(act, wd_ref[:, cols])

    if final_norm:
        @pl.when(f == pl.num_programs(1) - 1)
        def _():
            o_ref[...] = _rms(o_ref[...], fg_ref[...])


def _ffn(h, norm_g, wg, wu, wd, final_g, *, final_norm, tm, tf):
    t, d = h.shape
    f = wg.shape[1]
    return pl.pallas_call(
        functools.partial(_ffn_body, final_norm=final_norm),
        grid=(t // tm, f // tf),
        in_specs=[
            pl.BlockSpec((tm, d), lambda i, j: (i, 0)),
            pl.BlockSpec((1, d), lambda i, j: (0, 0)),
            pl.BlockSpec((d, tf), lambda i, j: (0, j)),
            pl.BlockSpec((d, tf), lambda i, j: (0, j)),
            pl.BlockSpec((tf, d), lambda i, j: (j, 0)),
            pl.BlockSpec((1, d), lambda i, j: (0, 0)),
        ],
        out_specs=pl.BlockSpec((tm, d), lambda i, j: (i, 0)),
        out_shape=jax.ShapeDtypeStruct((t, d), F32),
        scratch_shapes=[pltpu.VMEM((tm, d), BF16)],
        compiler_params=_cparams(("parallel", "arbitrary")),
        name="ffn",
    )(h, norm_g, wg, wu, wd, final_g)


def _pool_body(x_ref, xp_ref, xn_ref, g_ref, w_ref, sc_ref, o_ref, ext_ref, *, seq):
    i = pl.program_id(1)
    ts = x_ref.shape[0]
    cg = w_ref.shape[1]
    g = g_ref[...]
    first = i == 0
    last = i == pl.num_programs(1) - 1
    ext_ref[0:POOL_HALO, :] = jnp.where(first, 0.0, _rms(xp_ref[...], g))
    ext_ref[POOL_HALO:POOL_HALO + ts, :] = _rms(x_ref[...], g)
    ext_ref[POOL_HALO + ts:POOL_HALO + ts + POOL_HALO, :] = jnp.where(last, 0.0, _rms(xn_ref[...], g))

    t_glob = i * ts + lax.broadcasted_iota(jnp.int32, (ts, 1), 0)
    for gi, w in enumerate(POOL_WINDOWS):
        lanes = slice(gi * cg, (gi + 1) * cg)
        acc = ext_ref[POOL_HALO - w // 2:POOL_HALO - w // 2 + ts, lanes]
        for o in range(1, w):
            acc = acc + ext_ref[POOL_HALO - w // 2 + o:POOL_HALO - w // 2 + o + ts, lanes]
        lo = jnp.clip(t_glob - w // 2, 0, seq)
        hi = jnp.clip(t_glob - w // 2 + w, 0, seq)
        cnt = (hi - lo).astype(F32)
        pooled = acc / cnt - ext_ref[POOL_HALO:POOL_HALO + ts, lanes]
        y = _dot(pooled.astype(BF16), w_ref[gi])
        o_ref[:, lanes] = x_ref[:, lanes] + y * sc_ref[:, lanes]


def _pool(h, norm_g, w_grp, scale, *, ts):
    b, s, d = h.shape
    ng, cg, _ = w_grp.shape
    nb = ts // POOL_HALO
    last_blk = s // POOL_HALO - 1
    return pl.pallas_call(
        functools.partial(_pool_body, seq=s),
        grid=(b, s // ts),
        in_specs=[
            pl.BlockSpec((None, ts, d), lambda bi, i: (bi, i, 0)),
            pl.BlockSpec((None, POOL_HALO, d), lambda bi, i: (bi, jnp.maximum(i * nb - 1, 0), 0)),
            pl.BlockSpec((None, POOL_HALO, d), lambda bi, i: (bi, jnp.minimum((i + 1) * nb, last_blk), 0)),
            pl.BlockSpec((1, d), lambda bi, i: (0, 0)),
            pl.BlockSpec((ng, cg, cg), lambda bi, i: (0, 0, 0)),
            pl.BlockSpec((1, d), lambda bi, i: (0, 0)),
        ],
        out_specs=pl.BlockSpec((None, ts, d), lambda bi, i: (bi, i, 0)),
        out_shape=jax.ShapeDtypeStruct((b, s, d), F32),
        scratch_shapes=[pltpu.VMEM((ts + 2 * POOL_HALO, d), F32)],
        compiler_params=_cparams(("parallel", "parallel")),
        name="pool",
    )(h, h, h, norm_g, w_grp, scale)


def _norm_and_shift(x_ref, xp_ref, xn_ref, g, i, n_tiles):
    tm = x_ref.shape[0]
    hn = _rms(x_ref[...], g)
    prev_row = jnp.where(i == 0, 0.0, _rms(xp_ref[...], g)[SHIFT_HALO - 1:SHIFT_HALO])
    next_row = jnp.where(i == n_tiles - 1, 0.0, _rms(xn_ref[...], g)[0:1])
    row = lax.broadcasted_iota(jnp.int32, (tm, 1), 0)
    prev = jnp.where(row == 0, prev_row, pltpu.roll(hn, 1, axis=0))
    nxt = jnp.where(row == tm - 1, next_row, pltpu.roll(hn, tm - 1, axis=0))
    return hn, 0.5 * (prev + nxt) - hn


def _shift_specs(s, d, tm):
    nb = tm // SHIFT_HALO
    last_blk = s // SHIFT_HALO - 1
    return [
        pl.BlockSpec((None, tm, d), lambda bi, i, *_: (bi, i, 0)),
        pl.BlockSpec((None, SHIFT_HALO, d), lambda bi, i, *_: (bi, jnp.maximum(i * nb - 1, 0), 0)),
        pl.BlockSpec((None, SHIFT_HALO, d), lambda bi, i, *_: (bi, jnp.minimum((i + 1) * nb, last_blk), 0)),
    ]


def _rkv_body(x_ref, xp_ref, xn_ref, g_ref, mu_ref, w_ref, o_ref, hn_ref, xx_ref):
    j = pl.program_id(2)

    @pl.when(j == 0)
    def _():
        hn, xx = _norm_and_shift(x_ref, xp_ref, xn_ref, g_ref[...], pl.program_id(1), pl.num_programs(1))
        hn_ref[...] = hn
        xx_ref[...] = xx

    xm = (hn_ref[...] + xx_ref[...] * mu_ref[...]).astype(BF16)
    o_ref[...] = _dot(xm, w_ref[...])


def _rwkv_rkv(h, norm_g, mu3, w3, *, tm):
    b, s, d = h.shape
    return pl.pallas_call(
        _rkv_body,
        grid=(b, s // tm, 3),
        in_specs=_shift_specs(s, d, tm) + [
            pl.BlockSpec((1, d), lambda bi, i, j: (0, 0)),
            pl.BlockSpec((None, 1, d), lambda bi, i, j: (j, 0, 0)),
            pl.BlockSpec((None, d, d), lambda bi, i, j: (j, 0, 0)),
        ],
        out_specs=pl.BlockSpec((None, None, tm, d), lambda bi, i, j: (j, bi, i, 0)),
        out_shape=jax.ShapeDtypeStruct((3, b, s, d), F32),
        scratch_shapes=[pltpu.VMEM((tm, d), F32), pltpu.VMEM((tm, d), F32)],
        compiler_params=_cparams(("parallel", "parallel", "arbitrary")),
        name="rwkv_rkv",
    )(h, h, h, norm_g, mu3, w3)


def _lora_body(x_ref, xp_ref, xn_ref, g_ref, mu_ref, w1_ref, w2_ref, w0_ref, a1_ref, a2_ref, a0_ref,
               g1_ref, g2_ref, lw_ref, alr_ref, gate_ref):
    hn, xx = _norm_and_shift(x_ref, xp_ref, xn_ref, g_ref[...], pl.program_id(1), pl.num_programs(1))
    xw = (hn + xx * mu_ref[0]).astype(BF16)
    xa = (hn + xx * mu_ref[1]).astype(BF16)
    xg = (hn + xx * mu_ref[2]).astype(BF16)
    neg_rate = -math.exp(-0.5)
    for dr in range(2):
        z = w0_ref[dr] + _dot(jnp.tanh(_dot(xw, w1_ref[dr])).astype(BF16), w2_ref[dr])
        lw_ref[dr] = neg_rate * jax.nn.sigmoid(z)
        alr_ref[dr] = jax.nn.sigmoid(a0_ref[dr] + _dot(_dot(xa, a1_ref[dr]).astype(BF16), a2_ref[dr]))
    gate_ref[...] = _dot(jax.nn.sigmoid(_dot(xg, g1_ref[...])).astype(BF16), g2_ref[...])


def _rwkv_lora(h, norm_g, mu3, w1, w2, w0, a1, a2, a0, g1, g2, *, tm):
    b, s, d = h.shape
    lw_, la_, lg_ = w1.shape[2], a1.shape[2], g1.shape[1]
    full = lambda *shape: pl.BlockSpec(shape, lambda bi, i: (0,) * len(shape))
    return pl.pallas_call(
        _lora_body,
        grid=(b, s // tm),
        in_specs=_shift_specs(s, d, tm) + [
            full(1, d), full(3, 1, d),
            full(2, d, lw_), full(2, lw_, d), full(2, 1, d),
            full(2, d, la_), full(2, la_, d), full(2, 1, d),
            full(d, lg_), full(lg_, d),
        ],
        out_specs=[
            pl.BlockSpec((2, None, tm, d), lambda bi, i: (0, bi, i, 0)),
            pl.BlockSpec((2, None, tm, d), lambda bi, i: (0, bi, i, 0)),
            pl.BlockSpec((None, tm, d), lambda bi, i: (bi, i, 0)),
        ],
        out_shape=[
            jax.ShapeDtypeStruct((2, b, s, d), F32),
            jax.ShapeDtypeStruct((2, b, s, d), F32),
            jax.ShapeDtypeStruct((b, s, d), F32),
        ],
        compiler_params=_cparams(("parallel", "parallel")),
        name="rwkv_lora",
    )(h, h, h, norm_g, mu3, w1, w2, w0, a1, a2, a0, g1, g2)


def _prefix_sum_rows(x, row_idx):
    shift = 1
    while shift < x.shape[0]:
        x = x + jnp.where(row_idx >= shift, pltpu.roll(x, shift, axis=0), 0.0)
        shift *= 2
    return x


def _stack(x, lane_head):
    return jnp.concatenate([jnp.where(lane_head == 0, x, 0.0), jnp.where(lane_head == 1, x, 0.0)], axis=0)


def _wkv_body(r_ref, k_ref, v_ref, lw_ref, alr_ref, kkp_ref, kap_ref, rkp_ref, y_ref, bonus_ref, s_ref):
    c = pl.program_id(2)
    fwd = pl.program_id(1) == 0
    ch, d = r_ref.shape
    rows = 2 * ch

    @pl.when(c == 0)
    def _():
        s_ref[...] = jnp.zeros_like(s_ref)

    row = lax.broadcasted_iota(jnp.int32, (rows, rows), 0)
    col = lax.broadcasted_iota(jnp.int32, (rows, rows), 1)
    t_loc, s_loc = row % ch, col % ch
    same_head = (row // ch) == (col // ch)
    before = (t_loc - s_loc) * jnp.where(fwd, 1, -1) > 0
    strict = same_head & before
    incl = same_head & (before | (s_loc == t_loc))
    eye = row == col
    tok_row = lax.broadcasted_iota(jnp.int32, (ch, PAIR), 0)
    lane_head = lax.broadcasted_iota(jnp.int32, (ch, PAIR), 1) // HEAD
    li = lax.broadcasted_iota(jnp.int32, (PAIR, PAIR), 0) // HEAD
    lj = lax.broadcasted_iota(jnp.int32, (PAIR, PAIR), 1) // HEAD
    group = (li == lj).astype(BF16)

    def pair_steps(p):
        sl = slice(p * PAIR, (p + 1) * PAIR)
        r, k, v, lw, alr = r_ref[:, sl], k_ref[:, sl], v_ref[:, sl], lw_ref[:, sl], alr_ref[:, sl]
        run = _prefix_sum_rows(lw, tok_row)
        cl_end = run[ch - 1:ch]
        cl = jnp.where(fwd, run, cl_end - run + lw)
        kk_raw = k * kkp_ref[:, sl]
        kd = k * (1.0 + (alr - 1.0) * kap_ref[:, sl])
        sums = _dot(jnp.concatenate([kk_raw * kk_raw, r * kd * rkp_ref[:, sl]], axis=0).astype(BF16), group)
        yield
        kk = kk_raw / jnp.maximum(jnp.sqrt(sums[:ch]), 1e-12)
        bv = kk * alr
        bonus_ref[:, sl] = sums[ch:] * v

        e_neg = jnp.exp(-cl)
        e_end = jnp.exp(cl_end - cl)
        a_t = _stack(-kk * jnp.exp(cl - lw), lane_head)
        r_t = _stack(r * jnp.exp(cl), lane_head)
        b_t = _stack(bv * e_neg, lane_head)
        k_t = _stack(kd * e_neg, lane_head)
        b_e = _stack(bv * e_end, lane_head).T.astype(BF16)
        k_e = _stack(kd * e_end, lane_head).T.astype(BF16)
        v_s = _stack(v, lane_head).astype(BF16)

        aa = _dot_nt(jnp.concatenate([a_t, r_t], axis=0).astype(BF16),
                     jnp.concatenate([b_t, k_t], axis=0).astype(BF16))
        yield
        a_ab = jnp.where(strict, aa[:rows, :rows], 0.0)
        a_ak = jnp.where(strict, aa[:rows, rows:], 0.0).astype(BF16)
        a_rb = jnp.where(incl, aa[rows:, :rows], 0.0).astype(BF16)
        a_rk = jnp.where(incl, aa[rows:, rows:], 0.0).astype(BF16)
        zyh = _dot(jnp.concatenate([a_ak, a_rk, k_e], axis=0), v_s)

        n_pow = a_ab
        inv = jnp.where(eye, 1.0, 0.0) + a_ab
        for it in range(1, INV_STEPS):
            nb = n_pow.astype(BF16)
            if it == 1:
                n_pow = _dot(nb, nb)
            else:
                both = _dot(jnp.concatenate([inv, n_pow], axis=0).astype(BF16), nb)
                inv = inv + both[:rows]
                n_pow = both[rows:]
            yield
        if INV_STEPS > 1:
            inv = inv + _dot(inv.astype(BF16), n_pow.astype(BF16))
            yield
        qp = _dot(inv.astype(BF16), jnp.concatenate([a_t, zyh[:rows]], axis=1).astype(BF16)).astype(BF16)
        yield
        rg = _dot(jnp.concatenate([a_rb, b_e], axis=0), qp)
        yield
        r_hat = r_t + rg[:rows, :PAIR]
        g_mat = jnp.where(eye, jnp.exp(cl_end), 0.0) + rg[rows:, :PAIR]
        ys = _dot(jnp.concatenate([r_hat, g_mat], axis=0).astype(BF16), s_ref[p].astype(BF16))
        yield
        y_bd = ys[:rows] + rg[:rows, PAIR:] + zyh[rows:2 * rows]
        y_ref[:, sl] = y_bd[:ch] + y_bd[ch:]
        s_ref[p] = ys[rows:] + rg[rows:, PAIR:] + zyh[2 * rows:]

    n_pairs = d // PAIR
    for g0 in range(0, n_pairs, WKV_GROUP):
        live = [pair_steps(p) for p in range(g0, min(g0 + WKV_GROUP, n_pairs))]
        while live:
            live = [g for g in live if next(g, _DONE) is not _DONE]


def _wkv(rkv, lw, alr, kkp, kap, rkp):
    _, b, s, d = rkv.shape
    nc = s // CHUNK
    assert 2 * CHUNK == PAIR

    def tok(bi, dr, c):
        return jnp.where(dr == 0, c, nc - 1 - c)

    rkv_spec = lambda j: pl.BlockSpec((None, None, CHUNK, d), lambda bi, dr, c: (j, bi, tok(bi, dr, c), 0))
    dir_spec = pl.BlockSpec((None, None, CHUNK, d), lambda bi, dr, c: (dr, bi, tok(bi, dr, c), 0))
    par_spec = pl.BlockSpec((1, d), lambda bi, dr, c: (0, 0))
    return pl.pallas_call(
        _wkv_body,
        grid=(b, 2, nc),
        in_specs=[rkv_spec(0), rkv_spec(1), rkv_spec(2), dir_spec, dir_spec, par_spec, par_spec, par_spec],
        out_specs=[dir_spec, dir_spec],
        out_shape=[jax.ShapeDtypeStruct((2, b, s, d), F32), jax.ShapeDtypeStruct((2, b, s, d), F32)],
        scratch_shapes=[pltpu.VMEM((d // PAIR, PAIR, PAIR), F32)],
        compiler_params=_cparams(("parallel", "parallel", "arbitrary")),
        name="wkv",
    )(rkv, rkv, rkv, lw, alr, kkp, kap, rkp)


def _out_body(x_ref, y_ref, bonus_ref, gate_ref, lnw_ref, lnb_ref, wo_ref, o_ref, m_ref):
    d = x_ref.shape[1]
    li = lax.broadcasted_iota(jnp.int32, (PAIR, PAIR), 0) // HEAD
    lj = lax.broadcasted_iota(jnp.int32, (PAIR, PAIR), 1) // HEAD
    avg = jnp.where(li == lj, 1.0 / HEAD, 0.0).astype(BF16)
    for p in range(d // PAIR):
        sl = slice(p * PAIR, (p + 1) * PAIR)
        y = y_ref[0, :, sl] + y_ref[1, :, sl]
        h3 = _split3(y)
        mean = _dot(h3[0], avg) + _dot(h3[1], avg) + _dot(h3[2], avg)
        dev = y - mean
        var = _dot2(dev * dev, avg)
        gn = dev * lax.rsqrt(var + GN_EPS) * lnw_ref[:, sl] + lnb_ref[:, sl]
        out = (gn + bonus_ref[0, :, sl] + bonus_ref[1, :, sl]) * gate_ref[:, sl]
        m_ref[:, sl] = out.astype(BF16)
    o_ref[...] = x_ref[...] + _dot(m_ref[...], wo_ref[...])


def _rwkv_out(h, y, bonus, gate, lnw, lnb, wo, *, tm):
    b, s, d = h.shape
    tok = pl.BlockSpec((None, tm, d), lambda bi, i: (bi, i, 0))
    both = pl.BlockSpec((2, None, tm, d), lambda bi, i: (0, bi, i, 0))
    par = pl.BlockSpec((1, d), lambda bi, i: (0, 0))
    return pl.pallas_call(
        _out_body,
        grid=(b, s // tm),
        in_specs=[tok, both, both, tok, par, par, pl.BlockSpec((d, d), lambda bi, i: (0, 0))],
        out_specs=tok,
        out_shape=jax.ShapeDtypeStruct((b, s, d), F32),
        scratch_shapes=[pltpu.VMEM((tm, d), BF16)],
        compiler_params=_cparams(("parallel", "parallel")),
        name="rwkv_out",
    )(h, y, bonus, gate, lnw, lnb, wo)


def _pad_lora(w_in, w_out):
    r = w_in.shape[-1]
    pad = (-r) % 128
    w_in = jnp.pad(w_in, [(0, 0)] * (w_in.ndim - 1) + [(0, pad)])
    w_out = jnp.pad(w_out, [(0, 0)] * (w_out.ndim - 2) + [(0, pad), (0, 0)])
    return w_in.astype(BF16), w_out.astype(BF16)


def _tile(n, want):
    t = min(n, want)
    while n % t:
        t //= 2
    return t


def _rwkv_mixer(h, norm_g, mu, w_r, w_k, w_v, w_o, w0, w1, w2, a0, a1, a2, g1, g2, k_k, k_a, r_k, ln_w, ln_b):
    b, s, d = h.shape
    row = lambda z: z.reshape(1, d)
    tm = _tile(s, 512)
    rkv = _rwkv_rkv(h, norm_g, mu[jnp.array([0, 2, 3])].reshape(3, 1, d),
                    jnp.stack([w_r, w_k, w_v]).astype(BF16), tm=tm)
    w1p, w2p = _pad_lora(w1, w2)
    a1p, a2p = _pad_lora(a1, a2)
    lw, alr, gate = _rwkv_lora(h, norm_g, mu[jnp.array([1, 4, 5])].reshape(3, 1, d),
                               w1p, w2p, w0.reshape(2, 1, d), a1p, a2p, a0.reshape(2, 1, d),
                               g1.astype(BF16), g2.astype(BF16), tm=_tile(s, 256))
    y, bonus = _wkv(rkv, lw, alr, row(k_k), row(k_a), row(r_k))
    return _rwkv_out(h, y, bonus, gate, row(ln_w), row(ln_b), w_o.astype(BF16), tm=_tile(s, 256))


def kernel(x, ffn1_norm, ffn1_gate, ffn1_up, ffn1_down, mix_norm, ffn2_norm, ffn2_gate, ffn2_up, ffn2_down, pool_w, pool_scale, rwkv_mu, rwkv_wr, rwkv_wk, rwkv_wv, rwkv_wo, rwkv_w0, rwkv_w1, rwkv_w2, rwkv_a0, rwkv_a1, rwkv_a2, rwkv_g1, rwkv_g2, rwkv_kk, rwkv_ka, rwkv_rk, rwkv_lnw, rwkv_lnb, final_norm):
    b, s, d = x.shape
    depth = ffn1_norm.shape[0]
    f = ffn1_gate.shape[2]
    row = lambda z: z.reshape(1, d)
    tm_ffn, tf = _tile(b * s, 1024), _tile(f, 512)

    def ffn(h, norm_g, wg, wu, wd, final):
        out = _ffn(h.reshape(b * s, d), row(norm_g), wg.astype(BF16), wu.astype(BF16), wd.astype(BF16),
                   row(final_norm), final_norm=final, tm=tm_ffn, tf=tf)
        return out.reshape(b, s, d)

    h = x
    for i in range(depth):
        h = ffn(h, ffn1_norm[i], ffn1_gate[i], ffn1_up[i], ffn1_down[i], False)
        j = i // 2
        if i % 2 == 0:
            h = _pool(h, row(mix_norm[i]), pool_w[j].astype(BF16), row(pool_scale[j]), ts=_tile(s, 512))
        else:
            h = _rwkv_mixer(h, row(mix_norm[i]), rwkv_mu[j], rwkv_wr[j], rwkv_wk[j], rwkv_wv[j], rwkv_wo[j],
                            rwkv_w0[j], rwkv_w1[j], rwkv_w2[j], rwkv_a0[j], rwkv_a1[j], rwkv_a2[j],
                            rwkv_g1[j], rwkv_g2[j], rwkv_kk[j], rwkv_ka[j], rwkv_rk[j],
                            rwkv_lnw[j], rwkv_lnb[j])
        h = ffn(h, ffn2_norm[i], ffn2_gate[i], ffn2_up[i], ffn2_down[i], i == depth - 1)
    return h
```

```python
import functools
import math

import jax
import jax.numpy as jnp
from jax import lax
from jax.experimental import pallas as pl
from jax.experimental.pallas import tpu as pltpu

F32 = jnp.float32
BF16 = jnp.bfloat16

RMS_EPS = 1e-6
GN_EPS = 64e-5
HEAD = 64
PAIR = 2 * HEAD
CHUNK = 64
POOL_WINDOWS = (2, 4, 8, 16)
POOL_HALO = 16
SHIFT_HALO = 8
INV_STEPS = (CHUNK - 1).bit_length()
GN_LANES = 256
WKV_GROUP = 16
_DONE = object()
FFN_DOWN_COLS = 512
VMEM_LIMIT = 60 * 1024 * 1024


def _cparams(sem):
    return pltpu.CompilerParams(dimension_semantics=sem, vmem_limit_bytes=VMEM_LIMIT)


def _rms(x, g):
    return x * lax.rsqrt(jnp.mean(x * x, axis=-1, keepdims=True) + RMS_EPS) * g


def _sigmoid(x):
    return 0.5 * jnp.tanh(0.5 * x) + 0.5


def _dot(a, b):
    return jnp.dot(a, b, preferred_element_type=F32)


def _dot_nt(a, b):
    return lax.dot_general(a, b, (((1,), (1,)), ((), ())), preferred_element_type=F32)


def _dot_tn(a, b):
    return lax.dot_general(a, b, (((0,), (0,)), ((), ())), preferred_element_type=F32)


def _split2(x):
    hi = x.astype(BF16)
    lo = (x - hi.astype(F32)).astype(BF16)
    return hi, lo


def _split3(x):
    hi = x.astype(BF16)
    r1 = x - hi.astype(F32)
    mid = r1.astype(BF16)
    lo = (r1 - mid.astype(F32)).astype(BF16)
    return hi, mid, lo


def _dot2(x, m_bf16):
    hi, lo = _split2(x)
    return _dot(hi, m_bf16) + _dot(lo, m_bf16)


def _ffn_body(x_ref, g_ref, wg_ref, wu_ref, wd_ref, fg_ref, o_ref, n_ref, *, final_norm):
    f = pl.program_id(1)
    d = o_ref.shape[1]

    @pl.when(f == 0)
    def _():
        x = x_ref[...]
        n_ref[...] = _rms(x, g_ref[...]).astype(BF16)
        o_ref[...] = x

    n = n_ref[...]
    gate = _dot(n, wg_ref[...])
    up = _dot(n, wu_ref[...])
    act = (0.5 * gate * jax.nn.sigmoid(gate) * up).astype(BF16)
    for c0 in range(0, d, FFN_DOWN_COLS):
        cols = slice(c0, c0 + FFN_DOWN_COLS)
        o_ref[:, cols] += _dot(act, wd_ref[:, cols])

    if final_norm:
        @pl.when(f == pl.num_programs(1) - 1)
        def _():
            o_ref[...] = _rms(o_ref[...], fg_ref[...])


def _ffn(h, norm_g, wg, wu, wd, final_g, *, final_norm, tm, tf):
    t, d = h.shape
    f = wg.shape[1]
    return pl.pallas_call(
        functools.partial(_ffn_body, final_norm=final_norm),
        grid=(t // tm, f // tf),
        in_specs=[
            pl.BlockSpec((tm, d), lambda i, j: (i, 0)),
            pl.BlockSpec((1, d), lambda i, j: (0, 0)),
            pl.BlockSpec((d, tf), lambda i, j: (0, j)),
            pl.BlockSpec((d, tf), lambda i, j: (0, j)),
            pl.BlockSpec((tf, d), lambda i, j: (j, 0)),
            pl.BlockSpec((1, d), lambda i, j: (0, 0)),
        ],
        out_specs=pl.BlockSpec((tm, d), lambda i, j: (i, 0)),
        out_shape=jax.ShapeDtypeStruct((t, d), F32),
        scratch_shapes=[pltpu.VMEM((tm, d), BF16)],
        compiler_params=_cparams(("parallel", "arbitrary")),
        name="ffn",
    )(h, norm_g, wg, wu, wd, final_g)


def _pool_body(x_ref, xp_ref, xn_ref, g_ref, w_ref, sc_ref, o_ref, ext_ref, *, seq):
    i = pl.program_id(1)
    ts = x_ref.shape[0]
    cg = w_ref.shape[1]
    g = g_ref[...]
    first = i == 0
    last = i == pl.num_programs(1) - 1
    ext_ref[0:POOL_HALO, :] = jnp.where(first, 0.0, _rms(xp_ref[...], g))
    ext_ref[POOL_HALO:POOL_HALO + ts, :] = _rms(x_ref[...], g)
    ext_ref[POOL_HALO + ts:POOL_HALO + ts + POOL_HALO, :] = jnp.where(last, 0.0, _rms(xn_ref[...], g))

    t_glob = i * ts + lax.broadcasted_iota(jnp.int32, (ts, 1), 0)
    for gi, w in enumerate(POOL_WINDOWS):
        lanes = slice(gi * cg, (gi + 1) * cg)
        acc = ext_ref[POOL_HALO - w // 2:POOL_HALO - w // 2 + ts, lanes]
        for o in range(1, w):
            acc = acc + ext_ref[POOL_HALO - w // 2 + o:POOL_HALO - w // 2 + o + ts, lanes]
        lo = jnp.clip(t_glob - w // 2, 0, seq)
        hi = jnp.clip(t_glob - w // 2 + w, 0, seq)
        cnt = (hi - lo).astype(F32)
        pooled = acc / cnt - ext_ref[POOL_HALO:POOL_HALO + ts, lanes]
        y = _dot(pooled.astype(BF16), w_ref[gi])
        o_ref[:, lanes] = x_ref[:, lanes] + y * sc_ref[:, lanes]


def _pool(h, norm_g, w_grp, scale, *, ts):
    b, s, d = h.shape
    ng, cg, _ = w_grp.shape
    nb = ts // POOL_HALO
    last_blk = s // POOL_HALO - 1
    return pl.pallas_call(
        functools.partial(_pool_body, seq=s),
        grid=(b, s // ts),
        in_specs=[
            pl.BlockSpec((None, ts, d), lambda bi, i: (bi, i, 0)),
            pl.BlockSpec((None, POOL_HALO, d), lambda bi, i: (bi, jnp.maximum(i * nb - 1, 0), 0)),
            pl.BlockSpec((None, POOL_HALO, d), lambda bi, i: (bi, jnp.minimum((i + 1) * nb, last_blk), 0)),
            pl.BlockSpec((1, d), lambda bi, i: (0, 0)),
            pl.BlockSpec((ng, cg, cg), lambda bi, i: (0, 0, 0)),
            pl.BlockSpec((1, d), lambda bi, i: (0, 0)),
        ],
        out_specs=pl.BlockSpec((None, ts, d), lambda bi, i: (bi, i, 0)),
        out_shape=jax.ShapeDtypeStruct((b, s, d), F32),
        scratch_shapes=[pltpu.VMEM((ts + 2 * POOL_HALO, d), F32)],
        compiler_params=_cparams(("parallel", "parallel")),
        name="pool",
    )(h, h, h, norm_g, w_grp, scale)


def _norm_and_shift(x_ref, xp_ref, xn_ref, g, i, n_tiles):
    tm = x_ref.shape[0]
    hn = _rms(x_ref[...], g)
    prev_row = jnp.where(i == 0, 0.0, _rms(xp_ref[...], g)[SHIFT_HALO - 1:SHIFT_HALO])
    next_row = jnp.where(i == n_tiles - 1, 0.0, _rms(xn_ref[...], g)[0:1])
    row = lax.broadcasted_iota(jnp.int32, (tm, 1), 0)
    prev = jnp.where(row == 0, prev_row, pltpu.roll(hn, 1, axis=0))
    nxt = jnp.where(row == tm - 1, next_row, pltpu.roll(hn, tm - 1, axis=0))
    return hn, 0.5 * (prev + nxt) - hn


def _shift_specs(s, d, tm):
    nb = tm // SHIFT_HALO
    last_blk = s // SHIFT_HALO - 1
    return [
        pl.BlockSpec((None, tm, d), lambda bi, i, *_: (bi, i, 0)),
        pl.BlockSpec((None, SHIFT_HALO, d), lambda bi, i, *_: (bi, jnp.maximum(i * nb - 1, 0), 0)),
        pl.BlockSpec((None, SHIFT_HALO, d), lambda bi, i, *_: (bi, jnp.minimum((i + 1) * nb, last_blk), 0)),
    ]


def _rkv_body(x_ref, xp_ref, xn_ref, g_ref, mu_ref, w_ref, o_ref):
    hn, xx = _norm_and_shift(x_ref, xp_ref, xn_ref, g_ref[...], pl.program_id(1), pl.num_programs(1))
    for j in range(3):
        xm = (hn + xx * mu_ref[j]).astype(BF16)
        o_ref[j] = _dot(xm, w_ref[j]).astype(o_ref.dtype)


def _rwkv_rkv(h, norm_g, mu3, w3, *, tm):
    b, s, d = h.shape
    return pl.pallas_call(
        _rkv_body,
        grid=(b, s // tm),
        in_specs=_shift_specs(s, d, tm) + [
            pl.BlockSpec((1, d), lambda bi, i: (0, 0)),
            pl.BlockSpec((3, 1, d), lambda bi, i: (0, 0, 0)),
            pl.BlockSpec((3, d, d), lambda bi, i: (0, 0, 0)),
        ],
        out_specs=pl.BlockSpec((3, None, tm, d), lambda bi, i: (0, bi, i, 0)),
        out_shape=jax.ShapeDtypeStruct((3, b, s, d), BF16),
        compiler_params=_cparams(("parallel", "parallel")),
        name="rwkv_rkv",
    )(h, h, h, norm_g, mu3, w3)


def _lora_body(x_ref, xp_ref, xn_ref, g_ref, mu_ref, w1_ref, w2_ref, w0_ref, a1_ref, a2_ref, a0_ref,
               g1_ref, g2_ref, lw_ref, alr_ref, gate_ref):
    hn, xx = _norm_and_shift(x_ref, xp_ref, xn_ref, g_ref[...], pl.program_id(1), pl.num_programs(1))
    xw = (hn + xx * mu_ref[0]).astype(BF16)
    xa = (hn + xx * mu_ref[1]).astype(BF16)
    xg = (hn + xx * mu_ref[2]).astype(BF16)
    neg_rate = -math.exp(-0.5)
    for dr in range(2):
        z = w0_ref[dr] + _dot(jnp.tanh(_dot(xw, w1_ref[dr])).astype(BF16), w2_ref[dr])
        lw_ref[dr] = neg_rate * _sigmoid(z)
        alr = _sigmoid(a0_ref[dr] + _dot(_dot(xa, a1_ref[dr]).astype(BF16), a2_ref[dr]))
        alr_ref[dr] = alr.astype(alr_ref.dtype)
    gate = _dot(_sigmoid(_dot(xg, g1_ref[...])).astype(BF16), g2_ref[...])
    gate_ref[...] = gate.astype(gate_ref.dtype)


def _rwkv_lora(h, norm_g, mu3, w1, w2, w0, a1, a2, a0, g1, g2, *, tm):
    b, s, d = h.shape
    lw_, la_, lg_ = w1.shape[2], a1.shape[2], g1.shape[1]
    full = lambda *shape: pl.BlockSpec(shape, lambda bi, i: (0,) * len(shape))
    return pl.pallas_call(
        _lora_body,
        grid=(b, s // tm),
        in_specs=_shift_specs(s, d, tm) + [
            full(1, d), full(3, 1, d),
            full(2, d, lw_), full(2, lw_, d), full(2, 1, d),
            full(2, d, la_), full(2, la_, d), full(2, 1, d),
            full(d, lg_), full(lg_, d),
        ],
        out_specs=[
            pl.BlockSpec((2, None, tm, d), lambda bi, i: (0, bi, i, 0)),
            pl.BlockSpec((2, None, tm, d), lambda bi, i: (0, bi, i, 0)),
            pl.BlockSpec((None, tm, d), lambda bi, i: (bi, i, 0)),
        ],
        out_shape=[
            jax.ShapeDtypeStruct((2, b, s, d), F32),
            jax.ShapeDtypeStruct((2, b, s, d), BF16),
            jax.ShapeDtypeStruct((b, s, d), BF16),
        ],
        compiler_params=_cparams(("parallel", "parallel")),
        name="rwkv_lora",
    )(h, h, h, norm_g, mu3, w1, w2, w0, a1, a2, a0, g1, g2)


def _prefix_sum_rows(x, row_idx):
    shift = 1
    while shift < x.shape[0]:
        x = x + jnp.where(row_idx >= shift, pltpu.roll(x, shift, axis=0), 0.0)
        shift *= 2
    return x


def _head_groups(lanes, value):
    li = lax.broadcasted_iota(jnp.int32, (lanes, lanes), 0) // HEAD
    lj = lax.broadcasted_iota(jnp.int32, (lanes, lanes), 1) // HEAD
    return jnp.where(li == lj, value, 0.0).astype(BF16)


def _stack(x, lane_head):
    return jnp.concatenate([jnp.where(lane_head == 0, x, 0.0), jnp.where(lane_head == 1, x, 0.0)], axis=0)


def _wkv_body(r_ref, k_ref, v_ref, lw_ref, alr_ref, kkp_ref, kap_ref, rkp_ref, y_ref, bonus_ref, s_ref):
    c = pl.program_id(2)
    fwd = pl.program_id(1) == 0
    ch, d = r_ref.shape
    rows = 2 * ch

    @pl.when(c == 0)
    def _():
        s_ref[...] = jnp.zeros_like(s_ref)

    row = lax.broadcasted_iota(jnp.int32, (rows, rows), 0)
    col = lax.broadcasted_iota(jnp.int32, (rows, rows), 1)
    t_loc, s_loc = row % ch, col % ch
    same_head = (row // ch) == (col // ch)
    before = (t_loc - s_loc) * jnp.where(fwd, 1, -1) > 0
    strict = same_head & before
    incl = same_head & (before | (s_loc == t_loc))
    eye = row == col
    tok_row = lax.broadcasted_iota(jnp.int32, (ch, PAIR), 0)
    lane_head = lax.broadcasted_iota(jnp.int32, (ch, PAIR), 1) // HEAD
    group = _head_groups(2 * PAIR, 1.0)

    def pair_steps(p):
        sl = slice(p * PAIR, (p + 1) * PAIR)
        r, k, v = r_ref[:, sl].astype(F32), k_ref[:, sl].astype(F32), v_ref[:, sl].astype(F32)
        lw, alr = lw_ref[:, sl], alr_ref[:, sl].astype(F32)
        run = _prefix_sum_rows(lw, tok_row)
        cl_end = run[ch - 1:ch]
        cl = jnp.where(fwd, run, cl_end - run + lw)
        kk_raw = k * kkp_ref[:, sl]
        kd = k * (1.0 + (alr - 1.0) * kap_ref[:, sl])
        sums = _dot(jnp.concatenate([kk_raw * kk_raw, r * kd * rkp_ref[:, sl]], axis=1).astype(BF16), group)
        yield
        kk = kk_raw / jnp.maximum(jnp.sqrt(sums[:, :PAIR]), 1e-12)
        bv = kk * alr
        bonus_ref[:, sl] = (sums[:, PAIR:] * v).astype(bonus_ref.dtype)

        e_neg = jnp.exp(-cl)
        e_end = jnp.exp(cl_end - cl)
        a_t = _stack(-kk * jnp.exp(cl - lw), lane_head)
        r_t = _stack(r * jnp.exp(cl), lane_head)
        b_t = _stack(bv * e_neg, lane_head)
        k_t = _stack(kd * e_neg, lane_head)
        b_e = _stack(bv * e_end, lane_head).T.astype(BF16)
        k_e = _stack(kd * e_end, lane_head).T.astype(BF16)
        v_s = _stack(v, lane_head).astype(BF16)

        aa = _dot_nt(jnp.concatenate([a_t, r_t], axis=0).astype(BF16),
                     jnp.concatenate([b_t, k_t], axis=0).astype(BF16))
        yield
        a_ab = jnp.where(strict, aa[:rows, :rows], 0.0)
        a_ak = jnp.where(strict, aa[:rows, rows:], 0.0).astype(BF16)
        a_rb = jnp.where(incl, aa[rows:, :rows], 0.0).astype(BF16)
        a_rk = jnp.where(incl, aa[rows:, rows:], 0.0).astype(BF16)
        zyh = _dot(jnp.concatenate([a_ak, a_rk, k_e], axis=0), v_s)

        n_pow = a_ab
        inv = jnp.where(eye, 1.0, 0.0) + a_ab
        for it in range(1, INV_STEPS):
            nb = n_pow.astype(BF16)
            if it == 1:
                n_pow = _dot(nb, nb)
            else:
                both = _dot(jnp.concatenate([inv, n_pow], axis=0).astype(BF16), nb)
                inv = inv + both[:rows]
                n_pow = both[rows:]
            yield
        if INV_STEPS > 1:
            inv = inv + _dot(inv.astype(BF16), n_pow.astype(BF16))
            yield
        qp = _dot(inv.astype(BF16), jnp.concatenate([a_t, zyh[:rows]], axis=1).astype(BF16)).astype(BF16)
        yield
        rg = _dot(jnp.concatenate([a_rb, b_e], axis=0), qp)
        yield
        r_hat = r_t + rg[:rows, :PAIR]
        g_mat = jnp.where(eye, jnp.exp(cl_end), 0.0) + rg[rows:, :PAIR]
        ys = _dot(jnp.concatenate([r_hat, g_mat], axis=0).astype(BF16), s_ref[p].astype(BF16))
        yield
        y_bd = ys[:rows] + rg[:rows, PAIR:] + zyh[rows:2 * rows]
        y_ref[:, sl] = (y_bd[:ch] + y_bd[ch:]).astype(y_ref.dtype)
        s_ref[p] = ys[rows:] + rg[rows:, PAIR:] + zyh[2 * rows:]

    n_pairs = d // PAIR
    for g0 in range(0, n_pairs, WKV_GROUP):
        live = [pair_steps(p) for p in range(g0, min(g0 + WKV_GROUP, n_pairs))]
        while live:
            live = [g for g in live if next(g, _DONE) is not _DONE]


def _wkv(rkv, lw, alr, kkp, kap, rkp):
    _, b, s, d = rkv.shape
    nc = s // CHUNK
    assert 2 * CHUNK == PAIR

    def tok(bi, dr, c):
        return jnp.where(dr == 0, c, nc - 1 - c)

    rkv_spec = lambda j: pl.BlockSpec((None, None, CHUNK, d), lambda bi, dr, c: (j, bi, tok(bi, dr, c), 0))
    dir_spec = pl.BlockSpec((None, None, CHUNK, d), lambda bi, dr, c: (dr, bi, tok(bi, dr, c), 0))
    par_spec = pl.BlockSpec((1, d), lambda bi, dr, c: (0, 0))
    return pl.pallas_call(
        _wkv_body,
        grid=(b, 2, nc),
        in_specs=[rkv_spec(0), rkv_spec(1), rkv_spec(2), dir_spec, dir_spec, par_spec, par_spec, par_spec],
        out_specs=[dir_spec, dir_spec],
        out_shape=[jax.ShapeDtypeStruct((2, b, s, d), BF16), jax.ShapeDtypeStruct((2, b, s, d), BF16)],
        scratch_shapes=[pltpu.VMEM((d // PAIR, PAIR, PAIR), F32)],
        compiler_params=_cparams(("parallel", "parallel", "arbitrary")),
        name="wkv",
    )(rkv, rkv, rkv, lw, alr, kkp, kap, rkp)


def _out_body(x_ref, y_ref, bonus_ref, gate_ref, lnw_ref, lnb_ref, wo_ref, o_ref, m_ref):
    d = x_ref.shape[1]
    avg = _head_groups(GN_LANES, 1.0 / HEAD)
    for c0 in range(0, d, GN_LANES):
        sl = slice(c0, c0 + GN_LANES)
        y = y_ref[0, :, sl].astype(F32) + y_ref[1, :, sl].astype(F32)
        mean = _dot2(y, avg)
        dev = y - mean
        var = _dot((dev * dev).astype(BF16), avg)
        gn = dev * lax.rsqrt(var + GN_EPS) * lnw_ref[:, sl] + lnb_ref[:, sl]
        bonus = bonus_ref[0, :, sl].astype(F32) + bonus_ref[1, :, sl].astype(F32)
        m_ref[:, sl] = ((gn + bonus) * gate_ref[:, sl].astype(F32)).astype(BF16)
    o_ref[...] = x_ref[...] + _dot(m_ref[...], wo_ref[...])


def _rwkv_out(h, y, bonus, gate, lnw, lnb, wo, *, tm):
    b, s, d = h.shape
    tok = pl.BlockSpec((None, tm, d), lambda bi, i: (bi, i, 0))
    both = pl.BlockSpec((2, None, tm, d), lambda bi, i: (0, bi, i, 0))
    par = pl.BlockSpec((1, d), lambda bi, i: (0, 0))
    return pl.pallas_call(
        _out_body,
        grid=(b, s // tm),
        in_specs=[tok, both, both, tok, par, par, pl.BlockSpec((d, d), lambda bi, i: (0, 0))],
        out_specs=tok,
        out_shape=jax.ShapeDtypeStruct((b, s, d), F32),
        scratch_shapes=[pltpu.VMEM((tm, d), BF16)],
        compiler_params=_cparams(("parallel", "parallel")),
        name="rwkv_out",
    )(h, y, bonus, gate, lnw, lnb, wo)


def _pad_lora(w_in, w_out):
    r = w_in.shape[-1]
    pad = (-r) % 128
    w_in = jnp.pad(w_in, [(0, 0)] * (w_in.ndim - 1) + [(0, pad)])
    w_out = jnp.pad(w_out, [(0, 0)] * (w_out.ndim - 2) + [(0, pad), (0, 0)])
    return w_in.astype(BF16), w_out.astype(BF16)


def _tile(n, want):
    t = min(n, want)
    while n % t:
        t //= 2
    return t


def _rwkv_mixer(h, norm_g, mu, w_r, w_k, w_v, w_o, w0, w1, w2, a0, a1, a2, g1, g2, k_k, k_a, r_k, ln_w, ln_b):
    b, s, d = h.shape
    row = lambda z: z.reshape(1, d)
    tm = _tile(s, 256)
    rkv = _rwkv_rkv(h, norm_g, mu[jnp.array([0, 2, 3])].reshape(3, 1, d),
                    jnp.stack([w_r, w_k, w_v]).astype(BF16), tm=tm)
    w1p, w2p = _pad_lora(w1, w2)
    a1p, a2p = _pad_lora(a1, a2)
    lw, alr, gate = _rwkv_lora(h, norm_g, mu[jnp.array([1, 4, 5])].reshape(3, 1, d),
                               w1p, w2p, w0.reshape(2, 1, d), a1p, a2p, a0.reshape(2, 1, d),
                               g1.astype(BF16), g2.astype(BF16), tm=_tile(s, 256))
    y, bonus = _wkv(rkv, lw, alr, row(k_k), row(k_a), row(r_k))
    return _rwkv_out(h, y, bonus, gate, row(ln_w), row(ln_b), w_o.astype(BF16), tm=_tile(s, 256))


def kernel(x, ffn1_norm, ffn1_gate, ffn1_up, ffn1_down, mix_norm, ffn2_norm, ffn2_gate, ffn2_up, ffn2_down, pool_w, pool_scale, rwkv_mu, rwkv_wr, rwkv_wk, rwkv_wv, rwkv_wo, rwkv_w0, rwkv_w1, rwkv_w2, rwkv_a0, rwkv_a1, rwkv_a2, rwkv_g1, rwkv_g2, rwkv_kk, rwkv_ka, rwkv_rk, rwkv_lnw, rwkv_lnb, final_norm):
    b, s, d = x.shape
    depth = ffn1_norm.shape[0]
    f = ffn1_gate.shape[2]
    row = lambda z: z.reshape(1, d)
    tm_ffn, tf = _tile(b * s, 1024), _tile(f, 512)

    def ffn(h, norm_g, wg, wu, wd, final):
        out = _ffn(h.reshape(b * s, d), row(norm_g), wg.astype(BF16), wu.astype(BF16), wd.astype(BF16),
                   row(final_norm), final_norm=final, tm=tm_ffn, tf=tf)
        return out.reshape(b, s, d)

    h = x
    for i in range(depth):
        h = ffn(h, ffn1_norm[i], ffn1_gate[i], ffn1_up[i], ffn1_down[i], False)
        j = i // 2
        if i % 2 == 0:
            h = _pool(h, row(mix_norm[i]), pool_w[j].astype(BF16), row(pool_scale[j]), ts=_tile(s, 512))
        else:
            h = _rwkv_mixer(h, row(mix_norm[i]), rwkv_mu[j], rwkv_wr[j], rwkv_wk[j], rwkv_wv[j], rwkv_wo[j],
                            rwkv_w0[j], rwkv_w1[j], rwkv_w2[j], rwkv_a0[j], rwkv_a1[j], rwkv_a2[j],
                            rwkv_g1[j], rwkv_g2[j], rwkv_kk[j], rwkv_ka[j], rwkv_rk[j],
                            rwkv_lnw[j], rwkv_lnb[j])
        h = ffn(h, ffn2_norm[i], ffn2_gate[i], ffn2_up[i], ffn2_down[i], i == depth - 1)
    return h
```

```python
import functools
import math

import jax
import jax.numpy as jnp
from jax import lax
from jax.experimental import pallas as pl
from jax.experimental.pallas import tpu as pltpu

F32 = jnp.float32
BF16 = jnp.bfloat16

RMS_EPS = 1e-6
GN_EPS = 64e-5
HEAD = 64
PAIR = 2 * HEAD
CHUNK = 64
POOL_WINDOWS = (2, 4, 8, 16)
POOL_HALO = 16
SHIFT_HALO = 8
INV_STEPS = (CHUNK - 1).bit_length()
GN_LANES = 256
WKV_SEQS = 2
_DONE = object()
FFN_DOWN_COLS = 512
VMEM_LIMIT = 60 * 1024 * 1024


def _cparams(sem):
    return pltpu.CompilerParams(dimension_semantics=sem, vmem_limit_bytes=VMEM_LIMIT)


def _rms(x, g):
    return x * lax.rsqrt(jnp.mean(x * x, axis=-1, keepdims=True) + RMS_EPS) * g


def _sigmoid(x):
    return 0.5 * jnp.tanh(0.5 * x) + 0.5


def _dot(a, b):
    return jnp.dot(a, b, preferred_element_type=F32)


def _dot_nt(a, b):
    return lax.dot_general(a, b, (((1,), (1,)), ((), ())), preferred_element_type=F32)


def _dot_tn(a, b):
    return lax.dot_general(a, b, (((0,), (0,)), ((), ())), preferred_element_type=F32)


def _split2(x):
    hi = x.astype(BF16)
    lo = (x - hi.astype(F32)).astype(BF16)
    return hi, lo


def _split3(x):
    hi = x.astype(BF16)
    r1 = x - hi.astype(F32)
    mid = r1.astype(BF16)
    lo = (r1 - mid.astype(F32)).astype(BF16)
    return hi, mid, lo


def _dot2(x, m_bf16):
    hi, lo = _split2(x)
    return _dot(hi, m_bf16) + _dot(lo, m_bf16)


def _ffn_body(*refs, final_norm, n_cast):
    x_ref, g_ref, wg_ref, wu_ref, wd_ref, fg_ref = refs[:6]
    cast_src = refs[6:6 + n_cast]
    o_ref = refs[6 + n_cast]
    cast_dst = refs[7 + n_cast:7 + 2 * n_cast]
    n_ref = refs[7 + 2 * n_cast]
    f = pl.program_id(1)
    d = o_ref.shape[1]

    @pl.when(f == 0)
    def _():
        x = x_ref[...]
        n_ref[...] = _rms(x, g_ref[...]).astype(BF16)
        o_ref[...] = x

    n = n_ref[...]
    gate = _dot(n, wg_ref[...])
    up = _dot(n, wu_ref[...])
    act = (0.5 * gate * jax.nn.sigmoid(gate) * up).astype(BF16)
    for c0 in range(0, d, FFN_DOWN_COLS):
        cols = slice(c0, c0 + FFN_DOWN_COLS)
        o_ref[:, cols] += _dot(act, wd_ref[:, cols])

    for src, dst in zip(cast_src, cast_dst):
        dst[...] = src[...].astype(dst.dtype)

    if final_norm:
        @pl.when(f == pl.num_programs(1) - 1)
        def _():
            o_ref[...] = _rms(o_ref[...], fg_ref[...])


def _cast_specs(stacked, layer, ni, nj):
    _, rows, cols = stacked.shape
    ok = lambda n, parts, unit: n % parts == 0 and (n // parts) % unit == 0
    if ok(rows, ni, 16) and ok(cols, nj, 128):
        blk, imap = (rows // ni, cols // nj), lambda i, j: (i, j)
    elif ok(rows, nj, 16) and ok(cols, ni, 128):
        blk, imap = (rows // nj, cols // ni), lambda i, j: (j, i)
    else:
        assert ok(rows, ni, 16), (rows, ni)
        nc = 1
        while nc * 2 <= nj and ok(cols, nc * 2, 128):
            nc *= 2
        blk, imap = (rows // ni, cols // nc), lambda i, j: (i, jnp.minimum(j, nc - 1))
    in_spec = pl.BlockSpec((None,) + blk, lambda i, j: (layer,) + imap(i, j))
    return in_spec, pl.BlockSpec(blk, imap), jax.ShapeDtypeStruct((rows, cols), BF16)


def _ffn(h, norm_g, wg, wu, wd, final_g, casts, *, final_norm, tm, tf):
    t, d = h.shape
    f = wg.shape[1]
    ni, nj = t // tm, f // tf
    specs = [_cast_specs(w, layer, ni, nj) for w, layer in casts]
    outs = pl.pallas_call(
        functools.partial(_ffn_body, final_norm=final_norm, n_cast=len(casts)),
        grid=(ni, nj),
        in_specs=[
            pl.BlockSpec((tm, d), lambda i, j: (i, 0)),
            pl.BlockSpec((1, d), lambda i, j: (0, 0)),
            pl.BlockSpec((d, tf), lambda i, j: (0, j)),
            pl.BlockSpec((d, tf), lambda i, j: (0, j)),
            pl.BlockSpec((tf, d), lambda i, j: (j, 0)),
            pl.BlockSpec((1, d), lambda i, j: (0, 0)),
        ] + [sp[0] for sp in specs],
        out_specs=[pl.BlockSpec((tm, d), lambda i, j: (i, 0))] + [sp[1] for sp in specs],
        out_shape=[jax.ShapeDtypeStruct((t, d), F32)] + [sp[2] for sp in specs],
        scratch_shapes=[pltpu.VMEM((tm, d), BF16)],
        compiler_params=_cparams(("parallel", "arbitrary")),
        name="ffn",
    )(h, norm_g, wg, wu, wd, final_g, *[w for w, _ in casts])
    return outs[0], list(outs[1:])


def _pool_body(x_ref, xp_ref, xn_ref, g_ref, w_ref, sc_ref, o_ref, ext_ref, *, seq):
    i = pl.program_id(1)
    ts = x_ref.shape[0]
    cg = w_ref.shape[1]
    g = g_ref[...]
    first = i == 0
    last = i == pl.num_programs(1) - 1
    ext_ref[0:POOL_HALO, :] = jnp.where(first, 0.0, _rms(xp_ref[...], g))
    ext_ref[POOL_HALO:POOL_HALO + ts, :] = _rms(x_ref[...], g)
    ext_ref[POOL_HALO + ts:POOL_HALO + ts + POOL_HALO, :] = jnp.where(last, 0.0, _rms(xn_ref[...], g))

    t_glob = i * ts + lax.broadcasted_iota(jnp.int32, (ts, 1), 0)
    for gi, w in enumerate(POOL_WINDOWS):
        lanes = slice(gi * cg, (gi + 1) * cg)
        acc = ext_ref[POOL_HALO - w // 2:POOL_HALO - w // 2 + ts, lanes]
        for o in range(1, w):
            acc = acc + ext_ref[POOL_HALO - w // 2 + o:POOL_HALO - w // 2 + o + ts, lanes]
        lo = jnp.clip(t_glob - w // 2, 0, seq)
        hi = jnp.clip(t_glob - w // 2 + w, 0, seq)
        cnt = (hi - lo).astype(F32)
        pooled = acc / cnt - ext_ref[POOL_HALO:POOL_HALO + ts, lanes]
        y = _dot(pooled.astype(BF16), w_ref[gi])
        o_ref[:, lanes] = x_ref[:, lanes] + y * sc_ref[:, lanes]


def _pool(h, norm_g, w_grp, scale, *, ts):
    b, s, d = h.shape
    ng, cg, _ = w_grp.shape
    nb = ts // POOL_HALO
    last_blk = s // POOL_HALO - 1
    return pl.pallas_call(
        functools.partial(_pool_body, seq=s),
        grid=(b, s // ts),
        in_specs=[
            pl.BlockSpec((None, ts, d), lambda bi, i: (bi, i, 0)),
            pl.BlockSpec((None, POOL_HALO, d), lambda bi, i: (bi, jnp.maximum(i * nb - 1, 0), 0)),
            pl.BlockSpec((None, POOL_HALO, d), lambda bi, i: (bi, jnp.minimum((i + 1) * nb, last_blk), 0)),
            pl.BlockSpec((1, d), lambda bi, i: (0, 0)),
            pl.BlockSpec((ng, cg, cg), lambda bi, i: (0, 0, 0)),
            pl.BlockSpec((1, d), lambda bi, i: (0, 0)),
        ],
        out_specs=pl.BlockSpec((None, ts, d), lambda bi, i: (bi, i, 0)),
        out_shape=jax.ShapeDtypeStruct((b, s, d), F32),
        scratch_shapes=[pltpu.VMEM((ts + 2 * POOL_HALO, d), F32)],
        compiler_params=_cparams(("parallel", "parallel")),
        name="pool",
    )(h, h, h, norm_g, w_grp, scale)


def _norm_and_shift(x_ref, xp_ref, xn_ref, g, i, n_tiles):
    tm = x_ref.shape[0]
    hn = _rms(x_ref[...], g)
    prev_row = jnp.where(i == 0, 0.0, _rms(xp_ref[...], g)[SHIFT_HALO - 1:SHIFT_HALO])
    next_row = jnp.where(i == n_tiles - 1, 0.0, _rms(xn_ref[...], g)[0:1])
    row = lax.broadcasted_iota(jnp.int32, (tm, 1), 0)
    prev = jnp.where(row == 0, prev_row, pltpu.roll(hn, 1, axis=0))
    nxt = jnp.where(row == tm - 1, next_row, pltpu.roll(hn, tm - 1, axis=0))
    return hn, 0.5 * (prev + nxt) - hn


def _shift_specs(s, d, tm):
    nb = tm // SHIFT_HALO
    last_blk = s // SHIFT_HALO - 1
    return [
        pl.BlockSpec((None, tm, d), lambda bi, i, *_: (bi, i, 0)),
        pl.BlockSpec((None, SHIFT_HALO, d), lambda bi, i, *_: (bi, jnp.maximum(i * nb - 1, 0), 0)),
        pl.BlockSpec((None, SHIFT_HALO, d), lambda bi, i, *_: (bi, jnp.minimum((i + 1) * nb, last_blk), 0)),
    ]


def _rkv_body(x_ref, xp_ref, xn_ref, g_ref, mu_ref, wr_ref, wk_ref, wv_ref, o_ref):
    hn, xx = _norm_and_shift(x_ref, xp_ref, xn_ref, g_ref[...], pl.program_id(1), pl.num_programs(1))
    for j, w_ref in enumerate((wr_ref, wk_ref, wv_ref)):
        xm = (hn + xx * mu_ref[j]).astype(BF16)
        o_ref[j] = _dot(xm, w_ref[...]).astype(o_ref.dtype)


def _rwkv_rkv(h, norm_g, mu3, w_r, w_k, w_v, *, tm):
    b, s, d = h.shape
    return pl.pallas_call(
        _rkv_body,
        grid=(b, s // tm),
        in_specs=_shift_specs(s, d, tm) + [
            pl.BlockSpec((1, d), lambda bi, i: (0, 0)),
            pl.BlockSpec((3, 1, d), lambda bi, i: (0, 0, 0)),
        ] + [pl.BlockSpec((d, d), lambda bi, i: (0, 0))] * 3,
        out_specs=pl.BlockSpec((3, None, tm, d), lambda bi, i: (0, bi, i, 0)),
        out_shape=jax.ShapeDtypeStruct((3, b, s, d), BF16),
        compiler_params=_cparams(("parallel", "parallel")),
        name="rwkv_rkv",
    )(h, h, h, norm_g, mu3, w_r, w_k, w_v)


def _lora_body(x_ref, xp_ref, xn_ref, g_ref, mu_ref, w1_ref, w2_ref, w0_ref, a1_ref, a2_ref, a0_ref,
               g1_ref, g2_ref, lw_ref, alr_ref, gate_ref):
    hn, xx = _norm_and_shift(x_ref, xp_ref, xn_ref, g_ref[...], pl.program_id(1), pl.num_programs(1))
    xw = (hn + xx * mu_ref[0]).astype(BF16)
    xa = (hn + xx * mu_ref[1]).astype(BF16)
    xg = (hn + xx * mu_ref[2]).astype(BF16)
    neg_rate = -math.exp(-0.5)
    for dr in range(2):
        z = w0_ref[dr] + _dot(jnp.tanh(_dot(xw, w1_ref[dr])).astype(BF16), w2_ref[dr])
        lw_ref[dr] = neg_rate * _sigmoid(z)
        alr = _sigmoid(a0_ref[dr] + _dot(_dot(xa, a1_ref[dr]).astype(BF16), a2_ref[dr]))
        alr_ref[dr] = alr.astype(alr_ref.dtype)
    gate = _dot(_sigmoid(_dot(xg, g1_ref[...])).astype(BF16), g2_ref[...])
    gate_ref[...] = gate.astype(gate_ref.dtype)


def _rwkv_lora(h, norm_g, mu3, w1, w2, w0, a1, a2, a0, g1, g2, *, tm):
    b, s, d = h.shape
    lw_, la_, lg_ = w1.shape[2], a1.shape[2], g1.shape[1]
    full = lambda *shape: pl.BlockSpec(shape, lambda bi, i: (0,) * len(shape))
    return pl.pallas_call(
        _lora_body,
        grid=(b, s // tm),
        in_specs=_shift_specs(s, d, tm) + [
            full(1, d), full(3, 1, d),
            full(2, d, lw_), full(2, lw_, d), full(2, 1, d),
            full(2, d, la_), full(2, la_, d), full(2, 1, d),
            full(d, lg_), full(lg_, d),
        ],
        out_specs=[
            pl.BlockSpec((2, None, tm, d), lambda bi, i: (0, bi, i, 0)),
            pl.BlockSpec((2, None, tm, d), lambda bi, i: (0, bi, i, 0)),
            pl.BlockSpec((None, tm, d), lambda bi, i: (bi, i, 0)),
        ],
        out_shape=[
            jax.ShapeDtypeStruct((2, b, s, d), F32),
            jax.ShapeDtypeStruct((2, b, s, d), BF16),
            jax.ShapeDtypeStruct((b, s, d), BF16),
        ],
        compiler_params=_cparams(("parallel", "parallel")),
        name="rwkv_lora",
    )(h, h, h, norm_g, mu3, w1, w2, w0, a1, a2, a0, g1, g2)


def _prefix_sum_rows(x, row_idx):
    shift = 1
    while shift < x.shape[0]:
        x = x + jnp.where(row_idx >= shift, pltpu.roll(x, shift, axis=0), 0.0)
        shift *= 2
    return x


def _head_groups(lanes, value):
    li = lax.broadcasted_iota(jnp.int32, (lanes, lanes), 0) // HEAD
    lj = lax.broadcasted_iota(jnp.int32, (lanes, lanes), 1) // HEAD
    return jnp.where(li == lj, value, 0.0).astype(BF16)


def _stack(x, lane_head):
    return jnp.concatenate([jnp.where(lane_head == 0, x, 0.0), jnp.where(lane_head == 1, x, 0.0)], axis=0)


def _wkv_body(r_ref, k_ref, v_ref, lw_ref, alr_ref, kkp_ref, kap_ref, rkp_ref, y_ref, bonus_ref, s_ref):
    c = pl.program_id(2)
    fwd = pl.program_id(1) == 0
    nb, ch, d = r_ref.shape
    rows = 2 * ch
    n_pairs = d // PAIR

    @pl.when(c == 0)
    def _():
        s_ref[...] = jnp.zeros_like(s_ref)

    row = lax.broadcasted_iota(jnp.int32, (rows, rows), 0)
    col = lax.broadcasted_iota(jnp.int32, (rows, rows), 1)
    t_loc, s_loc = row % ch, col % ch
    same_head = (row // ch) == (col // ch)
    before = (t_loc - s_loc) * jnp.where(fwd, 1, -1) > 0
    strict = same_head & before
    incl = same_head & (before | (s_loc == t_loc))
    eye = row == col
    tok_row = lax.broadcasted_iota(jnp.int32, (ch, PAIR), 0)
    lane_head = lax.broadcasted_iota(jnp.int32, (ch, PAIR), 1) // HEAD
    group = _head_groups(2 * PAIR, 1.0)

    def pair_steps(bb, p):
        sl = slice(p * PAIR, (p + 1) * PAIR)
        r, k, v = r_ref[bb, :, sl].astype(F32), k_ref[bb, :, sl].astype(F32), v_ref[bb, :, sl].astype(F32)
        lw, alr = lw_ref[bb, :, sl], alr_ref[bb, :, sl].astype(F32)
        run = _prefix_sum_rows(lw, tok_row)
        cl_end = run[ch - 1:ch]
        cl = jnp.where(fwd, run, cl_end - run + lw)
        kk_raw = k * kkp_ref[:, sl]
        kd = k * (1.0 + (alr - 1.0) * kap_ref[:, sl])
        sums = _dot(jnp.concatenate([kk_raw * kk_raw, r * kd * rkp_ref[:, sl]], axis=1).astype(BF16), group)
        yield
        kk = kk_raw / jnp.maximum(jnp.sqrt(sums[:, :PAIR]), 1e-12)
        bv = kk * alr
        bonus_ref[bb, :, sl] = (sums[:, PAIR:] * v).astype(bonus_ref.dtype)

        e_neg = jnp.exp(-cl)
        e_end = jnp.exp(cl_end - cl)
        a_t = _stack(-kk * jnp.exp(cl - lw), lane_head)
        r_t = _stack(r * jnp.exp(cl), lane_head)
        b_t = _stack(bv * e_neg, lane_head)
        k_t = _stack(kd * e_neg, lane_head)
        b_e = _stack(bv * e_end, lane_head).T.astype(BF16)
        k_e = _stack(kd * e_end, lane_head).T.astype(BF16)
        v_s = _stack(v, lane_head).astype(BF16)

        aa = _dot_nt(jnp.concatenate([a_t, r_t], axis=0).astype(BF16),
                     jnp.concatenate([b_t, k_t], axis=0).astype(BF16))
        yield
        a_ab = jnp.where(strict, aa[:rows, :rows], 0.0)
        a_ak = jnp.where(strict, aa[:rows, rows:], 0.0).astype(BF16)
        a_rb = jnp.where(incl, aa[rows:, :rows], 0.0).astype(BF16)
        a_rk = jnp.where(incl, aa[rows:, rows:], 0.0).astype(BF16)
        zyh = _dot(jnp.concatenate([a_ak, a_rk, k_e], axis=0), v_s)

        n_pow = a_ab
        inv = jnp.where(eye, 1.0, 0.0) + a_ab
        for it in range(1, INV_STEPS):
            nb = n_pow.astype(BF16)
            if it == 1:
                n_pow = _dot(nb, nb)
            else:
                both = _dot(jnp.concatenate([inv, n_pow], axis=0).astype(BF16), nb)
                inv = inv + both[:rows]
                n_pow = both[rows:]
            yield
        if INV_STEPS > 1:
            inv = inv + _dot(inv.astype(BF16), n_pow.astype(BF16))
            yield
        qp = _dot(inv.astype(BF16), jnp.concatenate([a_t, zyh[:rows]], axis=1).astype(BF16)).astype(BF16)
        yield
        rg = _dot(jnp.concatenate([a_rb, b_e], axis=0), qp)
        yield
        r_hat = r_t + rg[:rows, :PAIR]
        g_mat = jnp.where(eye, jnp.exp(cl_end), 0.0) + rg[rows:, :PAIR]
        ys = _dot(jnp.concatenate([r_hat, g_mat], axis=0).astype(BF16), s_ref[bb * n_pairs + p].astype(BF16))
        yield
        y_bd = ys[:rows] + rg[:rows, PAIR:] + zyh[rows:2 * rows]
        y_ref[bb, :, sl] = (y_bd[:ch] + y_bd[ch:]).astype(y_ref.dtype)
        s_ref[bb * n_pairs + p] = ys[rows:] + rg[rows:, PAIR:] + zyh[2 * rows:]

    live = [pair_steps(bb, p) for bb in range(nb) for p in range(n_pairs)]
    while live:
        live = [g for g in live if next(g, _DONE) is not _DONE]


def _wkv(rkv, lw, alr, kkp, kap, rkp):
    _, b, s, d = rkv.shape
    nc = s // CHUNK
    assert 2 * CHUNK == PAIR

    def tok(bi, dr, c):
        return jnp.where(dr == 0, c, nc - 1 - c)

    nb = math.gcd(b, WKV_SEQS)
    rkv_spec = lambda j: pl.BlockSpec((None, nb, CHUNK, d), lambda bi, dr, c: (j, bi, tok(bi, dr, c), 0))
    dir_spec = pl.BlockSpec((None, nb, CHUNK, d), lambda bi, dr, c: (dr, bi, tok(bi, dr, c), 0))
    par_spec = pl.BlockSpec((1, d), lambda bi, dr, c: (0, 0))
    return pl.pallas_call(
        _wkv_body,
        grid=(b // nb, 2, nc),
        in_specs=[rkv_spec(0), rkv_spec(1), rkv_spec(2), dir_spec, dir_spec, par_spec, par_spec, par_spec],
        out_specs=[dir_spec, dir_spec],
        out_shape=[jax.ShapeDtypeStruct((2, b, s, d), BF16), jax.ShapeDtypeStruct((2, b, s, d), BF16)],
        scratch_shapes=[pltpu.VMEM((nb * (d // PAIR), PAIR, PAIR), F32)],
        compiler_params=_cparams(("parallel", "parallel", "arbitrary")),
        name="wkv",
    )(rkv, rkv, rkv, lw, alr, kkp, kap, rkp)


def _out_body(x_ref, y_ref, bonus_ref, gate_ref, lnw_ref, lnb_ref, wo_ref, o_ref, m_ref):
    d = x_ref.shape[1]
    avg = _head_groups(GN_LANES, 1.0 / HEAD)
    for c0 in range(0, d, GN_LANES):
        sl = slice(c0, c0 + GN_LANES)
        y = y_ref[0, :, sl].astype(F32) + y_ref[1, :, sl].astype(F32)
        mean = _dot2(y, avg)
        dev = y - mean
        var = _dot((dev * dev).astype(BF16), avg)
        gn = dev * lax.rsqrt(var + GN_EPS) * lnw_ref[:, sl] + lnb_ref[:, sl]
        bonus = bonus_ref[0, :, sl].astype(F32) + bonus_ref[1, :, sl].astype(F32)
        m_ref[:, sl] = ((gn + bonus) * gate_ref[:, sl].astype(F32)).astype(BF16)
    o_ref[...] = x_ref[...] + _dot(m_ref[...], wo_ref[...])


def _rwkv_out(h, y, bonus, gate, lnw, lnb, wo, *, tm):
    b, s, d = h.shape
    tok = pl.BlockSpec((None, tm, d), lambda bi, i: (bi, i, 0))
    both = pl.BlockSpec((2, None, tm, d), lambda bi, i: (0, bi, i, 0))
    par = pl.BlockSpec((1, d), lambda bi, i: (0, 0))
    return pl.pallas_call(
        _out_body,
        grid=(b, s // tm),
        in_specs=[tok, both, both, tok, par, par, pl.BlockSpec((d, d), lambda bi, i: (0, 0))],
        out_specs=tok,
        out_shape=jax.ShapeDtypeStruct((b, s, d), F32),
        scratch_shapes=[pltpu.VMEM((tm, d), BF16)],
        compiler_params=_cparams(("parallel", "parallel")),
        name="rwkv_out",
    )(h, y, bonus, gate, lnw, lnb, wo)


def _pad_lora(w_in, w_out):
    r = w_in.shape[-1]
    pad = (-r) % 128
    w_in = jnp.pad(w_in, [(0, 0)] * (w_in.ndim - 1) + [(0, pad)])
    w_out = jnp.pad(w_out, [(0, 0)] * (w_out.ndim - 2) + [(0, pad), (0, 0)])
    return w_in.astype(BF16), w_out.astype(BF16)


def _tile(n, want):
    t = min(n, want)
    while n % t:
        t //= 2
    return t


def _rwkv_mixer(h, norm_g, mu, w_r, w_k, w_v, w_o, w0, w1, w2, a0, a1, a2, g1, g2, k_k, k_a, r_k, ln_w, ln_b):
    b, s, d = h.shape
    row = lambda z: z.reshape(1, d)
    tm = _tile(s, 256)
    rkv = _rwkv_rkv(h, norm_g, mu[jnp.array([0, 2, 3])].reshape(3, 1, d), w_r, w_k, w_v, tm=tm)
    w1p, w2p = _pad_lora(w1, w2)
    a1p, a2p = _pad_lora(a1, a2)
    lw, alr, gate = _rwkv_lora(h, norm_g, mu[jnp.array([1, 4, 5])].reshape(3, 1, d),
                               w1p, w2p, w0.reshape(2, 1, d), a1p, a2p, a0.reshape(2, 1, d),
                               g1.astype(BF16), g2.astype(BF16), tm=tm)
    y, bonus = _wkv(rkv, lw, alr, row(k_k), row(k_a), row(r_k))
    return _rwkv_out(h, y, bonus, gate, row(ln_w), row(ln_b), w_o, tm=tm)


def kernel(x, ffn1_norm, ffn1_gate, ffn1_up, ffn1_down, mix_norm, ffn2_norm, ffn2_gate, ffn2_up, ffn2_down, pool_w, pool_scale, rwkv_mu, rwkv_wr, rwkv_wk, rwkv_wv, rwkv_wo, rwkv_w0, rwkv_w1, rwkv_w2, rwkv_a0, rwkv_a1, rwkv_a2, rwkv_g1, rwkv_g2, rwkv_kk, rwkv_ka, rwkv_rk, rwkv_lnw, rwkv_lnb, final_norm):
    b, s, d = x.shape
    depth = ffn1_norm.shape[0]
    f = ffn1_gate.shape[2]
    row = lambda z: z.reshape(1, d)
    tm_ffn, tf = _tile(b * s, 1024), _tile(f, 512)
    ffn_params = ((ffn1_norm, ffn1_gate, ffn1_up, ffn1_down), (ffn2_norm, ffn2_gate, ffn2_up, ffn2_down))
    order = [(i, half) for i in range(depth) for half in (0, 1)]

    weights = [ffn_params[0][k][0].astype(BF16) for k in (1, 2, 3)]
    rwkv_w = None
    h = x
    for q, (i, half) in enumerate(order):
        j = i // 2
        if half == 1:
            if i % 2 == 0:
                h = _pool(h, row(mix_norm[i]), pool_w[j].astype(BF16), row(pool_scale[j]), ts=_tile(s, 512))
            else:
                h = _rwkv_mixer(h, row(mix_norm[i]), rwkv_mu[j], *rwkv_w,
                                rwkv_w0[j], rwkv_w1[j], rwkv_w2[j], rwkv_a0[j], rwkv_a1[j], rwkv_a2[j],
                                rwkv_g1[j], rwkv_g2[j], rwkv_kk[j], rwkv_ka[j], rwkv_rk[j],
                                rwkv_lnw[j], rwkv_lnb[j])
        casts = []
        if q + 1 < len(order):
            ni, nhalf = order[q + 1]
            casts += [(ffn_params[nhalf][k], ni) for k in (1, 2, 3)]
        if half == 0 and i % 2 == 1:
            casts += [(w, j) for w in (rwkv_wr, rwkv_wk, rwkv_wv, rwkv_wo)]
        out, cast = _ffn(h.reshape(b * s, d), row(ffn_params[half][0][i]), *weights, row(final_norm), casts,
                         final_norm=q == len(order) - 1, tm=tm_ffn, tf=tf)
        h = out.reshape(b, s, d)
        if q + 1 < len(order):
            weights, cast = cast[:3], cast[3:]
        if cast:
            rwkv_w = cast
    return h
```

```python
import functools
import math

import jax
import jax.numpy as jnp
from jax import lax
from jax.experimental import pallas as pl
from jax.experimental.pallas import tpu as pltpu

F32 = jnp.float32
BF16 = jnp.bfloat16

RMS_EPS = 1e-6
GN_EPS = 64e-5
HEAD = 64
PAIR = 2 * HEAD
CHUNK = 64
POOL_WINDOWS = (2, 4, 8, 16)
POOL_HALO = 16
SHIFT_HALO = 8
INV_STEPS = (CHUNK - 1).bit_length()
GN_LANES = 256
WKV_SEQS = 4
WKV_WAVE = 2
WKV_SKEW = 1
_DONE = object()
FFN_DOWN_COLS = 512
VMEM_LIMIT = 60 * 1024 * 1024


def _cparams(sem):
    return pltpu.CompilerParams(dimension_semantics=sem, vmem_limit_bytes=VMEM_LIMIT)


def _rms(x, g):
    return x * lax.rsqrt(jnp.mean(x * x, axis=-1, keepdims=True) + RMS_EPS) * g


def _sigmoid(x):
    return 0.5 * jnp.tanh(0.5 * x) + 0.5


def _dot(a, b):
    return jnp.dot(a, b, preferred_element_type=F32)


def _dot_nt(a, b):
    return lax.dot_general(a, b, (((1,), (1,)), ((), ())), preferred_element_type=F32)


def _dot_tn(a, b):
    return lax.dot_general(a, b, (((0,), (0,)), ((), ())), preferred_element_type=F32)


def _split2(x):
    hi = x.astype(BF16)
    lo = (x - hi.astype(F32)).astype(BF16)
    return hi, lo


def _split3(x):
    hi = x.astype(BF16)
    r1 = x - hi.astype(F32)
    mid = r1.astype(BF16)
    lo = (r1 - mid.astype(F32)).astype(BF16)
    return hi, mid, lo


def _dot2(x, m_bf16):
    hi, lo = _split2(x)
    return _dot(hi, m_bf16) + _dot(lo, m_bf16)


def _ffn_body(*refs, final_norm, n_cast):
    x_ref, g_ref, wg_ref, wu_ref, wd_ref, fg_ref = refs[:6]
    cast_src = refs[6:6 + n_cast]
    o_ref = refs[6 + n_cast]
    cast_dst = refs[7 + n_cast:7 + 2 * n_cast]
    n_ref = refs[7 + 2 * n_cast]
    f = pl.program_id(1)
    d = o_ref.shape[1]

    @pl.when(f == 0)
    def _():
        x = x_ref[...]
        n_ref[...] = _rms(x, g_ref[...]).astype(BF16)
        o_ref[...] = x

    n = n_ref[...]
    gate = _dot(n, wg_ref[...])
    up = _dot(n, wu_ref[...])
    act = (0.5 * gate * jax.nn.sigmoid(gate) * up).astype(BF16)
    for c0 in range(0, d, FFN_DOWN_COLS):
        cols = slice(c0, c0 + FFN_DOWN_COLS)
        o_ref[:, cols] += _dot(act, wd_ref[:, cols])

    for src, dst in zip(cast_src, cast_dst):
        dst[...] = src[...].astype(dst.dtype)

    if final_norm:
        @pl.when(f == pl.num_programs(1) - 1)
        def _():
            o_ref[...] = _rms(o_ref[...], fg_ref[...])


def _cast_specs(stacked, layer, ni, nj):
    _, rows, cols = stacked.shape
    ok = lambda n, parts, unit: n % parts == 0 and (n // parts) % unit == 0
    if ok(rows, ni, 16) and ok(cols, nj, 128):
        blk, imap = (rows // ni, cols // nj), lambda i, j: (i, j)
    elif ok(rows, nj, 16) and ok(cols, ni, 128):
        blk, imap = (rows // nj, cols // ni), lambda i, j: (j, i)
    else:
        assert ok(rows, ni, 16), (rows, ni)
        nc = 1
        while nc * 2 <= nj and ok(cols, nc * 2, 128):
            nc *= 2
        blk, imap = (rows // ni, cols // nc), lambda i, j: (i, jnp.minimum(j, nc - 1))
    in_spec = pl.BlockSpec((None,) + blk, lambda i, j: (layer,) + imap(i, j))
    return in_spec, pl.BlockSpec(blk, imap), jax.ShapeDtypeStruct((rows, cols), BF16)


def _ffn(h, norm_g, wg, wu, wd, final_g, casts, *, final_norm, tm, tf):
    t, d = h.shape
    f = wg.shape[1]
    ni, nj = t // tm, f // tf
    specs = [_cast_specs(w, layer, ni, nj) for w, layer in casts]
    outs = pl.pallas_call(
        functools.partial(_ffn_body, final_norm=final_norm, n_cast=len(casts)),
        grid=(ni, nj),
        in_specs=[
            pl.BlockSpec((tm, d), lambda i, j: (i, 0)),
            pl.BlockSpec((1, d), lambda i, j: (0, 0)),
            pl.BlockSpec((d, tf), lambda i, j: (0, j)),
            pl.BlockSpec((d, tf), lambda i, j: (0, j)),
            pl.BlockSpec((tf, d), lambda i, j: (j, 0)),
            pl.BlockSpec((1, d), lambda i, j: (0, 0)),
        ] + [sp[0] for sp in specs],
        out_specs=[pl.BlockSpec((tm, d), lambda i, j: (i, 0))] + [sp[1] for sp in specs],
        out_shape=[jax.ShapeDtypeStruct((t, d), F32)] + [sp[2] for sp in specs],
        scratch_shapes=[pltpu.VMEM((tm, d), BF16)],
        compiler_params=_cparams(("parallel", "arbitrary")),
        name="ffn",
    )(h, norm_g, wg, wu, wd, final_g, *[w for w, _ in casts])
    return outs[0], list(outs[1:])


def _pool_body(x_ref, xp_ref, xn_ref, g_ref, w_ref, sc_ref, o_ref, ext_ref, *, seq):
    i = pl.program_id(1)
    ts = x_ref.shape[0]
    cg = w_ref.shape[1]
    g = g_ref[...]
    first = i == 0
    last = i == pl.num_programs(1) - 1
    ext_ref[0:POOL_HALO, :] = jnp.where(first, 0.0, _rms(xp_ref[...], g))
    ext_ref[POOL_HALO:POOL_HALO + ts, :] = _rms(x_ref[...], g)
    ext_ref[POOL_HALO + ts:POOL_HALO + ts + POOL_HALO, :] = jnp.where(last, 0.0, _rms(xn_ref[...], g))

    t_glob = i * ts + lax.broadcasted_iota(jnp.int32, (ts, 1), 0)
    for gi, w in enumerate(POOL_WINDOWS):
        lanes = slice(gi * cg, (gi + 1) * cg)
        acc = ext_ref[POOL_HALO - w // 2:POOL_HALO - w // 2 + ts, lanes]
        for o in range(1, w):
            acc = acc + ext_ref[POOL_HALO - w // 2 + o:POOL_HALO - w // 2 + o + ts, lanes]
        lo = jnp.clip(t_glob - w // 2, 0, seq)
        hi = jnp.clip(t_glob - w // 2 + w, 0, seq)
        cnt = (hi - lo).astype(F32)
        pooled = acc / cnt - ext_ref[POOL_HALO:POOL_HALO + ts, lanes]
        y = _dot(pooled.astype(BF16), w_ref[gi])
        o_ref[:, lanes] = x_ref[:, lanes] + y * sc_ref[:, lanes]


def _pool(h, norm_g, w_grp, scale, *, ts):
    b, s, d = h.shape
    ng, cg, _ = w_grp.shape
    nb = ts // POOL_HALO
    last_blk = s // POOL_HALO - 1
    return pl.pallas_call(
        functools.partial(_pool_body, seq=s),
        grid=(b, s // ts),
        in_specs=[
            pl.BlockSpec((None, ts, d), lambda bi, i: (bi, i, 0)),
            pl.BlockSpec((None, POOL_HALO, d), lambda bi, i: (bi, jnp.maximum(i * nb - 1, 0), 0)),
            pl.BlockSpec((None, POOL_HALO, d), lambda bi, i: (bi, jnp.minimum((i + 1) * nb, last_blk), 0)),
            pl.BlockSpec((1, d), lambda bi, i: (0, 0)),
            pl.BlockSpec((ng, cg, cg), lambda bi, i: (0, 0, 0)),
            pl.BlockSpec((1, d), lambda bi, i: (0, 0)),
        ],
        out_specs=pl.BlockSpec((None, ts, d), lambda bi, i: (bi, i, 0)),
        out_shape=jax.ShapeDtypeStruct((b, s, d), F32),
        scratch_shapes=[pltpu.VMEM((ts + 2 * POOL_HALO, d), F32)],
        compiler_params=_cparams(("parallel", "parallel")),
        name="pool",
    )(h, h, h, norm_g, w_grp, scale)


def _norm_and_shift(x_ref, xp_ref, xn_ref, g, i, n_tiles):
    tm = x_ref.shape[0]
    hn = _rms(x_ref[...], g)
    prev_row = jnp.where(i == 0, 0.0, _rms(xp_ref[...], g)[SHIFT_HALO - 1:SHIFT_HALO])
    next_row = jnp.where(i == n_tiles - 1, 0.0, _rms(xn_ref[...], g)[0:1])
    row = lax.broadcasted_iota(jnp.int32, (tm, 1), 0)
    prev = jnp.where(row == 0, prev_row, pltpu.roll(hn, 1, axis=0))
    nxt = jnp.where(row == tm - 1, next_row, pltpu.roll(hn, tm - 1, axis=0))
    return hn, 0.5 * (prev + nxt) - hn


def _shift_specs(s, d, tm):
    nb = tm // SHIFT_HALO
    last_blk = s // SHIFT_HALO - 1
    return [
        pl.BlockSpec((None, tm, d), lambda bi, i, *_: (bi, i, 0)),
        pl.BlockSpec((None, SHIFT_HALO, d), lambda bi, i, *_: (bi, jnp.maximum(i * nb - 1, 0), 0)),
        pl.BlockSpec((None, SHIFT_HALO, d), lambda bi, i, *_: (bi, jnp.minimum((i + 1) * nb, last_blk), 0)),
    ]


def _rkv_body(x_ref, xp_ref, xn_ref, g_ref, mu_ref, wr_ref, wk_ref, wv_ref, o_ref):
    hn, xx = _norm_and_shift(x_ref, xp_ref, xn_ref, g_ref[...], pl.program_id(1), pl.num_programs(1))
    for j, w_ref in enumerate((wr_ref, wk_ref, wv_ref)):
        xm = (hn + xx * mu_ref[j]).astype(BF16)
        o_ref[j] = _dot(xm, w_ref[...]).astype(o_ref.dtype)


def _rwkv_rkv(h, norm_g, mu3, w_r, w_k, w_v, *, tm):
    b, s, d = h.shape
    return pl.pallas_call(
        _rkv_body,
        grid=(b, s // tm),
        in_specs=_shift_specs(s, d, tm) + [
            pl.BlockSpec((1, d), lambda bi, i: (0, 0)),
            pl.BlockSpec((3, 1, d), lambda bi, i: (0, 0, 0)),
        ] + [pl.BlockSpec((d, d), lambda bi, i: (0, 0))] * 3,
        out_specs=pl.BlockSpec((3, None, tm, d), lambda bi, i: (0, bi, i, 0)),
        out_shape=jax.ShapeDtypeStruct((3, b, s, d), BF16),
        compiler_params=_cparams(("parallel", "parallel")),
        name="rwkv_rkv",
    )(h, h, h, norm_g, mu3, w_r, w_k, w_v)


def _lora_body(x_ref, xp_ref, xn_ref, g_ref, mu_ref, w1_ref, w2_ref, w0_ref, a1_ref, a2_ref, a0_ref,
               g1_ref, g2_ref, lw_ref, alr_ref, gate_ref):
    hn, xx = _norm_and_shift(x_ref, xp_ref, xn_ref, g_ref[...], pl.program_id(1), pl.num_programs(1))
    xw = (hn + xx * mu_ref[0]).astype(BF16)
    xa = (hn + xx * mu_ref[1]).astype(BF16)
    xg = (hn + xx * mu_ref[2]).astype(BF16)
    neg_rate = -math.exp(-0.5)
    for dr in range(2):
        z = w0_ref[dr] + _dot(jnp.tanh(_dot(xw, w1_ref[dr])).astype(BF16), w2_ref[dr])
        lw_ref[dr] = neg_rate * _sigmoid(z)
        alr = _sigmoid(a0_ref[dr] + _dot(_dot(xa, a1_ref[dr]).astype(BF16), a2_ref[dr]))
        alr_ref[dr] = alr.astype(alr_ref.dtype)
    gate = _dot(_sigmoid(_dot(xg, g1_ref[...])).astype(BF16), g2_ref[...])
    gate_ref[...] = gate.astype(gate_ref.dtype)


def _rwkv_lora(h, norm_g, mu3, w1, w2, w0, a1, a2, a0, g1, g2, *, tm):
    b, s, d = h.shape
    lw_, la_, lg_ = w1.shape[2], a1.shape[2], g1.shape[1]
    full = lambda *shape: pl.BlockSpec(shape, lambda bi, i: (0,) * len(shape))
    return pl.pallas_call(
        _lora_body,
        grid=(b, s // tm),
        in_specs=_shift_specs(s, d, tm) + [
            full(1, d), full(3, 1, d),
            full(2, d, lw_), full(2, lw_, d), full(2, 1, d),
            full(2, d, la_), full(2, la_, d), full(2, 1, d),
            full(d, lg_), full(lg_, d),
        ],
        out_specs=[
            pl.BlockSpec((2, None, tm, d), lambda bi, i: (0, bi, i, 0)),
            pl.BlockSpec((2, None, tm, d), lambda bi, i: (0, bi, i, 0)),
            pl.BlockSpec((None, tm, d), lambda bi, i: (bi, i, 0)),
        ],
        out_shape=[
            jax.ShapeDtypeStruct((2, b, s, d), F32),
            jax.ShapeDtypeStruct((2, b, s, d), BF16),
            jax.ShapeDtypeStruct((b, s, d), BF16),
        ],
        compiler_params=_cparams(("parallel", "parallel")),
        name="rwkv_lora",
    )(h, h, h, norm_g, mu3, w1, w2, w0, a1, a2, a0, g1, g2)


def _prefix_sum_rows(x, row_idx):
    shift = 1
    while shift < x.shape[0]:
        x = x + jnp.where(row_idx >= shift, pltpu.roll(x, shift, axis=0), 0.0)
        shift *= 2
    return x


def _head_groups(lanes, value):
    li = lax.broadcasted_iota(jnp.int32, (lanes, lanes), 0) // HEAD
    lj = lax.broadcasted_iota(jnp.int32, (lanes, lanes), 1) // HEAD
    return jnp.where(li == lj, value, 0.0).astype(BF16)


def _stack(x, lane_head):
    return jnp.concatenate([jnp.where(lane_head == 0, x, 0.0), jnp.where(lane_head == 1, x, 0.0)], axis=0)


def _wkv_body(r_ref, k_ref, v_ref, lw_ref, alr_ref, kkp_ref, kap_ref, rkp_ref, y_ref, bonus_ref, s_ref):
    c = pl.program_id(2)
    fwd = pl.program_id(1) == 0
    nb, ch, d = r_ref.shape
    rows = 2 * ch
    n_pairs = d // PAIR

    @pl.when(c == 0)
    def _():
        s_ref[...] = jnp.zeros_like(s_ref)

    row = lax.broadcasted_iota(jnp.int32, (rows, rows), 0)
    col = lax.broadcasted_iota(jnp.int32, (rows, rows), 1)
    t_loc, s_loc = row % ch, col % ch
    same_head = (row // ch) == (col // ch)
    before = (t_loc - s_loc) * jnp.where(fwd, 1, -1) > 0
    strict = same_head & before
    incl = same_head & (before | (s_loc == t_loc))
    eye = row == col
    tok_row = lax.broadcasted_iota(jnp.int32, (ch, PAIR), 0)
    lane_head = lax.broadcasted_iota(jnp.int32, (ch, PAIR), 1) // HEAD
    group = _head_groups(2 * PAIR, 1.0)

    def pair_steps(bb, p):
        sl = slice(p * PAIR, (p + 1) * PAIR)
        r, k, v = r_ref[bb, :, sl].astype(F32), k_ref[bb, :, sl].astype(F32), v_ref[bb, :, sl].astype(F32)
        lw, alr = lw_ref[bb, :, sl], alr_ref[bb, :, sl].astype(F32)
        run = _prefix_sum_rows(lw, tok_row)
        cl_end = run[ch - 1:ch]
        cl = jnp.where(fwd, run, cl_end - run + lw)
        kk_raw = k * kkp_ref[:, sl]
        kd = k * (1.0 + (alr - 1.0) * kap_ref[:, sl])
        sums = _dot(jnp.concatenate([kk_raw * kk_raw, r * kd * rkp_ref[:, sl]], axis=1).astype(BF16), group)
        yield
        kk = kk_raw / jnp.maximum(jnp.sqrt(sums[:, :PAIR]), 1e-12)
        bv = kk * alr
        bonus_ref[bb, :, sl] = (sums[:, PAIR:] * v).astype(bonus_ref.dtype)

        e_neg = jnp.exp(-cl)
        e_end = jnp.exp(cl_end - cl)
        a_t = _stack(-kk * jnp.exp(cl - lw), lane_head)
        r_t = _stack(r * jnp.exp(cl), lane_head)
        b_t = _stack(bv * e_neg, lane_head)
        k_t = _stack(kd * e_neg, lane_head)
        b_e = _stack(bv * e_end, lane_head).T.astype(BF16)
        k_e = _stack(kd * e_end, lane_head).T.astype(BF16)
        v_s = _stack(v, lane_head).astype(BF16)

        aa = _dot_nt(jnp.concatenate([a_t, r_t], axis=0).astype(BF16),
                     jnp.concatenate([b_t, k_t], axis=0).astype(BF16))
        yield
        a_ab = jnp.where(strict, aa[:rows, :rows], 0.0)
        a_ak = jnp.where(strict, aa[:rows, rows:], 0.0).astype(BF16)
        a_rb = jnp.where(incl, aa[rows:, :rows], 0.0).astype(BF16)
        a_rk = jnp.where(incl, aa[rows:, rows:], 0.0).astype(BF16)
        zyh = _dot(jnp.concatenate([a_ak, a_rk, k_e], axis=0), v_s)

        n_pow = a_ab
        inv = jnp.where(eye, 1.0, 0.0) + a_ab
        for it in range(1, INV_STEPS):
            nb = n_pow.astype(BF16)
            if it == 1:
                n_pow = _dot(nb, nb)
            else:
                both = _dot(jnp.concatenate([inv, n_pow], axis=0).astype(BF16), nb)
                inv = inv + both[:rows]
                n_pow = both[rows:]
            yield
        if INV_STEPS > 1:
            inv = inv + _dot(inv.astype(BF16), n_pow.astype(BF16))
            yield
        qp = _dot(inv.astype(BF16), jnp.concatenate([a_t, zyh[:rows]], axis=1).astype(BF16)).astype(BF16)
        yield
        rg = _dot(jnp.concatenate([a_rb, b_e], axis=0), qp)
        yield
        r_hat = r_t + rg[:rows, :PAIR]
        g_mat = jnp.where(eye, jnp.exp(cl_end), 0.0) + rg[rows:, :PAIR]
        ys = _dot(jnp.concatenate([r_hat, g_mat], axis=0).astype(BF16), s_ref[bb * n_pairs + p].astype(BF16))
        yield
        y_bd = ys[:rows] + rg[:rows, PAIR:] + zyh[rows:2 * rows]
        y_ref[bb, :, sl] = (y_bd[:ch] + y_bd[ch:]).astype(y_ref.dtype)
        s_ref[bb * n_pairs + p] = ys[rows:] + rg[rows:, PAIR:] + zyh[2 * rows:]

    chains = [pair_steps(bb, p) for bb in range(nb) for p in range(n_pairs)]
    live = list(range(len(chains)))
    tick = 0
    while live:
        live = [i for i in live
                if (i // WKV_WAVE) * WKV_SKEW > tick or next(chains[i], _DONE) is not _DONE]
        tick += 1


def _wkv(rkv, lw, alr, kkp, kap, rkp):
    _, b, s, d = rkv.shape
    nc = s // CHUNK
    assert 2 * CHUNK == PAIR

    def tok(bi, dr, c):
        return jnp.where(dr == 0, c, nc - 1 - c)

    nb = math.gcd(b, WKV_SEQS)
    rkv_spec = lambda j: pl.BlockSpec((None, nb, CHUNK, d), lambda bi, dr, c: (j, bi, tok(bi, dr, c), 0))
    dir_spec = pl.BlockSpec((None, nb, CHUNK, d), lambda bi, dr, c: (dr, bi, tok(bi, dr, c), 0))
    par_spec = pl.BlockSpec((1, d), lambda bi, dr, c: (0, 0))
    return pl.pallas_call(
        _wkv_body,
        grid=(b // nb, 2, nc),
        in_specs=[rkv_spec(0), rkv_spec(1), rkv_spec(2), dir_spec, dir_spec, par_spec, par_spec, par_spec],
        out_specs=[dir_spec, dir_spec],
        out_shape=[jax.ShapeDtypeStruct((2, b, s, d), BF16), jax.ShapeDtypeStruct((2, b, s, d), BF16)],
        scratch_shapes=[pltpu.VMEM((nb * (d // PAIR), PAIR, PAIR), F32)],
        compiler_params=_cparams(("parallel", "parallel", "arbitrary")),
        name="wkv",
    )(rkv, rkv, rkv, lw, alr, kkp, kap, rkp)


def _out_body(x_ref, y_ref, bonus_ref, gate_ref, lnw_ref, lnb_ref, wo_ref, o_ref, m_ref):
    d = x_ref.shape[1]
    avg = _head_groups(GN_LANES, 1.0 / HEAD)
    for c0 in range(0, d, GN_LANES):
        sl = slice(c0, c0 + GN_LANES)
        y = y_ref[0, :, sl].astype(F32) + y_ref[1, :, sl].astype(F32)
        mean = _dot2(y, avg)
        dev = y - mean
        var = _dot((dev * dev).astype(BF16), avg)
        gn = dev * lax.rsqrt(var + GN_EPS) * lnw_ref[:, sl] + lnb_ref[:, sl]
        bonus = bonus_ref[0, :, sl].astype(F32) + bonus_ref[1, :, sl].astype(F32)
        m_ref[:, sl] = ((gn + bonus) * gate_ref[:, sl].astype(F32)).astype(BF16)
    o_ref[...] = x_ref[...] + _dot(m_ref[...], wo_ref[...])


def _rwkv_out(h, y, bonus, gate, lnw, lnb, wo, *, tm):
    b, s, d = h.shape
    tok = pl.BlockSpec((None, tm, d), lambda bi, i: (bi, i, 0))
    both = pl.BlockSpec((2, None, tm, d), lambda bi, i: (0, bi, i, 0))
    par = pl.BlockSpec((1, d), lambda bi, i: (0, 0))
    return pl.pallas_call(
        _out_body,
        grid=(b, s // tm),
        in_specs=[tok, both, both, tok, par, par, pl.BlockSpec((d, d), lambda bi, i: (0, 0))],
        out_specs=tok,
        out_shape=jax.ShapeDtypeStruct((b, s, d), F32),
        scratch_shapes=[pltpu.VMEM((tm, d), BF16)],
        compiler_params=_cparams(("parallel", "parallel")),
        name="rwkv_out",
    )(h, y, bonus, gate, lnw, lnb, wo)


def _pad_lora(w_in, w_out):
    r = w_in.shape[-1]
    pad = (-r) % 128
    w_in = jnp.pad(w_in, [(0, 0)] * (w_in.ndim - 1) + [(0, pad)])
    w_out = jnp.pad(w_out, [(0, 0)] * (w_out.ndim - 2) + [(0, pad), (0, 0)])
    return w_in.astype(BF16), w_out.astype(BF16)


def _tile(n, want):
    t = min(n, want)
    while n % t:
        t //= 2
    return t


def _rwkv_mixer(h, norm_g, mu, w_r, w_k, w_v, w_o, w0, w1, w2, a0, a1, a2, g1, g2, k_k, k_a, r_k, ln_w, ln_b):
    b, s, d = h.shape
    row = lambda z: z.reshape(1, d)
    tm = _tile(s, 256)
    rkv = _rwkv_rkv(h, norm_g, mu[jnp.array([0, 2, 3])].reshape(3, 1, d), w_r, w_k, w_v, tm=tm)
    w1p, w2p = _pad_lora(w1, w2)
    a1p, a2p = _pad_lora(a1, a2)
    lw, alr, gate = _rwkv_lora(h, norm_g, mu[jnp.array([1, 4, 5])].reshape(3, 1, d),
                               w1p, w2p, w0.reshape(2, 1, d), a1p, a2p, a0.reshape(2, 1, d),
                               g1.astype(BF16), g2.astype(BF16), tm=tm)
    y, bonus = _wkv(rkv, lw, alr, row(k_k), row(k_a), row(r_k))
    return _rwkv_out(h, y, bonus, gate, row(ln_w), row(ln_b), w_o, tm=tm)


def kernel(x, ffn1_norm, ffn1_gate, ffn1_up, ffn1_down, mix_norm, ffn2_norm, ffn2_gate, ffn2_up, ffn2_down, pool_w, pool_scale, rwkv_mu, rwkv_wr, rwkv_wk, rwkv_wv, rwkv_wo, rwkv_w0, rwkv_w1, rwkv_w2, rwkv_a0, rwkv_a1, rwkv_a2, rwkv_g1, rwkv_g2, rwkv_kk, rwkv_ka, rwkv_rk, rwkv_lnw, rwkv_lnb, final_norm):
    b, s, d = x.shape
    depth = ffn1_norm.shape[0]
    f = ffn1_gate.shape[2]
    row = lambda z: z.reshape(1, d)
    tm_ffn, tf = _tile(b * s, 1024), _tile(f, 512)
    ffn_params = ((ffn1_norm, ffn1_gate, ffn1_up, ffn1_down), (ffn2_norm, ffn2_gate, ffn2_up, ffn2_down))
    order = [(i, half) for i in range(depth) for half in (0, 1)]

    weights = [ffn_params[0][k][0].astype(BF16) for k in (1, 2, 3)]
    rwkv_w = None
    h = x
    for q, (i, half) in enumerate(order):
        j = i // 2
        if half == 1:
            if i % 2 == 0:
                h = _pool(h, row(mix_norm[i]), pool_w[j].astype(BF16), row(pool_scale[j]), ts=_tile(s, 512))
            else:
                h = _rwkv_mixer(h, row(mix_norm[i]), rwkv_mu[j], *rwkv_w,
                                rwkv_w0[j], rwkv_w1[j], rwkv_w2[j], rwkv_a0[j], rwkv_a1[j], rwkv_a2[j],
                                rwkv_g1[j], rwkv_g2[j], rwkv_kk[j], rwkv_ka[j], rwkv_rk[j],
                                rwkv_lnw[j], rwkv_lnb[j])
        casts = []
        if q + 1 < len(order):
            ni, nhalf = order[q + 1]
            casts += [(ffn_params[nhalf][k], ni) for k in (1, 2, 3)]
        if half == 0 and i % 2 == 1:
            casts += [(w, j) for w in (rwkv_wr, rwkv_wk, rwkv_wv, rwkv_wo)]
        out, cast = _ffn(h.reshape(b * s, d), row(ffn_params[half][0][i]), *weights, row(final_norm), casts,
                         final_norm=q == len(order) - 1, tm=tm_ffn, tf=tf)
        h = out.reshape(b, s, d)
        if q + 1 < len(order):
            weights, cast = cast[:3], cast[3:]
        if cast:
            rwkv_w = cast
    return h
```

```python
import functools
import math

import jax
import jax.numpy as jnp
from jax import lax
from jax.experimental import pallas as pl
from jax.experimental.pallas import tpu as pltpu

F32 = jnp.float32
BF16 = jnp.bfloat16

RMS_EPS = 1e-6
GN_EPS = 64e-5
HEAD = 64
PAIR = 2 * HEAD
CHUNK = 64
POOL_WINDOWS = (2, 4, 8, 16)
POOL_HALO = 16
SHIFT_HALO = 8
INV_STEPS = (CHUNK - 1).bit_length()
GN_LANES = 256
WKV_SEQS = 4
WKV_WAVE = 2
WKV_SKEW = 1
_DONE = object()
FFN_DOWN_COLS = 512
VMEM_LIMIT = 60 * 1024 * 1024


def _cparams(sem):
    return pltpu.CompilerParams(dimension_semantics=sem, vmem_limit_bytes=VMEM_LIMIT)


def _rms(x, g):
    return x * lax.rsqrt(jnp.mean(x * x, axis=-1, keepdims=True) + RMS_EPS) * g


def _sigmoid(x):
    return 0.5 * jnp.tanh(0.5 * x) + 0.5


def _dot(a, b):
    return jnp.dot(a, b, preferred_element_type=F32)


def _dot_nt(a, b):
    return lax.dot_general(a, b, (((1,), (1,)), ((), ())), preferred_element_type=F32)


def _dot_tn(a, b):
    return lax.dot_general(a, b, (((0,), (0,)), ((), ())), preferred_element_type=F32)


def _split2(x):
    hi = x.astype(BF16)
    lo = (x - hi.astype(F32)).astype(BF16)
    return hi, lo


def _split3(x):
    hi = x.astype(BF16)
    r1 = x - hi.astype(F32)
    mid = r1.astype(BF16)
    lo = (r1 - mid.astype(F32)).astype(BF16)
    return hi, mid, lo


def _dot2(x, m_bf16):
    hi, lo = _split2(x)
    return _dot(hi, m_bf16) + _dot(lo, m_bf16)


def _ffn_body(*refs, final_norm, n_cast):
    x_ref, g_ref, wg_ref, wu_ref, wd_ref, fg_ref = refs[:6]
    cast_src = refs[6:6 + n_cast]
    o_ref = refs[6 + n_cast]
    cast_dst = refs[7 + n_cast:7 + 2 * n_cast]
    n_ref = refs[7 + 2 * n_cast]
    f = pl.program_id(1)
    d = o_ref.shape[1]

    @pl.when(f == 0)
    def _():
        x = x_ref[...]
        n_ref[...] = _rms(x, g_ref[...]).astype(BF16)
        o_ref[...] = x

    n = n_ref[...]
    gate = _dot(n, wg_ref[...])
    up = _dot(n, wu_ref[...])
    act = (0.5 * gate * jax.nn.sigmoid(gate) * up).astype(BF16)
    for c0 in range(0, d, FFN_DOWN_COLS):
        cols = slice(c0, c0 + FFN_DOWN_COLS)
        o_ref[:, cols] += _dot(act, wd_ref[:, cols])

    for src, dst in zip(cast_src, cast_dst):
        dst[...] = src[...].astype(dst.dtype)

    if final_norm:
        @pl.when(f == pl.num_programs(1) - 1)
        def _():
            o_ref[...] = _rms(o_ref[...], fg_ref[...])


def _cast_specs(stacked, layer, ni, nj):
    _, rows, cols = stacked.shape
    ok = lambda n, parts, unit: n % parts == 0 and (n // parts) % unit == 0
    if ok(rows, ni, 16) and ok(cols, nj, 128):
        blk, imap = (rows // ni, cols // nj), lambda i, j: (i, j)
    elif ok(rows, nj, 16) and ok(cols, ni, 128):
        blk, imap = (rows // nj, cols // ni), lambda i, j: (j, i)
    else:
        assert ok(rows, ni, 16), (rows, ni)
        nc = 1
        while nc * 2 <= nj and ok(cols, nc * 2, 128):
            nc *= 2
        blk, imap = (rows // ni, cols // nc), lambda i, j: (i, jnp.minimum(j, nc - 1))
    in_spec = pl.BlockSpec((None,) + blk, lambda i, j: (layer,) + imap(i, j))
    return in_spec, pl.BlockSpec(blk, imap), jax.ShapeDtypeStruct((rows, cols), BF16)


def _ffn(h, norm_g, wg, wu, wd, final_g, casts, *, final_norm, tm, tf):
    t, d = h.shape
    f = wg.shape[1]
    ni, nj = t // tm, f // tf
    specs = [_cast_specs(w, layer, ni, nj) for w, layer in casts]
    outs = pl.pallas_call(
        functools.partial(_ffn_body, final_norm=final_norm, n_cast=len(casts)),
        grid=(ni, nj),
        in_specs=[
            pl.BlockSpec((tm, d), lambda i, j: (i, 0)),
            pl.BlockSpec((1, d), lambda i, j: (0, 0)),
            pl.BlockSpec((d, tf), lambda i, j: (0, j)),
            pl.BlockSpec((d, tf), lambda i, j: (0, j)),
            pl.BlockSpec((tf, d), lambda i, j: (j, 0)),
            pl.BlockSpec((1, d), lambda i, j: (0, 0)),
        ] + [sp[0] for sp in specs],
        out_specs=[pl.BlockSpec((tm, d), lambda i, j: (i, 0))] + [sp[1] for sp in specs],
        out_shape=[jax.ShapeDtypeStruct((t, d), F32)] + [sp[2] for sp in specs],
        scratch_shapes=[pltpu.VMEM((tm, d), BF16)],
        compiler_params=_cparams(("parallel", "arbitrary")),
        name="ffn",
    )(h, norm_g, wg, wu, wd, final_g, *[w for w, _ in casts])
    return outs[0], list(outs[1:])


def _pool_body(x_ref, xp_ref, xn_ref, g_ref, w_ref, sc_ref, o_ref, ext_ref, *, seq):
    i = pl.program_id(1)
    ts = x_ref.shape[0]
    cg = w_ref.shape[1]
    g = g_ref[...]
    first = i == 0
    last = i == pl.num_programs(1) - 1
    ext_ref[0:POOL_HALO, :] = jnp.where(first, 0.0, _rms(xp_ref[...], g))
    ext_ref[POOL_HALO:POOL_HALO + ts, :] = _rms(x_ref[...], g)
    ext_ref[POOL_HALO + ts:POOL_HALO + ts + POOL_HALO, :] = jnp.where(last, 0.0, _rms(xn_ref[...], g))

    t_glob = i * ts + lax.broadcasted_iota(jnp.int32, (ts, 1), 0)
    for gi, w in enumerate(POOL_WINDOWS):
        lanes = slice(gi * cg, (gi + 1) * cg)
        acc = ext_ref[POOL_HALO - w // 2:POOL_HALO - w // 2 + ts, lanes]
        for o in range(1, w):
            acc = acc + ext_ref[POOL_HALO - w // 2 + o:POOL_HALO - w // 2 + o + ts, lanes]
        lo = jnp.clip(t_glob - w // 2, 0, seq)
        hi = jnp.clip(t_glob - w // 2 + w, 0, seq)
        cnt = (hi - lo).astype(F32)
        pooled = acc / cnt - ext_ref[POOL_HALO:POOL_HALO + ts, lanes]
        y = _dot(pooled.astype(BF16), w_ref[gi])
        o_ref[:, lanes] = x_ref[:, lanes] + y * sc_ref[:, lanes]


def _pool(h, norm_g, w_grp, scale, *, ts):
    b, s, d = h.shape
    ng, cg, _ = w_grp.shape
    nb = ts // POOL_HALO
    last_blk = s // POOL_HALO - 1
    return pl.pallas_call(
        functools.partial(_pool_body, seq=s),
        grid=(b, s // ts),
        in_specs=[
            pl.BlockSpec((None, ts, d), lambda bi, i: (bi, i, 0)),
            pl.BlockSpec((None, POOL_HALO, d), lambda bi, i: (bi, jnp.maximum(i * nb - 1, 0), 0)),
            pl.BlockSpec((None, POOL_HALO, d), lambda bi, i: (bi, jnp.minimum((i + 1) * nb, last_blk), 0)),
            pl.BlockSpec((1, d), lambda bi, i: (0, 0)),
            pl.BlockSpec((ng, cg, cg), lambda bi, i: (0, 0, 0)),
            pl.BlockSpec((1, d), lambda bi, i: (0, 0)),
        ],
        out_specs=pl.BlockSpec((None, ts, d), lambda bi, i: (bi, i, 0)),
        out_shape=jax.ShapeDtypeStruct((b, s, d), F32),
        scratch_shapes=[pltpu.VMEM((ts + 2 * POOL_HALO, d), F32)],
        compiler_params=_cparams(("parallel", "parallel")),
        name="pool",
    )(h, h, h, norm_g, w_grp, scale)


def _norm_and_shift(x_ref, xp_ref, xn_ref, g, i, n_tiles):
    tm = x_ref.shape[0]
    hn = _rms(x_ref[...], g)
    prev_row = jnp.where(i == 0, 0.0, _rms(xp_ref[...], g)[SHIFT_HALO - 1:SHIFT_HALO])
    next_row = jnp.where(i == n_tiles - 1, 0.0, _rms(xn_ref[...], g)[0:1])
    row = lax.broadcasted_iota(jnp.int32, (tm, 1), 0)
    prev = jnp.where(row == 0, prev_row, pltpu.roll(hn, 1, axis=0))
    nxt = jnp.where(row == tm - 1, next_row, pltpu.roll(hn, tm - 1, axis=0))
    return hn, 0.5 * (prev + nxt) - hn


def _shift_specs(s, d, tm):
    nb = tm // SHIFT_HALO
    last_blk = s // SHIFT_HALO - 1
    return [
        pl.BlockSpec((None, tm, d), lambda bi, i, *_: (bi, i, 0)),
        pl.BlockSpec((None, SHIFT_HALO, d), lambda bi, i, *_: (bi, jnp.maximum(i * nb - 1, 0), 0)),
        pl.BlockSpec((None, SHIFT_HALO, d), lambda bi, i, *_: (bi, jnp.minimum((i + 1) * nb, last_blk), 0)),
    ]


def _rkv_body(x_ref, xp_ref, xn_ref, g_ref, mu_ref, wr_ref, wk_ref, wv_ref, o_ref):
    hn, xx = _norm_and_shift(x_ref, xp_ref, xn_ref, g_ref[...], pl.program_id(1), pl.num_programs(1))
    for j, w_ref in enumerate((wr_ref, wk_ref, wv_ref)):
        xm = (hn + xx * mu_ref[j]).astype(BF16)
        o_ref[j] = _dot(xm, w_ref[...]).astype(o_ref.dtype)


def _rwkv_rkv(h, norm_g, mu3, w_r, w_k, w_v, *, tm):
    b, s, d = h.shape
    return pl.pallas_call(
        _rkv_body,
        grid=(b, s // tm),
        in_specs=_shift_specs(s, d, tm) + [
            pl.BlockSpec((1, d), lambda bi, i: (0, 0)),
            pl.BlockSpec((3, 1, d), lambda bi, i: (0, 0, 0)),
        ] + [pl.BlockSpec((d, d), lambda bi, i: (0, 0))] * 3,
        out_specs=pl.BlockSpec((3, None, tm, d), lambda bi, i: (0, bi, i, 0)),
        out_shape=jax.ShapeDtypeStruct((3, b, s, d), BF16),
        compiler_params=_cparams(("parallel", "parallel")),
        name="rwkv_rkv",
    )(h, h, h, norm_g, mu3, w_r, w_k, w_v)


def _lora_body(x_ref, xp_ref, xn_ref, g_ref, mu_ref, w1_ref, w2_ref, w0_ref, a1_ref, a2_ref, a0_ref,
               g1_ref, g2_ref, lw_ref, alr_ref, gate_ref):
    hn, xx = _norm_and_shift(x_ref, xp_ref, xn_ref, g_ref[...], pl.program_id(1), pl.num_programs(1))
    xw = (hn + xx * mu_ref[0]).astype(BF16)
    xa = (hn + xx * mu_ref[1]).astype(BF16)
    xg = (hn + xx * mu_ref[2]).astype(BF16)
    neg_rate = -math.exp(-0.5)
    for dr in range(2):
        z = w0_ref[dr] + _dot(jnp.tanh(_dot(xw, w1_ref[dr])).astype(BF16), w2_ref[dr])
        lw_ref[dr] = neg_rate * _sigmoid(z)
        alr = _sigmoid(a0_ref[dr] + _dot(_dot(xa, a1_ref[dr]).astype(BF16), a2_ref[dr]))
        alr_ref[dr] = alr.astype(alr_ref.dtype)
    gate = _dot(_sigmoid(_dot(xg, g1_ref[...])).astype(BF16), g2_ref[...])
    gate_ref[...] = gate.astype(gate_ref.dtype)


def _rwkv_lora(h, norm_g, mu3, w1, w2, w0, a1, a2, a0, g1, g2, *, tm):
    b, s, d = h.shape
    lw_, la_, lg_ = w1.shape[2], a1.shape[2], g1.shape[1]
    full = lambda *shape: pl.BlockSpec(shape, lambda bi, i: (0,) * len(shape))
    return pl.pallas_call(
        _lora_body,
        grid=(b, s // tm),
        in_specs=_shift_specs(s, d, tm) + [
            full(1, d), full(3, 1, d),
            full(2, d, lw_), full(2, lw_, d), full(2, 1, d),
            full(2, d, la_), full(2, la_, d), full(2, 1, d),
            full(d, lg_), full(lg_, d),
        ],
        out_specs=[
            pl.BlockSpec((2, None, tm, d), lambda bi, i: (0, bi, i, 0)),
            pl.BlockSpec((2, None, tm, d), lambda bi, i: (0, bi, i, 0)),
            pl.BlockSpec((None, tm, d), lambda bi, i: (bi, i, 0)),
        ],
        out_shape=[
            jax.ShapeDtypeStruct((2, b, s, d), F32),
            jax.ShapeDtypeStruct((2, b, s, d), BF16),
            jax.ShapeDtypeStruct((b, s, d), BF16),
        ],
        compiler_params=_cparams(("parallel", "parallel")),
        name="rwkv_lora",
    )(h, h, h, norm_g, mu3, w1, w2, w0, a1, a2, a0, g1, g2)


def _prefix_sum_rows(x, row_idx):
    shift = 1
    while shift < x.shape[0]:
        x = x + jnp.where(row_idx >= shift, pltpu.roll(x, shift, axis=0), 0.0)
        shift *= 2
    return x


def _head_groups(lanes, value):
    li = lax.broadcasted_iota(jnp.int32, (lanes, lanes), 0) // HEAD
    lj = lax.broadcasted_iota(jnp.int32, (lanes, lanes), 1) // HEAD
    return jnp.where(li == lj, value, 0.0).astype(BF16)


def _stack(x, lane_head):
    return jnp.concatenate([jnp.where(lane_head == 0, x, 0.0), jnp.where(lane_head == 1, x, 0.0)], axis=0)


def _wkv_body(r_ref, k_ref, v_ref, lw_ref, alr_ref, kkp_ref, kap_ref, rkp_ref, y_ref, bonus_ref, s_ref):
    c = pl.program_id(2)
    fwd = pl.program_id(1) == 0
    nb, ch, d = r_ref.shape
    rows = 2 * ch
    n_pairs = d // PAIR

    @pl.when(c == 0)
    def _():
        s_ref[...] = jnp.zeros_like(s_ref)

    row = lax.broadcasted_iota(jnp.int32, (rows, rows), 0)
    col = lax.broadcasted_iota(jnp.int32, (rows, rows), 1)
    t_loc, s_loc = row % ch, col % ch
    same_head = (row // ch) == (col // ch)
    before = (t_loc - s_loc) * jnp.where(fwd, 1, -1) > 0
    strict = same_head & before
    incl = same_head & (before | (s_loc == t_loc))
    eye = row == col
    tok_row = lax.broadcasted_iota(jnp.int32, (ch, PAIR), 0)
    lane_head = lax.broadcasted_iota(jnp.int32, (ch, PAIR), 1) // HEAD
    group = _head_groups(2 * PAIR, 1.0)

    def pair_steps(bb, p):
        sl = slice(p * PAIR, (p + 1) * PAIR)
        r, k, v = r_ref[bb, :, sl].astype(F32), k_ref[bb, :, sl].astype(F32), v_ref[bb, :, sl].astype(F32)
        lw, alr = lw_ref[bb, :, sl], alr_ref[bb, :, sl].astype(F32)
        run = _prefix_sum_rows(lw, tok_row)
        cl_end = run[ch - 1:ch]
        cl = jnp.where(fwd, run, cl_end - run + lw)
        kk_raw = k * kkp_ref[:, sl]
        kd = k * (1.0 + (alr - 1.0) * kap_ref[:, sl])
        sums = _dot(jnp.concatenate([kk_raw * kk_raw, r * kd * rkp_ref[:, sl]], axis=1).astype(BF16), group)
        yield
        kk = kk_raw / jnp.maximum(jnp.sqrt(sums[:, :PAIR]), 1e-12)
        bv = kk * alr
        bonus_ref[bb, :, sl] = (sums[:, PAIR:] * v).astype(bonus_ref.dtype)

        e_neg = jnp.exp(-cl)
        e_end = jnp.exp(cl_end - cl)
        a_t = _stack(-kk * jnp.exp(cl - lw), lane_head)
        r_t = _stack(r * jnp.exp(cl), lane_head)
        b_t = _stack(bv * e_neg, lane_head)
        k_t = _stack(kd * e_neg, lane_head)
        b_e = _stack(bv * e_end, lane_head).T.astype(BF16)
        k_e = _stack(kd * e_end, lane_head).T.astype(BF16)
        v_s = _stack(v, lane_head).astype(BF16)

        ar = jnp.concatenate([a_t, r_t], axis=0).astype(BF16)
        aa = _dot_nt(ar, jnp.concatenate([b_t, k_t], axis=0).astype(BF16))
        decay_end = jnp.where(eye, jnp.exp(cl_end), 0.0).astype(BF16)
        ars = _dot(jnp.concatenate([ar, decay_end], axis=0), s_ref[bb * n_pairs + p].astype(BF16))
        yield
        a_ab = jnp.where(strict, aa[:rows, :rows], 0.0)
        a_ak = jnp.where(strict, aa[:rows, rows:], 0.0).astype(BF16)
        a_rb = jnp.where(incl, aa[rows:, :rows], 0.0).astype(BF16)
        a_rk = jnp.where(incl, aa[rows:, rows:], 0.0).astype(BF16)
        zyh = _dot(jnp.concatenate([a_ak, a_rk, k_e], axis=0), v_s)

        n_pow = a_ab
        inv = jnp.where(eye, 1.0, 0.0) + a_ab
        for it in range(1, INV_STEPS):
            nb = n_pow.astype(BF16)
            if it == 1:
                n_pow = _dot(nb, nb)
            else:
                both = _dot(jnp.concatenate([inv, n_pow], axis=0).astype(BF16), nb)
                inv = inv + both[:rows]
                n_pow = both[rows:]
            yield
        if INV_STEPS > 1:
            inv = inv + _dot(inv.astype(BF16), n_pow.astype(BF16))
            yield
        u = _dot(inv.astype(BF16), (ars[:rows] + zyh[:rows]).astype(BF16)).astype(BF16)
        yield
        fin = _dot(jnp.concatenate([a_rb, b_e], axis=0), u)
        yield
        y_bd = ars[rows:2 * rows] + fin[:rows] + zyh[rows:2 * rows]
        y_ref[bb, :, sl] = (y_bd[:ch] + y_bd[ch:]).astype(y_ref.dtype)
        s_ref[bb * n_pairs + p] = ars[2 * rows:] + fin[rows:] + zyh[2 * rows:]

    chains = [pair_steps(bb, p) for bb in range(nb) for p in range(n_pairs)]
    live = list(range(len(chains)))
    tick = 0
    while live:
        live = [i for i in live
                if (i // WKV_WAVE) * WKV_SKEW > tick or next(chains[i], _DONE) is not _DONE]
        tick += 1


def _wkv(rkv, lw, alr, kkp, kap, rkp):
    _, b, s, d = rkv.shape
    nc = s // CHUNK
    assert 2 * CHUNK == PAIR

    def tok(bi, dr, c):
        return jnp.where(dr == 0, c, nc - 1 - c)

    nb = math.gcd(b, WKV_SEQS)
    rkv_spec = lambda j: pl.BlockSpec((None, nb, CHUNK, d), lambda bi, dr, c: (j, bi, tok(bi, dr, c), 0))
    dir_spec = pl.BlockSpec((None, nb, CHUNK, d), lambda bi, dr, c: (dr, bi, tok(bi, dr, c), 0))
    par_spec = pl.BlockSpec((1, d), lambda bi, dr, c: (0, 0))
    return pl.pallas_call(
        _wkv_body,
        grid=(b // nb, 2, nc),
        in_specs=[rkv_spec(0), rkv_spec(1), rkv_spec(2), dir_spec, dir_spec, par_spec, par_spec, par_spec],
        out_specs=[dir_spec, dir_spec],
        out_shape=[jax.ShapeDtypeStruct((2, b, s, d), BF16), jax.ShapeDtypeStruct((2, b, s, d), BF16)],
        scratch_shapes=[pltpu.VMEM((nb * (d // PAIR), PAIR, PAIR), F32)],
        compiler_params=_cparams(("parallel", "parallel", "arbitrary")),
        name="wkv",
    )(rkv, rkv, rkv, lw, alr, kkp, kap, rkp)


def _out_body(x_ref, y_ref, bonus_ref, gate_ref, lnw_ref, lnb_ref, wo_ref, o_ref, m_ref):
    d = x_ref.shape[1]
    avg = _head_groups(GN_LANES, 1.0 / HEAD)
    for c0 in range(0, d, GN_LANES):
        sl = slice(c0, c0 + GN_LANES)
        y = y_ref[0, :, sl].astype(F32) + y_ref[1, :, sl].astype(F32)
        mean = _dot2(y, avg)
        dev = y - mean
        var = _dot((dev * dev).astype(BF16), avg)
        gn = dev * lax.rsqrt(var + GN_EPS) * lnw_ref[:, sl] + lnb_ref[:, sl]
        bonus = bonus_ref[0, :, sl].astype(F32) + bonus_ref[1, :, sl].astype(F32)
        m_ref[:, sl] = ((gn + bonus) * gate_ref[:, sl].astype(F32)).astype(BF16)
    o_ref[...] = x_ref[...] + _dot(m_ref[...], wo_ref[...])


def _rwkv_out(h, y, bonus, gate, lnw, lnb, wo, *, tm):
    b, s, d = h.shape
    tok = pl.BlockSpec((None, tm, d), lambda bi, i: (bi, i, 0))
    both = pl.BlockSpec((2, None, tm, d), lambda bi, i: (0, bi, i, 0))
    par = pl.BlockSpec((1, d), lambda bi, i: (0, 0))
    return pl.pallas_call(
        _out_body,
        grid=(b, s // tm),
        in_specs=[tok, both, both, tok, par, par, pl.BlockSpec((d, d), lambda bi, i: (0, 0))],
        out_specs=tok,
        out_shape=jax.ShapeDtypeStruct((b, s, d), F32),
        scratch_shapes=[pltpu.VMEM((tm, d), BF16)],
        compiler_params=_cparams(("parallel", "parallel")),
        name="rwkv_out",
    )(h, y, bonus, gate, lnw, lnb, wo)


def _pad_lora(w_in, w_out):
    r = w_in.shape[-1]
    pad = (-r) % 128
    w_in = jnp.pad(w_in, [(0, 0)] * (w_in.ndim - 1) + [(0, pad)])
    w_out = jnp.pad(w_out, [(0, 0)] * (w_out.ndim - 2) + [(0, pad), (0, 0)])
    return w_in.astype(BF16), w_out.astype(BF16)


def _tile(n, want):
    t = min(n, want)
    while n % t:
        t //= 2
    return t


def _rwkv_mixer(h, norm_g, mu, w_r, w_k, w_v, w_o, w0, w1, w2, a0, a1, a2, g1, g2, k_k, k_a, r_k, ln_w, ln_b):
    b, s, d = h.shape
    row = lambda z: z.reshape(1, d)
    tm = _tile(s, 256)
    rkv = _rwkv_rkv(h, norm_g, mu[jnp.array([0, 2, 3])].reshape(3, 1, d), w_r, w_k, w_v, tm=tm)
    w1p, w2p = _pad_lora(w1, w2)
    a1p, a2p = _pad_lora(a1, a2)
    lw, alr, gate = _rwkv_lora(h, norm_g, mu[jnp.array([1, 4, 5])].reshape(3, 1, d),
                               w1p, w2p, w0.reshape(2, 1, d), a1p, a2p, a0.reshape(2, 1, d),
                               g1.astype(BF16), g2.astype(BF16), tm=tm)
    y, bonus = _wkv(rkv, lw, alr, row(k_k), row(k_a), row(r_k))
    return _rwkv_out(h, y, bonus, gate, row(ln_w), row(ln_b), w_o, tm=tm)


def kernel(x, ffn1_norm, ffn1_gate, ffn1_up, ffn1_down, mix_norm, ffn2_norm, ffn2_gate, ffn2_up, ffn2_down, pool_w, pool_scale, rwkv_mu, rwkv_wr, rwkv_wk, rwkv_wv, rwkv_wo, rwkv_w0, rwkv_w1, rwkv_w2, rwkv_a0, rwkv_a1, rwkv_a2, rwkv_g1, rwkv_g2, rwkv_kk, rwkv_ka, rwkv_rk, rwkv_lnw, rwkv_lnb, final_norm):
    b, s, d = x.shape
    depth = ffn1_norm.shape[0]
    f = ffn1_gate.shape[2]
    row = lambda z: z.reshape(1, d)
    tm_ffn, tf = _tile(b * s, 1024), _tile(f, 512)
    ffn_params = ((ffn1_norm, ffn1_gate, ffn1_up, ffn1_down), (ffn2_norm, ffn2_gate, ffn2_up, ffn2_down))
    order = [(i, half) for i in range(depth) for half in (0, 1)]

    weights = [ffn_params[0][k][0].astype(BF16) for k in (1, 2, 3)]
    rwkv_w = None
    h = x
    for q, (i, half) in enumerate(order):
        j = i // 2
        if half == 1:
            if i % 2 == 0:
                h = _pool(h, row(mix_norm[i]), pool_w[j].astype(BF16), row(pool_scale[j]), ts=_tile(s, 512))
            else:
                h = _rwkv_mixer(h, row(mix_norm[i]), rwkv_mu[j], *rwkv_w,
                                rwkv_w0[j], rwkv_w1[j], rwkv_w2[j], rwkv_a0[j], rwkv_a1[j], rwkv_a2[j],
                                rwkv_g1[j], rwkv_g2[j], rwkv_kk[j], rwkv_ka[j], rwkv_rk[j],
                                rwkv_lnw[j], rwkv_lnb[j])
        casts = []
        if q + 1 < len(order):
            ni, nhalf = order[q + 1]
            casts += [(ffn_params[nhalf][k], ni) for k in (1, 2, 3)]
        if half == 0 and i % 2 == 1:
            casts += [(w, j) for w in (rwkv_wr, rwkv_wk, rwkv_wv, rwkv_wo)]
        out, cast = _ffn(h.reshape(b * s, d), row(ffn_params[half][0][i]), *weights, row(final_norm), casts,
                         final_norm=q == len(order) - 1, tm=tm_ffn, tf=tf)
        h = out.reshape(b, s, d)
        if q + 1 < len(order):
            weights, cast = cast[:3], cast[3:]
        if cast:
            rwkv_w = cast
    return h
```

```python
import functools
import math

import jax
import jax.numpy as jnp
from jax import lax
from jax.experimental import pallas as pl
from jax.experimental.pallas import tpu as pltpu

F32 = jnp.float32
BF16 = jnp.bfloat16

RMS_EPS = 1e-6
GN_EPS = 64e-5
HEAD = 64
PAIR = 2 * HEAD
CHUNK = 64
POOL_WINDOWS = (2, 4, 8, 16)
POOL_HALO = 16
SHIFT_HALO = 8
INV_STEPS = (CHUNK - 1).bit_length()
GN_LANES = 256
WKV_SEQS = 4
WKV_WAVE = 2
WKV_SKEW = 1
ROW_SLAB = 256
_DONE = object()
FFN_DOWN_COLS = 512
VMEM_LIMIT = 60 * 1024 * 1024


def _cparams(sem):
    return pltpu.CompilerParams(dimension_semantics=sem, vmem_limit_bytes=VMEM_LIMIT)


def _rms(x, g):
    return x * lax.rsqrt(jnp.mean(x * x, axis=-1, keepdims=True) + RMS_EPS) * g


def _sigmoid(x):
    return 0.5 * jnp.tanh(0.5 * x) + 0.5


def _dot(a, b):
    return jnp.dot(a, b, preferred_element_type=F32)


def _dot_nt(a, b):
    return lax.dot_general(a, b, (((1,), (1,)), ((), ())), preferred_element_type=F32)


def _dot_tn(a, b):
    return lax.dot_general(a, b, (((0,), (0,)), ((), ())), preferred_element_type=F32)


def _split2(x):
    hi = x.astype(BF16)
    lo = (x - hi.astype(F32)).astype(BF16)
    return hi, lo


def _split3(x):
    hi = x.astype(BF16)
    r1 = x - hi.astype(F32)
    mid = r1.astype(BF16)
    lo = (r1 - mid.astype(F32)).astype(BF16)
    return hi, mid, lo


def _dot2(x, m_bf16):
    hi, lo = _split2(x)
    return _dot(hi, m_bf16) + _dot(lo, m_bf16)


def _ffn_body(*refs, final_norm, n_cast):
    x_ref, g_ref, wg_ref, wu_ref, wd_ref, fg_ref = refs[:6]
    cast_src = refs[6:6 + n_cast]
    o_ref = refs[6 + n_cast]
    cast_dst = refs[7 + n_cast:7 + 2 * n_cast]
    n_ref = refs[7 + 2 * n_cast]
    f = pl.program_id(1)
    d = o_ref.shape[1]

    @pl.when(f == 0)
    def _():
        x = x_ref[...]
        n_ref[...] = _rms(x, g_ref[...]).astype(BF16)
        o_ref[...] = x

    n = n_ref[...]
    gate = _dot(n, wg_ref[...])
    up = _dot(n, wu_ref[...])
    act = (0.5 * gate * jax.nn.sigmoid(gate) * up).astype(BF16)
    for c0 in range(0, d, FFN_DOWN_COLS):
        cols = slice(c0, c0 + FFN_DOWN_COLS)
        o_ref[:, cols] += _dot(act, wd_ref[:, cols])

    for src, dst in zip(cast_src, cast_dst):
        dst[...] = src[...].astype(dst.dtype)

    if final_norm:
        @pl.when(f == pl.num_programs(1) - 1)
        def _():
            o_ref[...] = _rms(o_ref[...], fg_ref[...])


def _cast_specs(stacked, layer, ni, nj):
    _, rows, cols = stacked.shape
    ok = lambda n, parts, unit: n % parts == 0 and (n // parts) % unit == 0
    if ok(rows, ni, 16) and ok(cols, nj, 128):
        blk, imap = (rows // ni, cols // nj), lambda i, j: (i, j)
    elif ok(rows, nj, 16) and ok(cols, ni, 128):
        blk, imap = (rows // nj, cols // ni), lambda i, j: (j, i)
    else:
        assert ok(rows, ni, 16), (rows, ni)
        nc = 1
        while nc * 2 <= nj and ok(cols, nc * 2, 128):
            nc *= 2
        blk, imap = (rows // ni, cols // nc), lambda i, j: (i, jnp.minimum(j, nc - 1))
    in_spec = pl.BlockSpec((None,) + blk, lambda i, j: (layer,) + imap(i, j))
    return in_spec, pl.BlockSpec(blk, imap), jax.ShapeDtypeStruct((rows, cols), BF16)


def _ffn(h, norm_g, wg, wu, wd, final_g, casts, *, final_norm, tm, tf):
    t, d = h.shape
    f = wg.shape[1]
    ni, nj = t // tm, f // tf
    specs = [_cast_specs(w, layer, ni, nj) for w, layer in casts]
    outs = pl.pallas_call(
        functools.partial(_ffn_body, final_norm=final_norm, n_cast=len(casts)),
        grid=(ni, nj),
        in_specs=[
            pl.BlockSpec((tm, d), lambda i, j: (i, 0)),
            pl.BlockSpec((1, d), lambda i, j: (0, 0)),
            pl.BlockSpec((d, tf), lambda i, j: (0, j)),
            pl.BlockSpec((d, tf), lambda i, j: (0, j)),
            pl.BlockSpec((tf, d), lambda i, j: (j, 0)),
            pl.BlockSpec((1, d), lambda i, j: (0, 0)),
        ] + [sp[0] for sp in specs],
        out_specs=[pl.BlockSpec((tm, d), lambda i, j: (i, 0))] + [sp[1] for sp in specs],
        out_shape=[jax.ShapeDtypeStruct((t, d), F32)] + [sp[2] for sp in specs],
        scratch_shapes=[pltpu.VMEM((tm, d), BF16)],
        compiler_params=_cparams(("parallel", "arbitrary")),
        name="ffn",
    )(h, norm_g, wg, wu, wd, final_g, *[w for w, _ in casts])
    return outs[0], list(outs[1:])


def _pool_body(x_ref, xp_ref, xn_ref, g_ref, w_ref, sc_ref, o_ref, ext_ref, *, seq):
    i = pl.program_id(1)
    ts = x_ref.shape[0]
    cg = w_ref.shape[1]
    g = g_ref[...]
    first = i == 0
    last = i == pl.num_programs(1) - 1
    ext_ref[0:POOL_HALO, :] = jnp.where(first, 0.0, _rms(xp_ref[...], g))
    ext_ref[POOL_HALO:POOL_HALO + ts, :] = _rms(x_ref[...], g)
    ext_ref[POOL_HALO + ts:POOL_HALO + ts + POOL_HALO, :] = jnp.where(last, 0.0, _rms(xn_ref[...], g))

    t_glob = i * ts + lax.broadcasted_iota(jnp.int32, (ts, 1), 0)
    for gi, w in enumerate(POOL_WINDOWS):
        lanes = slice(gi * cg, (gi + 1) * cg)
        acc = ext_ref[POOL_HALO - w // 2:POOL_HALO - w // 2 + ts, lanes]
        for o in range(1, w):
            acc = acc + ext_ref[POOL_HALO - w // 2 + o:POOL_HALO - w // 2 + o + ts, lanes]
        lo = jnp.clip(t_glob - w // 2, 0, seq)
        hi = jnp.clip(t_glob - w // 2 + w, 0, seq)
        cnt = (hi - lo).astype(F32)
        pooled = acc / cnt - ext_ref[POOL_HALO:POOL_HALO + ts, lanes]
        y = _dot(pooled.astype(BF16), w_ref[gi])
        o_ref[:, lanes] = x_ref[:, lanes] + y * sc_ref[:, lanes]


def _pool(h, norm_g, w_grp, scale, *, ts):
    b, s, d = h.shape
    ng, cg, _ = w_grp.shape
    nb = ts // POOL_HALO
    last_blk = s // POOL_HALO - 1
    return pl.pallas_call(
        functools.partial(_pool_body, seq=s),
        grid=(b, s // ts),
        in_specs=[
            pl.BlockSpec((None, ts, d), lambda bi, i: (bi, i, 0)),
            pl.BlockSpec((None, POOL_HALO, d), lambda bi, i: (bi, jnp.maximum(i * nb - 1, 0), 0)),
            pl.BlockSpec((None, POOL_HALO, d), lambda bi, i: (bi, jnp.minimum((i + 1) * nb, last_blk), 0)),
            pl.BlockSpec((1, d), lambda bi, i: (0, 0)),
            pl.BlockSpec((ng, cg, cg), lambda bi, i: (0, 0, 0)),
            pl.BlockSpec((1, d), lambda bi, i: (0, 0)),
        ],
        out_specs=pl.BlockSpec((None, ts, d), lambda bi, i: (bi, i, 0)),
        out_shape=jax.ShapeDtypeStruct((b, s, d), F32),
        scratch_shapes=[pltpu.VMEM((ts + 2 * POOL_HALO, d), F32)],
        compiler_params=_cparams(("parallel", "parallel")),
        name="pool",
    )(h, h, h, norm_g, w_grp, scale)


def _norm_and_shift(x_ref, xp_ref, xn_ref, g, i, n_tiles, lo=0, hi=None):
    tm = x_ref.shape[0]
    hi = tm if hi is None else hi
    n = hi - lo
    hn = _rms(x_ref[lo:hi, :], g)
    if lo == 0:
        prev_row = jnp.where(i == 0, 0.0, _rms(xp_ref[...], g)[SHIFT_HALO - 1:SHIFT_HALO])
    else:
        prev_row = _rms(x_ref[lo - SHIFT_HALO:lo, :], g)[SHIFT_HALO - 1:SHIFT_HALO]
    if hi == tm:
        next_row = jnp.where(i == n_tiles - 1, 0.0, _rms(xn_ref[...], g)[0:1])
    else:
        next_row = _rms(x_ref[hi:hi + SHIFT_HALO, :], g)[0:1]
    row = lax.broadcasted_iota(jnp.int32, (n, 1), 0)
    prev = jnp.where(row == 0, prev_row, pltpu.roll(hn, 1, axis=0))
    nxt = jnp.where(row == n - 1, next_row, pltpu.roll(hn, n - 1, axis=0))
    return hn, 0.5 * (prev + nxt) - hn


def _run_chains(chains, wave=1, skew=1):
    live = list(range(len(chains)))
    tick = 0
    while live:
        live = [i for i in live if (i // wave) * skew > tick or next(chains[i], _DONE) is not _DONE]
        tick += 1


def _shift_specs(s, d, tm):
    nb = tm // SHIFT_HALO
    last_blk = s // SHIFT_HALO - 1
    return [
        pl.BlockSpec((None, tm, d), lambda bi, i, *_: (bi, i, 0)),
        pl.BlockSpec((None, SHIFT_HALO, d), lambda bi, i, *_: (bi, jnp.maximum(i * nb - 1, 0), 0)),
        pl.BlockSpec((None, SHIFT_HALO, d), lambda bi, i, *_: (bi, jnp.minimum((i + 1) * nb, last_blk), 0)),
    ]


def _rkv_body(x_ref, xp_ref, xn_ref, g_ref, mu_ref, wr_ref, wk_ref, wv_ref, o_ref):
    hn, xx = _norm_and_shift(x_ref, xp_ref, xn_ref, g_ref[...], pl.program_id(1), pl.num_programs(1))
    for j, w_ref in enumerate((wr_ref, wk_ref, wv_ref)):
        xm = (hn + xx * mu_ref[j]).astype(BF16)
        o_ref[j] = _dot(xm, w_ref[...]).astype(o_ref.dtype)


def _rwkv_rkv(h, norm_g, mu3, w_r, w_k, w_v, *, tm):
    b, s, d = h.shape
    return pl.pallas_call(
        _rkv_body,
        grid=(b, s // tm),
        in_specs=_shift_specs(s, d, tm) + [
            pl.BlockSpec((1, d), lambda bi, i: (0, 0)),
            pl.BlockSpec((3, 1, d), lambda bi, i: (0, 0, 0)),
        ] + [pl.BlockSpec((d, d), lambda bi, i: (0, 0))] * 3,
        out_specs=pl.BlockSpec((3, None, tm, d), lambda bi, i: (0, bi, i, 0)),
        out_shape=jax.ShapeDtypeStruct((3, b, s, d), BF16),
        compiler_params=_cparams(("parallel", "parallel")),
        name="rwkv_rkv",
    )(h, h, h, norm_g, mu3, w_r, w_k, w_v)


def _lora_body(x_ref, xp_ref, xn_ref, g_ref, mu_ref, w1_ref, w2_ref, w0_ref, a1_ref, a2_ref, a0_ref,
               g1_ref, g2_ref, lw_ref, alr_ref, gate_ref):
    tm = x_ref.shape[0]
    neg_rate = -math.exp(-0.5)

    def slab_steps(lo, hi):
        hn, xx = _norm_and_shift(x_ref, xp_ref, xn_ref, g_ref[...], pl.program_id(1), pl.num_programs(1), lo, hi)
        xw = (hn + xx * mu_ref[0]).astype(BF16)
        xa = (hn + xx * mu_ref[1]).astype(BF16)
        xg = (hn + xx * mu_ref[2]).astype(BF16)
        tw = [_dot(xw, w1_ref[dr]) for dr in range(2)]
        ta = [_dot(xa, a1_ref[dr]) for dr in range(2)]
        tg = _dot(xg, g1_ref[...])
        yield
        z = [_dot(jnp.tanh(tw[dr]).astype(BF16), w2_ref[dr]) for dr in range(2)]
        av = [_dot(ta[dr].astype(BF16), a2_ref[dr]) for dr in range(2)]
        gate = _dot(_sigmoid(tg).astype(BF16), g2_ref[...])
        yield
        for dr in range(2):
            lw_ref[dr, lo:hi, :] = neg_rate * _sigmoid(w0_ref[dr] + z[dr])
            alr_ref[dr, lo:hi, :] = _sigmoid(a0_ref[dr] + av[dr]).astype(alr_ref.dtype)
        gate_ref[lo:hi, :] = gate.astype(gate_ref.dtype)

    slab = min(tm, ROW_SLAB)
    _run_chains([slab_steps(lo, lo + slab) for lo in range(0, tm, slab)])


def _rwkv_lora(h, norm_g, mu3, w1, w2, w0, a1, a2, a0, g1, g2, *, tm):
    b, s, d = h.shape
    lw_, la_, lg_ = w1.shape[2], a1.shape[2], g1.shape[1]
    full = lambda *shape: pl.BlockSpec(shape, lambda bi, i: (0,) * len(shape))
    return pl.pallas_call(
        _lora_body,
        grid=(b, s // tm),
        in_specs=_shift_specs(s, d, tm) + [
            full(1, d), full(3, 1, d),
            full(2, d, lw_), full(2, lw_, d), full(2, 1, d),
            full(2, d, la_), full(2, la_, d), full(2, 1, d),
            full(d, lg_), full(lg_, d),
        ],
        out_specs=[
            pl.BlockSpec((2, None, tm, d), lambda bi, i: (0, bi, i, 0)),
            pl.BlockSpec((2, None, tm, d), lambda bi, i: (0, bi, i, 0)),
            pl.BlockSpec((None, tm, d), lambda bi, i: (bi, i, 0)),
        ],
        out_shape=[
            jax.ShapeDtypeStruct((2, b, s, d), F32),
            jax.ShapeDtypeStruct((2, b, s, d), BF16),
            jax.ShapeDtypeStruct((b, s, d), BF16),
        ],
        compiler_params=_cparams(("parallel", "parallel")),
        name="rwkv_lora",
    )(h, h, h, norm_g, mu3, w1, w2, w0, a1, a2, a0, g1, g2)


def _prefix_sum_rows(x, row_idx):
    shift = 1
    while shift < x.shape[0]:
        x = x + jnp.where(row_idx >= shift, pltpu.roll(x, shift, axis=0), 0.0)
        shift *= 2
    return x


def _head_groups(lanes, value):
    li = lax.broadcasted_iota(jnp.int32, (lanes, lanes), 0) // HEAD
    lj = lax.broadcasted_iota(jnp.int32, (lanes, lanes), 1) // HEAD
    return jnp.where(li == lj, value, 0.0).astype(BF16)


def _stack(x, lane_head):
    return jnp.concatenate([jnp.where(lane_head == 0, x, 0.0), jnp.where(lane_head == 1, x, 0.0)], axis=0)


def _wkv_body(r_ref, k_ref, v_ref, lw_ref, alr_ref, kkp_ref, kap_ref, rkp_ref, y_ref, bonus_ref, s_ref):
    c = pl.program_id(2)
    fwd = pl.program_id(1) == 0
    nb, ch, d = r_ref.shape
    rows = 2 * ch
    n_pairs = d // PAIR

    @pl.when(c == 0)
    def _():
        s_ref[...] = jnp.zeros_like(s_ref)

    row = lax.broadcasted_iota(jnp.int32, (rows, rows), 0)
    col = lax.broadcasted_iota(jnp.int32, (rows, rows), 1)
    t_loc, s_loc = row % ch, col % ch
    same_head = (row // ch) == (col // ch)
    before = (t_loc - s_loc) * jnp.where(fwd, 1, -1) > 0
    strict = same_head & before
    incl = same_head & (before | (s_loc == t_loc))
    eye = row == col
    tok_row = lax.broadcasted_iota(jnp.int32, (ch, PAIR), 0)
    lane_head = lax.broadcasted_iota(jnp.int32, (ch, PAIR), 1) // HEAD
    group = _head_groups(2 * PAIR, 1.0)

    def pair_steps(bb, p):
        sl = slice(p * PAIR, (p + 1) * PAIR)
        r, k, v = r_ref[bb, :, sl].astype(F32), k_ref[bb, :, sl].astype(F32), v_ref[bb, :, sl].astype(F32)
        lw, alr = lw_ref[bb, :, sl], alr_ref[bb, :, sl].astype(F32)
        run = _prefix_sum_rows(lw, tok_row)
        cl_end = run[ch - 1:ch]
        cl = jnp.where(fwd, run, cl_end - run + lw)
        kk_raw = k * kkp_ref[:, sl]
        kd = k * (1.0 + (alr - 1.0) * kap_ref[:, sl])
        sums = _dot(jnp.concatenate([kk_raw * kk_raw, r * kd * rkp_ref[:, sl]], axis=1).astype(BF16), group)
        yield
        kk = kk_raw / jnp.maximum(jnp.sqrt(sums[:, :PAIR]), 1e-12)
        bv = kk * alr
        bonus_ref[bb, :, sl] = (sums[:, PAIR:] * v).astype(bonus_ref.dtype)

        e_neg = jnp.exp(-cl)
        e_end = jnp.exp(cl_end - cl)
        a_t = _stack(-kk * jnp.exp(cl - lw), lane_head)
        r_t = _stack(r * jnp.exp(cl), lane_head)
        b_t = _stack(bv * e_neg, lane_head)
        k_t = _stack(kd * e_neg, lane_head)
        b_e = _stack(bv * e_end, lane_head).T.astype(BF16)
        k_e = _stack(kd * e_end, lane_head).T.astype(BF16)
        v_s = _stack(v, lane_head).astype(BF16)

        ar = jnp.concatenate([a_t, r_t], axis=0).astype(BF16)
        aa = _dot_nt(ar, jnp.concatenate([b_t, k_t], axis=0).astype(BF16))
        decay_end = jnp.where(eye, jnp.exp(cl_end), 0.0).astype(BF16)
        ars = _dot(jnp.concatenate([ar, decay_end], axis=0), s_ref[bb * n_pairs + p].astype(BF16))
        yield
        a_ab = jnp.where(strict, aa[:rows, :rows], 0.0)
        a_ak = jnp.where(strict, aa[:rows, rows:], 0.0).astype(BF16)
        a_rb = jnp.where(incl, aa[rows:, :rows], 0.0).astype(BF16)
        a_rk = jnp.where(incl, aa[rows:, rows:], 0.0).astype(BF16)
        zyh = _dot(jnp.concatenate([a_ak, a_rk, k_e], axis=0), v_s)

        n_pow = a_ab
        inv = jnp.where(eye, 1.0, 0.0) + a_ab
        for it in range(1, INV_STEPS):
            nb = n_pow.astype(BF16)
            if it == 1:
                n_pow = _dot(nb, nb)
            else:
                both = _dot(jnp.concatenate([inv, n_pow], axis=0).astype(BF16), nb)
                inv = inv + both[:rows]
                n_pow = both[rows:]
            yield
        if INV_STEPS > 1:
            inv = inv + _dot(inv.astype(BF16), n_pow.astype(BF16))
            yield
        u = _dot(inv.astype(BF16), (ars[:rows] + zyh[:rows]).astype(BF16)).astype(BF16)
        yield
        fin = _dot(jnp.concatenate([a_rb, b_e], axis=0), u)
        yield
        y_bd = ars[rows:2 * rows] + fin[:rows] + zyh[rows:2 * rows]
        y_ref[bb, :, sl] = (y_bd[:ch] + y_bd[ch:]).astype(y_ref.dtype)
        s_ref[bb * n_pairs + p] = ars[2 * rows:] + fin[rows:] + zyh[2 * rows:]

    _run_chains([pair_steps(bb, p) for bb in range(nb) for p in range(n_pairs)], WKV_WAVE, WKV_SKEW)


def _wkv(rkv, lw, alr, kkp, kap, rkp):
    _, b, s, d = rkv.shape
    nc = s // CHUNK
    assert 2 * CHUNK == PAIR

    def tok(bi, dr, c):
        return jnp.where(dr == 0, c, nc - 1 - c)

    nb = math.gcd(b, WKV_SEQS)
    rkv_spec = lambda j: pl.BlockSpec((None, nb, CHUNK, d), lambda bi, dr, c: (j, bi, tok(bi, dr, c), 0))
    dir_spec = pl.BlockSpec((None, nb, CHUNK, d), lambda bi, dr, c: (dr, bi, tok(bi, dr, c), 0))
    par_spec = pl.BlockSpec((1, d), lambda bi, dr, c: (0, 0))
    return pl.pallas_call(
        _wkv_body,
        grid=(b // nb, 2, nc),
        in_specs=[rkv_spec(0), rkv_spec(1), rkv_spec(2), dir_spec, dir_spec, par_spec, par_spec, par_spec],
        out_specs=[dir_spec, dir_spec],
        out_shape=[jax.ShapeDtypeStruct((2, b, s, d), BF16), jax.ShapeDtypeStruct((2, b, s, d), BF16)],
        scratch_shapes=[pltpu.VMEM((nb * (d // PAIR), PAIR, PAIR), F32)],
        compiler_params=_cparams(("parallel", "parallel", "arbitrary")),
        name="wkv",
    )(rkv, rkv, rkv, lw, alr, kkp, kap, rkp)


def _out_body(x_ref, y_ref, bonus_ref, gate_ref, lnw_ref, lnb_ref, wo_ref, o_ref):
    tm, d = x_ref.shape
    avg = _head_groups(GN_LANES, 1.0 / HEAD)

    def slab_steps(lo, hi):
        mixed = []
        for c0 in range(0, d, GN_LANES):
            sl = slice(c0, c0 + GN_LANES)
            y = y_ref[0, lo:hi, sl].astype(F32) + y_ref[1, lo:hi, sl].astype(F32)
            mean = _dot2(y, avg)
            dev = y - mean
            var = _dot((dev * dev).astype(BF16), avg)
            gn = dev * lax.rsqrt(var + GN_EPS) * lnw_ref[:, sl] + lnb_ref[:, sl]
            bonus = bonus_ref[0, lo:hi, sl].astype(F32) + bonus_ref[1, lo:hi, sl].astype(F32)
            mixed.append(((gn + bonus) * gate_ref[lo:hi, sl].astype(F32)).astype(BF16))
            yield
        mixed = jnp.concatenate(mixed, axis=1)
        for c0 in range(0, d, GN_LANES):
            sl = slice(c0, c0 + GN_LANES)
            o_ref[lo:hi, sl] = x_ref[lo:hi, sl] + _dot(mixed, wo_ref[:, sl])
            yield

    slab = min(tm, ROW_SLAB)
    _run_chains([slab_steps(lo, lo + slab) for lo in range(0, tm, slab)], skew=d // GN_LANES)


def _rwkv_out(h, y, bonus, gate, lnw, lnb, wo, *, tm):
    b, s, d = h.shape
    tok = pl.BlockSpec((None, tm, d), lambda bi, i: (bi, i, 0))
    both = pl.BlockSpec((2, None, tm, d), lambda bi, i: (0, bi, i, 0))
    par = pl.BlockSpec((1, d), lambda bi, i: (0, 0))
    return pl.pallas_call(
        _out_body,
        grid=(b, s // tm),
        in_specs=[tok, both, both, tok, par, par, pl.BlockSpec((d, d), lambda bi, i: (0, 0))],
        out_specs=tok,
        out_shape=jax.ShapeDtypeStruct((b, s, d), F32),
        compiler_params=_cparams(("parallel", "parallel")),
        name="rwkv_out",
    )(h, y, bonus, gate, lnw, lnb, wo)


def _pad_lora(w_in, w_out):
    r = w_in.shape[-1]
    pad = (-r) % 128
    w_in = jnp.pad(w_in, [(0, 0)] * (w_in.ndim - 1) + [(0, pad)])
    w_out = jnp.pad(w_out, [(0, 0)] * (w_out.ndim - 2) + [(0, pad), (0, 0)])
    return w_in.astype(BF16), w_out.astype(BF16)


def _tile(n, want):
    t = min(n, want)
    while n % t:
        t //= 2
    return t


def _rwkv_mixer(h, norm_g, mu, w_r, w_k, w_v, w_o, w0, w1, w2, a0, a1, a2, g1, g2, k_k, k_a, r_k, ln_w, ln_b):
    b, s, d = h.shape
    row = lambda z: z.reshape(1, d)
    tm = _tile(s, 256)
    tm2 = _tile(s, 2 * ROW_SLAB)
    rkv = _rwkv_rkv(h, norm_g, mu[jnp.array([0, 2, 3])].reshape(3, 1, d), w_r, w_k, w_v, tm=tm)
    w1p, w2p = _pad_lora(w1, w2)
    a1p, a2p = _pad_lora(a1, a2)
    lw, alr, gate = _rwkv_lora(h, norm_g, mu[jnp.array([1, 4, 5])].reshape(3, 1, d),
                               w1p, w2p, w0.reshape(2, 1, d), a1p, a2p, a0.reshape(2, 1, d),
                               g1.astype(BF16), g2.astype(BF16), tm=tm2)
    y, bonus = _wkv(rkv, lw, alr, row(k_k), row(k_a), row(r_k))
    return _rwkv_out(h, y, bonus, gate, row(ln_w), row(ln_b), w_o, tm=tm2)


def kernel(x, ffn1_norm, ffn1_gate, ffn1_up, ffn1_down, mix_norm, ffn2_norm, ffn2_gate, ffn2_up, ffn2_down, pool_w, pool_scale, rwkv_mu, rwkv_wr, rwkv_wk, rwkv_wv, rwkv_wo, rwkv_w0, rwkv_w1, rwkv_w2, rwkv_a0, rwkv_a1, rwkv_a2, rwkv_g1, rwkv_g2, rwkv_kk, rwkv_ka, rwkv_rk, rwkv_lnw, rwkv_lnb, final_norm):
    b, s, d = x.shape
    depth = ffn1_norm.shape[0]
    f = ffn1_gate.shape[2]
    row = lambda z: z.reshape(1, d)
    tm_ffn, tf = _tile(b * s, 1024), _tile(f, 512)
    ffn_params = ((ffn1_norm, ffn1_gate, ffn1_up, ffn1_down), (ffn2_norm, ffn2_gate, ffn2_up, ffn2_down))
    order = [(i, half) for i in range(depth) for half in (0, 1)]

    weights = [ffn_params[0][k][0].astype(BF16) for k in (1, 2, 3)]
    rwkv_w = None
    h = x
    for q, (i, half) in enumerate(order):
        j = i // 2
        if half == 1:
            if i % 2 == 0:
                h = _pool(h, row(mix_norm[i]), pool_w[j].astype(BF16), row(pool_scale[j]), ts=_tile(s, 512))
            else:
                h = _rwkv_mixer(h, row(mix_norm[i]), rwkv_mu[j], *rwkv_w,
                                rwkv_w0[j], rwkv_w1[j], rwkv_w2[j], rwkv_a0[j], rwkv_a1[j], rwkv_a2[j],
                                rwkv_g1[j], rwkv_g2[j], rwkv_kk[j], rwkv_ka[j], rwkv_rk[j],
                                rwkv_lnw[j], rwkv_lnb[j])
        casts = []
        if q + 1 < len(order):
            ni, nhalf = order[q + 1]
            casts += [(ffn_params[nhalf][k], ni) for k in (1, 2, 3)]
        if half == 0 and i % 2 == 1:
            casts += [(w, j) for w in (rwkv_wr, rwkv_wk, rwkv_wv, rwkv_wo)]
        out, cast = _ffn(h.reshape(b * s, d), row(ffn_params[half][0][i]), *weights, row(final_norm), casts,
                         final_norm=q == len(order) - 1, tm=tm_ffn, tf=tf)
        h = out.reshape(b, s, d)
        if q + 1 < len(order):
            weights, cast = cast[:3], cast[3:]
        if cast:
            rwkv_w = cast
    return h
```

```python
import functools
import math

import jax
import jax.numpy as jnp
from jax import lax
from jax.experimental import pallas as pl
from jax.experimental.pallas import tpu as pltpu

F32 = jnp.float32
BF16 = jnp.bfloat16

RMS_EPS = 1e-6
GN_EPS = 64e-5
HEAD = 64
PAIR = 2 * HEAD
CHUNK = 64
POOL_WINDOWS = (2, 4, 8, 16)
POOL_HALO = 16
SHIFT_HALO = 8
INV_STEPS = (CHUNK - 1).bit_length()
GN_LANES = 256
WKV_SEQS = 4
WKV_WAVE = 2
WKV_SKEW = 1
ROW_SLAB = 256
_DONE = object()
FFN_DOWN_COLS = 512
VMEM_LIMIT = 60 * 1024 * 1024


def _cparams(sem):
    return pltpu.CompilerParams(dimension_semantics=sem, vmem_limit_bytes=VMEM_LIMIT)


def _rms(x, g):
    return x * lax.rsqrt(jnp.mean(x * x, axis=-1, keepdims=True) + RMS_EPS) * g


def _sigmoid(x):
    return 0.5 * jnp.tanh(0.5 * x) + 0.5


def _dot(a, b):
    return jnp.dot(a, b, preferred_element_type=F32)


def _dot_nt(a, b):
    return lax.dot_general(a, b, (((1,), (1,)), ((), ())), preferred_element_type=F32)


def _dot_tn(a, b):
    return lax.dot_general(a, b, (((0,), (0,)), ((), ())), preferred_element_type=F32)


def _split2(x):
    hi = x.astype(BF16)
    lo = (x - hi.astype(F32)).astype(BF16)
    return hi, lo


def _split3(x):
    hi = x.astype(BF16)
    r1 = x - hi.astype(F32)
    mid = r1.astype(BF16)
    lo = (r1 - mid.astype(F32)).astype(BF16)
    return hi, mid, lo


def _dot2(x, m_bf16):
    hi, lo = _split2(x)
    return _dot(hi, m_bf16) + _dot(lo, m_bf16)


def _ffn_body(*refs, final_norm, n_cast):
    x_ref, g_ref, wg_ref, wu_ref, wd_ref, fg_ref = refs[:6]
    cast_src = refs[6:6 + n_cast]
    o_ref = refs[6 + n_cast]
    cast_dst = refs[7 + n_cast:7 + 2 * n_cast]
    n_ref = refs[7 + 2 * n_cast]
    f = pl.program_id(1)
    d = o_ref.shape[1]

    @pl.when(f == 0)
    def _():
        x = x_ref[...]
        n_ref[...] = _rms(x, g_ref[...]).astype(BF16)
        o_ref[...] = x

    n = n_ref[...]
    gate = _dot(n, wg_ref[...])
    up = _dot(n, wu_ref[...])
    act = (0.5 * gate * jax.nn.sigmoid(gate) * up).astype(BF16)
    for c0 in range(0, d, FFN_DOWN_COLS):
        cols = slice(c0, c0 + FFN_DOWN_COLS)
        o_ref[:, cols] += _dot(act, wd_ref[:, cols])

    for src, dst in zip(cast_src, cast_dst):
        dst[...] = src[...].astype(dst.dtype)

    if final_norm:
        @pl.when(f == pl.num_programs(1) - 1)
        def _():
            o_ref[...] = _rms(o_ref[...], fg_ref[...])


def _cast_specs(stacked, layer, ni, nj):
    _, rows, cols = stacked.shape
    ok = lambda n, parts, unit: n % parts == 0 and (n // parts) % unit == 0
    if ok(rows, ni, 16) and ok(cols, nj, 128):
        blk, imap = (rows // ni, cols // nj), lambda i, j: (i, j)
    elif ok(rows, nj, 16) and ok(cols, ni, 128):
        blk, imap = (rows // nj, cols // ni), lambda i, j: (j, i)
    else:
        assert ok(rows, ni, 16), (rows, ni)
        nc = 1
        while nc * 2 <= nj and ok(cols, nc * 2, 128):
            nc *= 2
        blk, imap = (rows // ni, cols // nc), lambda i, j: (i, jnp.minimum(j, nc - 1))
    in_spec = pl.BlockSpec((None,) + blk, lambda i, j: (layer,) + imap(i, j))
    return in_spec, pl.BlockSpec(blk, imap), jax.ShapeDtypeStruct((rows, cols), BF16)


def _ffn(h, norm_g, wg, wu, wd, final_g, casts, *, final_norm, tm, tf):
    t, d = h.shape
    f = wg.shape[1]
    ni, nj = t // tm, f // tf
    specs = [_cast_specs(w, layer, ni, nj) for w, layer in casts]
    outs = pl.pallas_call(
        functools.partial(_ffn_body, final_norm=final_norm, n_cast=len(casts)),
        grid=(ni, nj),
        in_specs=[
            pl.BlockSpec((tm, d), lambda i, j: (i, 0)),
            pl.BlockSpec((1, d), lambda i, j: (0, 0)),
            pl.BlockSpec((d, tf), lambda i, j: (0, j)),
            pl.BlockSpec((d, tf), lambda i, j: (0, j)),
            pl.BlockSpec((tf, d), lambda i, j: (j, 0)),
            pl.BlockSpec((1, d), lambda i, j: (0, 0)),
        ] + [sp[0] for sp in specs],
        out_specs=[pl.BlockSpec((tm, d), lambda i, j: (i, 0))] + [sp[1] for sp in specs],
        out_shape=[jax.ShapeDtypeStruct((t, d), F32)] + [sp[2] for sp in specs],
        scratch_shapes=[pltpu.VMEM((tm, d), BF16)],
        compiler_params=_cparams(("parallel", "arbitrary")),
        name="ffn",
    )(h, norm_g, wg, wu, wd, final_g, *[w for w, _ in casts])
    return outs[0], list(outs[1:])


def _pool_body(x_ref, xp_ref, xn_ref, g_ref, w_ref, sc_ref, o_ref, ext_ref, *, seq):
    i = pl.program_id(1)
    ts = x_ref.shape[0]
    cg = w_ref.shape[1]
    g = g_ref[...]
    first = i == 0
    last = i == pl.num_programs(1) - 1
    ext_ref[0:POOL_HALO, :] = jnp.where(first, 0.0, _rms(xp_ref[...], g))
    ext_ref[POOL_HALO:POOL_HALO + ts, :] = _rms(x_ref[...], g)
    ext_ref[POOL_HALO + ts:POOL_HALO + ts + POOL_HALO, :] = jnp.where(last, 0.0, _rms(xn_ref[...], g))

    t_glob = i * ts + lax.broadcasted_iota(jnp.int32, (ts, 1), 0)
    for gi, w in enumerate(POOL_WINDOWS):
        lanes = slice(gi * cg, (gi + 1) * cg)
        acc = ext_ref[POOL_HALO - w // 2:POOL_HALO - w // 2 + ts, lanes]
        for o in range(1, w):
            acc = acc + ext_ref[POOL_HALO - w // 2 + o:POOL_HALO - w // 2 + o + ts, lanes]
        lo = jnp.clip(t_glob - w // 2, 0, seq)
        hi = jnp.clip(t_glob - w // 2 + w, 0, seq)
        cnt = (hi - lo).astype(F32)
        pooled = acc / cnt - ext_ref[POOL_HALO:POOL_HALO + ts, lanes]
        y = _dot(pooled.astype(BF16), w_ref[gi])
        o_ref[:, lanes] = x_ref[:, lanes] + y * sc_ref[:, lanes]


def _pool(h, norm_g, w_grp, scale, *, ts):
    b, s, d = h.shape
    ng, cg, _ = w_grp.shape
    nb = ts // POOL_HALO
    last_blk = s // POOL_HALO - 1
    return pl.pallas_call(
        functools.partial(_pool_body, seq=s),
        grid=(b, s // ts),
        in_specs=[
            pl.BlockSpec((None, ts, d), lambda bi, i: (bi, i, 0)),
            pl.BlockSpec((None, POOL_HALO, d), lambda bi, i: (bi, jnp.maximum(i * nb - 1, 0), 0)),
            pl.BlockSpec((None, POOL_HALO, d), lambda bi, i: (bi, jnp.minimum((i + 1) * nb, last_blk), 0)),
            pl.BlockSpec((1, d), lambda bi, i: (0, 0)),
            pl.BlockSpec((ng, cg, cg), lambda bi, i: (0, 0, 0)),
            pl.BlockSpec((1, d), lambda bi, i: (0, 0)),
        ],
        out_specs=pl.BlockSpec((None, ts, d), lambda bi, i: (bi, i, 0)),
        out_shape=jax.ShapeDtypeStruct((b, s, d), F32),
        scratch_shapes=[pltpu.VMEM((ts + 2 * POOL_HALO, d), F32)],
        compiler_params=_cparams(("parallel", "parallel")),
        name="pool",
    )(h, h, h, norm_g, w_grp, scale)


def _norm_and_shift(x_ref, xp_ref, xn_ref, g, i, n_tiles, lo=0, hi=None):
    tm = x_ref.shape[0]
    hi = tm if hi is None else hi
    n = hi - lo
    hn = _rms(x_ref[lo:hi, :], g)
    if lo == 0:
        prev_row = jnp.where(i == 0, 0.0, _rms(xp_ref[...], g)[SHIFT_HALO - 1:SHIFT_HALO])
    else:
        prev_row = _rms(x_ref[lo - SHIFT_HALO:lo, :], g)[SHIFT_HALO - 1:SHIFT_HALO]
    if hi == tm:
        next_row = jnp.where(i == n_tiles - 1, 0.0, _rms(xn_ref[...], g)[0:1])
    else:
        next_row = _rms(x_ref[hi:hi + SHIFT_HALO, :], g)[0:1]
    row = lax.broadcasted_iota(jnp.int32, (n, 1), 0)
    prev = jnp.where(row == 0, prev_row, pltpu.roll(hn, 1, axis=0))
    nxt = jnp.where(row == n - 1, next_row, pltpu.roll(hn, n - 1, axis=0))
    return hn, 0.5 * (prev + nxt) - hn


def _run_chains(chains, wave=1, skew=1):
    live = list(range(len(chains)))
    tick = 0
    while live:
        live = [i for i in live if (i // wave) * skew > tick or next(chains[i], _DONE) is not _DONE]
        tick += 1


def _shift_specs(s, d, tm):
    nb = tm // SHIFT_HALO
    last_blk = s // SHIFT_HALO - 1
    return [
        pl.BlockSpec((None, tm, d), lambda bi, i, *_: (bi, i, 0)),
        pl.BlockSpec((None, SHIFT_HALO, d), lambda bi, i, *_: (bi, jnp.maximum(i * nb - 1, 0), 0)),
        pl.BlockSpec((None, SHIFT_HALO, d), lambda bi, i, *_: (bi, jnp.minimum((i + 1) * nb, last_blk), 0)),
    ]


def _rkv_body(x_ref, xp_ref, xn_ref, g_ref, mu_ref, wr_ref, wk_ref, wv_ref, o_ref):
    tm = x_ref.shape[0]

    def slab_steps(lo, hi):
        hn, xx = _norm_and_shift(x_ref, xp_ref, xn_ref, g_ref[...], pl.program_id(1), pl.num_programs(1), lo, hi)
        yield
        for j, w_ref in enumerate((wr_ref, wk_ref, wv_ref)):
            xm = (hn + xx * mu_ref[j]).astype(BF16)
            o_ref[j, lo:hi, :] = _dot(xm, w_ref[...]).astype(o_ref.dtype)
            yield

    slab = min(tm, ROW_SLAB)
    _run_chains([slab_steps(lo, lo + slab) for lo in range(0, tm, slab)])


def _rwkv_rkv(h, norm_g, mu3, w_r, w_k, w_v, *, tm):
    b, s, d = h.shape
    return pl.pallas_call(
        _rkv_body,
        grid=(b, s // tm),
        in_specs=_shift_specs(s, d, tm) + [
            pl.BlockSpec((1, d), lambda bi, i: (0, 0)),
            pl.BlockSpec((3, 1, d), lambda bi, i: (0, 0, 0)),
        ] + [pl.BlockSpec((d, d), lambda bi, i: (0, 0))] * 3,
        out_specs=pl.BlockSpec((3, None, tm, d), lambda bi, i: (0, bi, i, 0)),
        out_shape=jax.ShapeDtypeStruct((3, b, s, d), BF16),
        compiler_params=_cparams(("parallel", "parallel")),
        name="rwkv_rkv",
    )(h, h, h, norm_g, mu3, w_r, w_k, w_v)


def _lora_body(x_ref, xp_ref, xn_ref, g_ref, mu_ref, w1_ref, w2_ref, w0_ref, a1_ref, a2_ref, a0_ref,
               g1_ref, g2_ref, lw_ref, alr_ref, gate_ref):
    tm = x_ref.shape[0]
    neg_rate = -math.exp(-0.5)

    def slab_steps(lo, hi):
        hn, xx = _norm_and_shift(x_ref, xp_ref, xn_ref, g_ref[...], pl.program_id(1), pl.num_programs(1), lo, hi)
        xw = (hn + xx * mu_ref[0]).astype(BF16)
        xa = (hn + xx * mu_ref[1]).astype(BF16)
        xg = (hn + xx * mu_ref[2]).astype(BF16)
        tw = [_dot(xw, w1_ref[dr]) for dr in range(2)]
        ta = [_dot(xa, a1_ref[dr]) for dr in range(2)]
        tg = _dot(xg, g1_ref[...])
        yield
        z = [_dot(jnp.tanh(tw[dr]).astype(BF16), w2_ref[dr]) for dr in range(2)]
        av = [_dot(ta[dr].astype(BF16), a2_ref[dr]) for dr in range(2)]
        gate = _dot(_sigmoid(tg).astype(BF16), g2_ref[...])
        yield
        for dr in range(2):
            lw_ref[dr, lo:hi, :] = neg_rate * _sigmoid(w0_ref[dr] + z[dr])
            alr_ref[dr, lo:hi, :] = _sigmoid(a0_ref[dr] + av[dr]).astype(alr_ref.dtype)
        gate_ref[lo:hi, :] = gate.astype(gate_ref.dtype)

    slab = min(tm, ROW_SLAB)
    _run_chains([slab_steps(lo, lo + slab) for lo in range(0, tm, slab)])


def _rwkv_lora(h, norm_g, mu3, w1, w2, w0, a1, a2, a0, g1, g2, *, tm):
    b, s, d = h.shape
    lw_, la_, lg_ = w1.shape[2], a1.shape[2], g1.shape[1]
    full = lambda *shape: pl.BlockSpec(shape, lambda bi, i: (0,) * len(shape))
    return pl.pallas_call(
        _lora_body,
        grid=(b, s // tm),
        in_specs=_shift_specs(s, d, tm) + [
            full(1, d), full(3, 1, d),
            full(2, d, lw_), full(2, lw_, d), full(2, 1, d),
            full(2, d, la_), full(2, la_, d), full(2, 1, d),
            full(d, lg_), full(lg_, d),
        ],
        out_specs=[
            pl.BlockSpec((2, None, tm, d), lambda bi, i: (0, bi, i, 0)),
            pl.BlockSpec((2, None, tm, d), lambda bi, i: (0, bi, i, 0)),
            pl.BlockSpec((None, tm, d), lambda bi, i: (bi, i, 0)),
        ],
        out_shape=[
            jax.ShapeDtypeStruct((2, b, s, d), F32),
            jax.ShapeDtypeStruct((2, b, s, d), BF16),
            jax.ShapeDtypeStruct((b, s, d), BF16),
        ],
        compiler_params=_cparams(("parallel", "parallel")),
        name="rwkv_lora",
    )(h, h, h, norm_g, mu3, w1, w2, w0, a1, a2, a0, g1, g2)


def _prefix_sum_rows(x, row_idx):
    shift = 1
    while shift < x.shape[0]:
        x = x + jnp.where(row_idx >= shift, pltpu.roll(x, shift, axis=0), 0.0)
        shift *= 2
    return x


def _head_groups(lanes, value):
    li = lax.broadcasted_iota(jnp.int32, (lanes, lanes), 0) // HEAD
    lj = lax.broadcasted_iota(jnp.int32, (lanes, lanes), 1) // HEAD
    return jnp.where(li == lj, value, 0.0).astype(BF16)


def _stack(x, lane_head):
    return jnp.concatenate([jnp.where(lane_head == 0, x, 0.0), jnp.where(lane_head == 1, x, 0.0)], axis=0)


def _wkv_body(r_ref, k_ref, v_ref, lw_ref, alr_ref, kkp_ref, kap_ref, rkp_ref, y_ref, bonus_ref, s_ref):
    c = pl.program_id(2)
    fwd = pl.program_id(1) == 0
    nb, ch, d = r_ref.shape
    rows = 2 * ch
    n_pairs = d // PAIR

    @pl.when(c == 0)
    def _():
        s_ref[...] = jnp.zeros_like(s_ref)

    row = lax.broadcasted_iota(jnp.int32, (rows, rows), 0)
    col = lax.broadcasted_iota(jnp.int32, (rows, rows), 1)
    t_loc, s_loc = row % ch, col % ch
    same_head = (row // ch) == (col // ch)
    before = (t_loc - s_loc) * jnp.where(fwd, 1, -1) > 0
    strict = same_head & before
    incl = same_head & (before | (s_loc == t_loc))
    eye = row == col
    tok_row = lax.broadcasted_iota(jnp.int32, (ch, PAIR), 0)
    lane_head = lax.broadcasted_iota(jnp.int32, (ch, PAIR), 1) // HEAD
    group = _head_groups(2 * PAIR, 1.0)

    def pair_steps(bb, p):
        sl = slice(p * PAIR, (p + 1) * PAIR)
        r, k, v = r_ref[bb, :, sl].astype(F32), k_ref[bb, :, sl].astype(F32), v_ref[bb, :, sl].astype(F32)
        lw, alr = lw_ref[bb, :, sl], alr_ref[bb, :, sl].astype(F32)
        run = _prefix_sum_rows(lw, tok_row)
        cl_end = run[ch - 1:ch]
        cl = jnp.where(fwd, run, cl_end - run + lw)
        kk_raw = k * kkp_ref[:, sl]
        kd = k * (1.0 + (alr - 1.0) * kap_ref[:, sl])
        sums = _dot(jnp.concatenate([kk_raw * kk_raw, r * kd * rkp_ref[:, sl]], axis=1).astype(BF16), group)
        yield
        kk = kk_raw / jnp.maximum(jnp.sqrt(sums[:, :PAIR]), 1e-12)
        bv = kk * alr
        bonus_ref[bb, :, sl] = (sums[:, PAIR:] * v).astype(bonus_ref.dtype)

        e_neg = jnp.exp(-cl)
        e_end = jnp.exp(cl_end - cl)
        a_t = _stack(-kk * jnp.exp(cl - lw), lane_head)
        r_t = _stack(r * jnp.exp(cl), lane_head)
        b_t = _stack(bv * e_neg, lane_head)
        k_t = _stack(kd * e_neg, lane_head)
        b_e = _stack(bv * e_end, lane_head).T.astype(BF16)
        k_e = _stack(kd * e_end, lane_head).T.astype(BF16)
        v_s = _stack(v, lane_head).astype(BF16)

        ar = jnp.concatenate([a_t, r_t], axis=0).astype(BF16)
        aa = _dot_nt(ar, jnp.concatenate([b_t, k_t], axis=0).astype(BF16))
        decay_end = jnp.where(eye, jnp.exp(cl_end), 0.0).astype(BF16)
        ars = _dot(jnp.concatenate([ar, decay_end], axis=0), s_ref[bb * n_pairs + p].astype(BF16))
        yield
        a_ab = jnp.where(strict, aa[:rows, :rows], 0.0)
        a_ak = jnp.where(strict, aa[:rows, rows:], 0.0).astype(BF16)
        a_rb = jnp.where(incl, aa[rows:, :rows], 0.0).astype(BF16)
        a_rk = jnp.where(incl, aa[rows:, rows:], 0.0).astype(BF16)
        zyh = _dot(jnp.concatenate([a_ak, a_rk, k_e], axis=0), v_s)

        n_pow = a_ab
        inv = jnp.where(eye, 1.0, 0.0) + a_ab
        for it in range(1, INV_STEPS):
            nb = n_pow.astype(BF16)
            if it == 1:
                n_pow = _dot(nb, nb)
            else:
                both = _dot(jnp.concatenate([inv, n_pow], axis=0).astype(BF16), nb)
                inv = inv + both[:rows]
                n_pow = both[rows:]
            yield
        if INV_STEPS > 1:
            inv = inv + _dot(inv.astype(BF16), n_pow.astype(BF16))
            yield
        u = _dot(inv.astype(BF16), (ars[:rows] + zyh[:rows]).astype(BF16)).astype(BF16)
        yield
        fin = _dot(jnp.concatenate([a_rb, b_e], axis=0), u)
        yield
        y_bd = ars[rows:2 * rows] + fin[:rows] + zyh[rows:2 * rows]
        y_ref[bb, :, sl] = (y_bd[:ch] + y_bd[ch:]).astype(y_ref.dtype)
        s_ref[bb * n_pairs + p] = ars[2 * rows:] + fin[rows:] + zyh[2 * rows:]

    _run_chains([pair_steps(bb, p) for bb in range(nb) for p in range(n_pairs)], WKV_WAVE, WKV_SKEW)


def _wkv(rkv, lw, alr, kkp, kap, rkp):
    _, b, s, d = rkv.shape
    nc = s // CHUNK
    assert 2 * CHUNK == PAIR

    def tok(bi, dr, c):
        return jnp.where(dr == 0, c, nc - 1 - c)

    nb = math.gcd(b, WKV_SEQS)
    rkv_spec = lambda j: pl.BlockSpec((None, nb, CHUNK, d), lambda bi, dr, c: (j, bi, tok(bi, dr, c), 0))
    dir_spec = pl.BlockSpec((None, nb, CHUNK, d), lambda bi, dr, c: (dr, bi, tok(bi, dr, c), 0))
    par_spec = pl.BlockSpec((1, d), lambda bi, dr, c: (0, 0))
    return pl.pallas_call(
        _wkv_body,
        grid=(b // nb, 2, nc),
        in_specs=[rkv_spec(0), rkv_spec(1), rkv_spec(2), dir_spec, dir_spec, par_spec, par_spec, par_spec],
        out_specs=[dir_spec, dir_spec],
        out_shape=[jax.ShapeDtypeStruct((2, b, s, d), BF16), jax.ShapeDtypeStruct((2, b, s, d), BF16)],
        scratch_shapes=[pltpu.VMEM((nb * (d // PAIR), PAIR, PAIR), F32)],
        compiler_params=_cparams(("parallel", "parallel", "arbitrary")),
        name="wkv",
    )(rkv, rkv, rkv, lw, alr, kkp, kap, rkp)


def _out_body(x_ref, y_ref, bonus_ref, gate_ref, lnw_ref, lnb_ref, wo_ref, o_ref):
    tm, d = x_ref.shape
    avg = _head_groups(GN_LANES, 1.0 / HEAD)

    def slab_steps(lo, hi):
        mixed = []
        for c0 in range(0, d, GN_LANES):
            sl = slice(c0, c0 + GN_LANES)
            y = y_ref[0, lo:hi, sl].astype(F32) + y_ref[1, lo:hi, sl].astype(F32)
            mean = _dot2(y, avg)
            dev = y - mean
            var = _dot((dev * dev).astype(BF16), avg)
            gn = dev * lax.rsqrt(var + GN_EPS) * lnw_ref[:, sl] + lnb_ref[:, sl]
            bonus = bonus_ref[0, lo:hi, sl].astype(F32) + bonus_ref[1, lo:hi, sl].astype(F32)
            mixed.append(((gn + bonus) * gate_ref[lo:hi, sl].astype(F32)).astype(BF16))
            yield
        mixed = jnp.concatenate(mixed, axis=1)
        for c0 in range(0, d, GN_LANES):
            sl = slice(c0, c0 + GN_LANES)
            o_ref[lo:hi, sl] = x_ref[lo:hi, sl] + _dot(mixed, wo_ref[:, sl])
            yield

    slab = min(tm, ROW_SLAB)
    _run_chains([slab_steps(lo, lo + slab) for lo in range(0, tm, slab)], skew=d // GN_LANES)


def _rwkv_out(h, y, bonus, gate, lnw, lnb, wo, *, tm):
    b, s, d = h.shape
    tok = pl.BlockSpec((None, tm, d), lambda bi, i: (bi, i, 0))
    both = pl.BlockSpec((2, None, tm, d), lambda bi, i: (0, bi, i, 0))
    par = pl.BlockSpec((1, d), lambda bi, i: (0, 0))
    return pl.pallas_call(
        _out_body,
        grid=(b, s // tm),
        in_specs=[tok, both, both, tok, par, par, pl.BlockSpec((d, d), lambda bi, i: (0, 0))],
        out_specs=tok,
        out_shape=jax.ShapeDtypeStruct((b, s, d), F32),
        compiler_params=_cparams(("parallel", "parallel")),
        name="rwkv_out",
    )(h, y, bonus, gate, lnw, lnb, wo)


def _pad_lora(w_in, w_out):
    r = w_in.shape[-1]
    pad = (-r) % 128
    w_in = jnp.pad(w_in, [(0, 0)] * (w_in.ndim - 1) + [(0, pad)])
    w_out = jnp.pad(w_out, [(0, 0)] * (w_out.ndim - 2) + [(0, pad), (0, 0)])
    return w_in.astype(BF16), w_out.astype(BF16)


def _tile(n, want):
    t = min(n, want)
    while n % t:
        t //= 2
    return t


def _rwkv_mixer(h, norm_g, mu, w_r, w_k, w_v, w_o, w0, w1, w2, a0, a1, a2, g1, g2, k_k, k_a, r_k, ln_w, ln_b):
    b, s, d = h.shape
    row = lambda z: z.reshape(1, d)
    tm2 = _tile(s, 2 * ROW_SLAB)
    rkv = _rwkv_rkv(h, norm_g, mu[jnp.array([0, 2, 3])].reshape(3, 1, d), w_r, w_k, w_v, tm=tm2)
    w1p, w2p = _pad_lora(w1, w2)
    a1p, a2p = _pad_lora(a1, a2)
    lw, alr, gate = _rwkv_lora(h, norm_g, mu[jnp.array([1, 4, 5])].reshape(3, 1, d),
                               w1p, w2p, w0.reshape(2, 1, d), a1p, a2p, a0.reshape(2, 1, d),
                               g1.astype(BF16), g2.astype(BF16), tm=tm2)
    y, bonus = _wkv(rkv, lw, alr, row(k_k), row(k_a), row(r_k))
    return _rwkv_out(h, y, bonus, gate, row(ln_w), row(ln_b), w_o, tm=tm2)


def kernel(x, ffn1_norm, ffn1_gate, ffn1_up, ffn1_down, mix_norm, ffn2_norm, ffn2_gate, ffn2_up, ffn2_down, pool_w, pool_scale, rwkv_mu, rwkv_wr, rwkv_wk, rwkv_wv, rwkv_wo, rwkv_w0, rwkv_w1, rwkv_w2, rwkv_a0, rwkv_a1, rwkv_a2, rwkv_g1, rwkv_g2, rwkv_kk, rwkv_ka, rwkv_rk, rwkv_lnw, rwkv_lnb, final_norm):
    b, s, d = x.shape
    depth = ffn1_norm.shape[0]
    f = ffn1_gate.shape[2]
    row = lambda z: z.reshape(1, d)
    tm_ffn, tf = _tile(b * s, 1024), _tile(f, 512)
    ffn_params = ((ffn1_norm, ffn1_gate, ffn1_up, ffn1_down), (ffn2_norm, ffn2_gate, ffn2_up, ffn2_down))
    order = [(i, half) for i in range(depth) for half in (0, 1)]

    weights = [ffn_params[0][k][0].astype(BF16) for k in (1, 2, 3)]
    rwkv_w = None
    h = x
    for q, (i, half) in enumerate(order):
        j = i // 2
        if half == 1:
            if i % 2 == 0:
                h = _pool(h, row(mix_norm[i]), pool_w[j].astype(BF16), row(pool_scale[j]), ts=_tile(s, 1024))
            else:
                h = _rwkv_mixer(h, row(mix_norm[i]), rwkv_mu[j], *rwkv_w,
                                rwkv_w0[j], rwkv_w1[j], rwkv_w2[j], rwkv_a0[j], rwkv_a1[j], rwkv_a2[j],
                                rwkv_g1[j], rwkv_g2[j], rwkv_kk[j], rwkv_ka[j], rwkv_rk[j],
                                rwkv_lnw[j], rwkv_lnb[j])
        casts = []
        if q + 1 < len(order):
            ni, nhalf = order[q + 1]
            casts += [(ffn_params[nhalf][k], ni) for k in (1, 2, 3)]
        if half == 0 and i % 2 == 1:
            casts += [(w, j) for w in (rwkv_wr, rwkv_wk, rwkv_wv, rwkv_wo)]
        out, cast = _ffn(h.reshape(b * s, d), row(ffn_params[half][0][i]), *weights, row(final_norm), casts,
                         final_norm=q == len(order) - 1, tm=tm_ffn, tf=tf)
        h = out.reshape(b, s, d)
        if q + 1 < len(order):
            weights, cast = cast[:3], cast[3:]
        if cast:
            rwkv_w = cast
    return h
```

```python
import functools
import math

import jax
import jax.numpy as jnp
from jax import lax
from jax.experimental import pallas as pl
from jax.experimental.pallas import tpu as pltpu

F32 = jnp.float32
BF16 = jnp.bfloat16

RMS_EPS = 1e-6
GN_EPS = 64e-5
HEAD = 64
PAIR = 2 * HEAD
CHUNK = 64
POOL_WINDOWS = (2, 4, 8, 16)
POOL_HALO = 16
SHIFT_HALO = 8
INV_STEPS = (CHUNK - 1).bit_length()
GN_LANES = 256
WKV_SEQS = 4
WKV_WAVE = 2
WKV_SKEW = 1
ROW_GROUP = 16
ROW_SLAB = 256
_DONE = object()
FFN_DOWN_COLS = 512
VMEM_LIMIT = 60 * 1024 * 1024


def _cparams(sem):
    return pltpu.CompilerParams(dimension_semantics=sem, vmem_limit_bytes=VMEM_LIMIT)


def _rms(x, g):
    return x * lax.rsqrt(jnp.mean(x * x, axis=-1, keepdims=True) + RMS_EPS) * g


def _sigmoid(x):
    return 0.5 * jnp.tanh(0.5 * x) + 0.5


def _dot(a, b):
    return jnp.dot(a, b, preferred_element_type=F32)


def _dot_nt(a, b):
    return lax.dot_general(a, b, (((1,), (1,)), ((), ())), preferred_element_type=F32)


def _dot_tn(a, b):
    return lax.dot_general(a, b, (((0,), (0,)), ((), ())), preferred_element_type=F32)


def _split2(x):
    hi = x.astype(BF16)
    lo = (x - hi.astype(F32)).astype(BF16)
    return hi, lo


def _split3(x):
    hi = x.astype(BF16)
    r1 = x - hi.astype(F32)
    mid = r1.astype(BF16)
    lo = (r1 - mid.astype(F32)).astype(BF16)
    return hi, mid, lo


def _dot2(x, m_bf16):
    hi, lo = _split2(x)
    return _dot(hi, m_bf16) + _dot(lo, m_bf16)


def _ffn_body(*refs, final_norm, n_cast):
    x_ref, g_ref, wg_ref, wu_ref, wd_ref, fg_ref = refs[:6]
    cast_src = refs[6:6 + n_cast]
    o_ref = refs[6 + n_cast]
    cast_dst = refs[7 + n_cast:7 + 2 * n_cast]
    n_ref = refs[7 + 2 * n_cast]
    f = pl.program_id(1)
    d = o_ref.shape[1]

    @pl.when(f == 0)
    def _():
        x = x_ref[...]
        n_ref[...] = _rms(x, g_ref[...]).astype(BF16)
        o_ref[...] = x

    n = n_ref[...]
    gate = _dot(n, wg_ref[...])
    up = _dot(n, wu_ref[...])
    act = (0.5 * gate * jax.nn.sigmoid(gate) * up).astype(BF16)
    for c0 in range(0, d, FFN_DOWN_COLS):
        cols = slice(c0, c0 + FFN_DOWN_COLS)
        o_ref[:, cols] += _dot(act, wd_ref[:, cols])

    for src, dst in zip(cast_src, cast_dst):
        dst[...] = src[...].astype(dst.dtype)

    if final_norm:
        @pl.when(f == pl.num_programs(1) - 1)
        def _():
            o_ref[...] = _rms(o_ref[...], fg_ref[...])


def _cast_specs(stacked, layer, ni, nj):
    _, rows, cols = stacked.shape
    ok = lambda n, parts, unit: n % parts == 0 and (n // parts) % unit == 0
    if ok(rows, ni, 16) and ok(cols, nj, 128):
        blk, imap = (rows // ni, cols // nj), lambda i, j: (i, j)
    elif ok(rows, nj, 16) and ok(cols, ni, 128):
        blk, imap = (rows // nj, cols // ni), lambda i, j: (j, i)
    else:
        assert ok(rows, ni, 16), (rows, ni)
        nc = 1
        while nc * 2 <= nj and ok(cols, nc * 2, 128):
            nc *= 2
        blk, imap = (rows // ni, cols // nc), lambda i, j: (i, jnp.minimum(j, nc - 1))
    in_spec = pl.BlockSpec((None,) + blk, lambda i, j: (layer,) + imap(i, j))
    return in_spec, pl.BlockSpec(blk, imap), jax.ShapeDtypeStruct((rows, cols), BF16)


def _ffn(h, norm_g, wg, wu, wd, final_g, casts, *, final_norm, tm, tf):
    t, d = h.shape
    f = wg.shape[1]
    ni, nj = t // tm, f // tf
    specs = [_cast_specs(w, layer, ni, nj) for w, layer in casts]
    outs = pl.pallas_call(
        functools.partial(_ffn_body, final_norm=final_norm, n_cast=len(casts)),
        grid=(ni, nj),
        in_specs=[
            pl.BlockSpec((tm, d), lambda i, j: (i, 0)),
            pl.BlockSpec((1, d), lambda i, j: (0, 0)),
            pl.BlockSpec((d, tf), lambda i, j: (0, j)),
            pl.BlockSpec((d, tf), lambda i, j: (0, j)),
            pl.BlockSpec((tf, d), lambda i, j: (j, 0)),
            pl.BlockSpec((1, d), lambda i, j: (0, 0)),
        ] + [sp[0] for sp in specs],
        out_specs=[pl.BlockSpec((tm, d), lambda i, j: (i, 0))] + [sp[1] for sp in specs],
        out_shape=[jax.ShapeDtypeStruct((t, d), F32)] + [sp[2] for sp in specs],
        scratch_shapes=[pltpu.VMEM((tm, d), BF16)],
        compiler_params=_cparams(("parallel", "arbitrary")),
        name="ffn",
    )(h, norm_g, wg, wu, wd, final_g, *[w for w, _ in casts])
    return outs[0], list(outs[1:])


def _pool_body(x_ref, xp_ref, xn_ref, g_ref, w_ref, sc_ref, o_ref, ext_ref, *, seq):
    i = pl.program_id(1)
    ts = x_ref.shape[0]
    cg = w_ref.shape[1]
    g = g_ref[...]
    first = i == 0
    last = i == pl.num_programs(1) - 1
    ext_ref[0:POOL_HALO, :] = jnp.where(first, 0.0, _rms(xp_ref[...], g))
    ext_ref[POOL_HALO:POOL_HALO + ts, :] = _rms(x_ref[...], g)
    ext_ref[POOL_HALO + ts:POOL_HALO + ts + POOL_HALO, :] = jnp.where(last, 0.0, _rms(xn_ref[...], g))

    t_glob = i * ts + lax.broadcasted_iota(jnp.int32, (ts, 1), 0)
    for gi, w in enumerate(POOL_WINDOWS):
        lanes = slice(gi * cg, (gi + 1) * cg)
        acc = ext_ref[POOL_HALO - w // 2:POOL_HALO - w // 2 + ts, lanes]
        for o in range(1, w):
            acc = acc + ext_ref[POOL_HALO - w // 2 + o:POOL_HALO - w // 2 + o + ts, lanes]
        lo = jnp.clip(t_glob - w // 2, 0, seq)
        hi = jnp.clip(t_glob - w // 2 + w, 0, seq)
        cnt = (hi - lo).astype(F32)
        pooled = acc / cnt - ext_ref[POOL_HALO:POOL_HALO + ts, lanes]
        y = _dot(pooled.astype(BF16), w_ref[gi])
        o_ref[:, lanes] = x_ref[:, lanes] + y * sc_ref[:, lanes]


def _pool(h, norm_g, w_grp, scale, *, ts):
    b, s, d = h.shape
    ng, cg, _ = w_grp.shape
    nb = ts // POOL_HALO
    last_blk = s // POOL_HALO - 1
    return pl.pallas_call(
        functools.partial(_pool_body, seq=s),
        grid=(b, s // ts),
        in_specs=[
            pl.BlockSpec((None, ts, d), lambda bi, i: (bi, i, 0)),
            pl.BlockSpec((None, POOL_HALO, d), lambda bi, i: (bi, jnp.maximum(i * nb - 1, 0), 0)),
            pl.BlockSpec((None, POOL_HALO, d), lambda bi, i: (bi, jnp.minimum((i + 1) * nb, last_blk), 0)),
            pl.BlockSpec((1, d), lambda bi, i: (0, 0)),
            pl.BlockSpec((ng, cg, cg), lambda bi, i: (0, 0, 0)),
            pl.BlockSpec((1, d), lambda bi, i: (0, 0)),
        ],
        out_specs=pl.BlockSpec((None, ts, d), lambda bi, i: (bi, i, 0)),
        out_shape=jax.ShapeDtypeStruct((b, s, d), F32),
        scratch_shapes=[pltpu.VMEM((ts + 2 * POOL_HALO, d), F32)],
        compiler_params=_cparams(("parallel", "parallel")),
        name="pool",
    )(h, h, h, norm_g, w_grp, scale)


def _norm_and_shift(x_ref, xp_ref, xn_ref, g, i, n_tiles, lo=0, hi=None):
    tm = x_ref.shape[0]
    hi = tm if hi is None else hi
    n = hi - lo
    hn = _rms(x_ref[lo:hi, :], g)
    if lo == 0:
        prev_row = jnp.where(i == 0, 0.0, _rms(xp_ref[...], g)[SHIFT_HALO - 1:SHIFT_HALO])
    else:
        prev_row = _rms(x_ref[lo - SHIFT_HALO:lo, :], g)[SHIFT_HALO - 1:SHIFT_HALO]
    if hi == tm:
        next_row = jnp.where(i == n_tiles - 1, 0.0, _rms(xn_ref[...], g)[0:1])
    else:
        next_row = _rms(x_ref[hi:hi + SHIFT_HALO, :], g)[0:1]
    row = lax.broadcasted_iota(jnp.int32, (n, 1), 0)
    prev = jnp.where(row == 0, prev_row, pltpu.roll(hn, 1, axis=0))
    nxt = jnp.where(row == n - 1, next_row, pltpu.roll(hn, n - 1, axis=0))
    return hn, 0.5 * (prev + nxt) - hn


def _run_chains(chains, wave=1, skew=1):
    live = list(range(len(chains)))
    tick = 0
    while live:
        live = [i for i in live if (i // wave) * skew > tick or next(chains[i], _DONE) is not _DONE]
        tick += 1


def _shift_specs(s, d, tm):
    nb = tm // SHIFT_HALO
    last_blk = s // SHIFT_HALO - 1
    return [
        pl.BlockSpec((None, tm, d), lambda bi, i, *_: (bi, i, 0)),
        pl.BlockSpec((None, SHIFT_HALO, d), lambda bi, i, *_: (bi, jnp.maximum(i * nb - 1, 0), 0)),
        pl.BlockSpec((None, SHIFT_HALO, d), lambda bi, i, *_: (bi, jnp.minimum((i + 1) * nb, last_blk), 0)),
    ]


def _rkv_body(x_ref, xp_ref, xn_ref, g_ref, mu_ref, wr_ref, wk_ref, wv_ref, o_ref):
    tm = x_ref.shape[0]

    def slab_steps(lo, hi):
        hn, xx = _norm_and_shift(x_ref, xp_ref, xn_ref, g_ref[...], pl.program_id(1), pl.num_programs(1), lo, hi)
        yield
        for j, w_ref in enumerate((wr_ref, wk_ref, wv_ref)):
            xm = (hn + xx * mu_ref[j]).astype(BF16)
            o_ref[j, lo:hi, :] = _dot(xm, w_ref[...]).astype(o_ref.dtype)
            yield

    slab = min(tm, ROW_SLAB)
    _run_chains([slab_steps(lo, lo + slab) for lo in range(0, tm, slab)])


def _rwkv_rkv(h, norm_g, mu3, w_r, w_k, w_v, *, tm):
    b, s, d = h.shape
    return pl.pallas_call(
        _rkv_body,
        grid=(b, s // tm),
        in_specs=_shift_specs(s, d, tm) + [
            pl.BlockSpec((1, d), lambda bi, i: (0, 0)),
            pl.BlockSpec((3, 1, d), lambda bi, i: (0, 0, 0)),
        ] + [pl.BlockSpec((d, d), lambda bi, i: (0, 0))] * 3,
        out_specs=pl.BlockSpec((3, None, tm, d), lambda bi, i: (0, bi, i, 0)),
        out_shape=jax.ShapeDtypeStruct((3, b, s, d), BF16),
        compiler_params=_cparams(("parallel", "parallel")),
        name="rwkv_rkv",
    )(h, h, h, norm_g, mu3, w_r, w_k, w_v)


def _lora_body(x_ref, xp_ref, xn_ref, g_ref, mu_ref, w1_ref, w2_ref, w0_ref, a1_ref, a2_ref, a0_ref,
               g1_ref, g2_ref, lw_ref, alr_ref, gate_ref):
    tm = x_ref.shape[0]
    neg_rate = -math.exp(-0.5)

    def slab_steps(lo, hi):
        hn, xx = _norm_and_shift(x_ref, xp_ref, xn_ref, g_ref[...], pl.program_id(1), pl.num_programs(1), lo, hi)
        xw = (hn + xx * mu_ref[0]).astype(BF16)
        xa = (hn + xx * mu_ref[1]).astype(BF16)
        xg = (hn + xx * mu_ref[2]).astype(BF16)
        tw = [_dot(xw, w1_ref[dr]) for dr in range(2)]
        ta = [_dot(xa, a1_ref[dr]) for dr in range(2)]
        tg = _dot(xg, g1_ref[...])
        yield
        z = [_dot(jnp.tanh(tw[dr]).astype(BF16), w2_ref[dr]) for dr in range(2)]
        av = [_dot(ta[dr].astype(BF16), a2_ref[dr]) for dr in range(2)]
        gate = _dot(_sigmoid(tg).astype(BF16), g2_ref[...])
        yield
        for dr in range(2):
            lw_ref[dr, lo:hi, :] = neg_rate * _sigmoid(w0_ref[dr] + z[dr])
            alr_ref[dr, lo:hi, :] = _sigmoid(a0_ref[dr] + av[dr]).astype(alr_ref.dtype)
        gate_ref[lo:hi, :] = gate.astype(gate_ref.dtype)

    slab = min(tm, ROW_SLAB)
    _run_chains([slab_steps(lo, lo + slab) for lo in range(0, tm, slab)])


def _rwkv_lora(h, norm_g, mu3, w1, w2, w0, a1, a2, a0, g1, g2, *, tm):
    b, s, d = h.shape
    lw_, la_, lg_ = w1.shape[2], a1.shape[2], g1.shape[1]
    full = lambda *shape: pl.BlockSpec(shape, lambda bi, i: (0,) * len(shape))
    return pl.pallas_call(
        _lora_body,
        grid=(b, s // tm),
        in_specs=_shift_specs(s, d, tm) + [
            full(1, d), full(3, 1, d),
            full(2, d, lw_), full(2, lw_, d), full(2, 1, d),
            full(2, d, la_), full(2, la_, d), full(2, 1, d),
            full(d, lg_), full(lg_, d),
        ],
        out_specs=[
            pl.BlockSpec((2, None, tm, d), lambda bi, i: (0, bi, i, 0)),
            pl.BlockSpec((2, None, tm, d), lambda bi, i: (0, bi, i, 0)),
            pl.BlockSpec((None, tm, d), lambda bi, i: (bi, i, 0)),
        ],
        out_shape=[
            jax.ShapeDtypeStruct((2, b, s, d), F32),
            jax.ShapeDtypeStruct((2, b, s, d), BF16),
            jax.ShapeDtypeStruct((b, s, d), BF16),
        ],
        compiler_params=_cparams(("parallel", "parallel")),
        name="rwkv_lora",
    )(h, h, h, norm_g, mu3, w1, w2, w0, a1, a2, a0, g1, g2)


def _prefix_sum_rows(x, row_idx):
    shift = 1
    while shift < x.shape[0]:
        x = x + jnp.where(row_idx >= shift, pltpu.roll(x, shift, axis=0), 0.0)
        shift *= 2
    return x


def _head_groups(lanes, value):
    li = lax.broadcasted_iota(jnp.int32, (lanes, lanes), 0) // HEAD
    lj = lax.broadcasted_iota(jnp.int32, (lanes, lanes), 1) // HEAD
    return jnp.where(li == lj, value, 0.0).astype(BF16)


def _stack(x, lane_head):
    return jnp.concatenate([jnp.where(lane_head == 0, x, 0.0), jnp.where(lane_head == 1, x, 0.0)], axis=0)


def _keep_rows(x, ch, skip, reverse):
    if skip == 0:
        return x
    blocks = [x[h * ch:(h + 1) * ch - skip] if reverse else x[h * ch + skip:(h + 1) * ch] for h in range(x.shape[0] // ch)]
    return jnp.concatenate(blocks, axis=0)


def _restore_rows(y, ch, skip, reverse, base=None):
    if skip == 0:
        return y if base is None else base + y
    keep = ch - skip
    out = []
    for h in range(y.shape[0] // keep):
        part = y[h * keep:(h + 1) * keep]
        if base is None:
            fill = jnp.zeros((skip, y.shape[1]), y.dtype)
        else:
            blk = base[h * ch:(h + 1) * ch]
            fill = blk[keep:] if reverse else blk[:skip]
            part = part + (blk[:keep] if reverse else blk[skip:])
        out += [part, fill] if reverse else [fill, part]
    return jnp.concatenate(out, axis=0)


def _wkv_body(r_ref, k_ref, v_ref, lw_ref, alr_ref, kkp_ref, kap_ref, rkp_ref, y_ref, bonus_ref, s_ref, *, reverse):
    c = pl.program_id(1)
    nb, ch, d = r_ref.shape
    rows = 2 * ch
    n_pairs = d // PAIR

    @pl.when(c == 0)
    def _():
        s_ref[...] = jnp.zeros_like(s_ref)

    row = lax.broadcasted_iota(jnp.int32, (rows, rows), 0)
    col = lax.broadcasted_iota(jnp.int32, (rows, rows), 1)
    t_loc, s_loc = row % ch, col % ch
    same_head = (row // ch) == (col // ch)
    before = (s_loc > t_loc) if reverse else (s_loc < t_loc)
    strict = same_head & before
    incl = same_head & (before | (s_loc == t_loc))
    eye = row == col
    tok_row = lax.broadcasted_iota(jnp.int32, (ch, PAIR), 0)
    lane_head = lax.broadcasted_iota(jnp.int32, (ch, PAIR), 1) // HEAD
    group = _head_groups(2 * PAIR, 1.0)

    def pair_steps(bb, p):
        sl = slice(p * PAIR, (p + 1) * PAIR)
        r, k, v = r_ref[bb, :, sl].astype(F32), k_ref[bb, :, sl].astype(F32), v_ref[bb, :, sl].astype(F32)
        lw, alr = lw_ref[bb, :, sl], alr_ref[bb, :, sl].astype(F32)
        run = _prefix_sum_rows(lw, tok_row)
        cl_end = run[ch - 1:ch]
        cl = cl_end - run + lw if reverse else run
        kk_raw = k * kkp_ref[:, sl]
        kd = k * (1.0 + (alr - 1.0) * kap_ref[:, sl])
        sums = _dot(jnp.concatenate([kk_raw * kk_raw, r * kd * rkp_ref[:, sl]], axis=1).astype(BF16), group)
        yield
        kk = kk_raw / jnp.maximum(jnp.sqrt(sums[:, :PAIR]), 1e-12)
        bv = kk * alr
        bonus_ref[bb, :, sl] = (sums[:, PAIR:] * v).astype(bonus_ref.dtype)

        e_neg = jnp.exp(-cl)
        e_end = jnp.exp(cl_end - cl)
        a_t = _stack(-kk * jnp.exp(cl - lw), lane_head)
        r_t = _stack(r * jnp.exp(cl), lane_head)
        b_t = _stack(bv * e_neg, lane_head)
        k_t = _stack(kd * e_neg, lane_head)
        b_e = _stack(bv * e_end, lane_head).T.astype(BF16)
        k_e = _stack(kd * e_end, lane_head).T.astype(BF16)
        v_s = _stack(v, lane_head).astype(BF16)

        ar = jnp.concatenate([a_t, r_t], axis=0).astype(BF16)
        aa = _dot_nt(ar, jnp.concatenate([b_t, k_t], axis=0).astype(BF16))
        decay_end = jnp.where(eye, jnp.exp(cl_end), 0.0).astype(BF16)
        ars = _dot(jnp.concatenate([ar, decay_end], axis=0), s_ref[bb * n_pairs + p].astype(BF16))
        yield
        a_ab = jnp.where(strict, aa[:rows, :rows], 0.0)
        a_ak = jnp.where(strict, aa[:rows, rows:], 0.0).astype(BF16)
        a_rb = jnp.where(incl, aa[rows:, :rows], 0.0).astype(BF16)
        a_rk = jnp.where(incl, aa[rows:, rows:], 0.0).astype(BF16)
        zyh = _dot(jnp.concatenate([a_ak, a_rk, k_e], axis=0), v_s)

        n_pow = a_ab
        inv = jnp.where(eye, 1.0, 0.0) + a_ab
        for it in range(1, INV_STEPS + 1):
            power = 2 ** (it - 1)
            nw = n_pow.astype(BF16)
            skip_inv = min(ch, power // ROW_GROUP * ROW_GROUP)
            skip_n = min(ch, 2 * power // ROW_GROUP * ROW_GROUP)
            if it == 1:
                n_pow = _dot(nw, nw)
            elif it < INV_STEPS:
                lhs = jnp.concatenate([_keep_rows(inv, ch, skip_inv, reverse), _keep_rows(n_pow, ch, skip_n, reverse)], axis=0)
                both = _dot(lhs.astype(BF16), nw)
                cut = rows - 2 * skip_inv
                inv = _restore_rows(both[:cut], ch, skip_inv, reverse, base=inv)
                n_pow = _restore_rows(both[cut:], ch, skip_n, reverse)
            else:
                inv = _restore_rows(_dot(_keep_rows(inv, ch, skip_inv, reverse).astype(BF16), nw), ch, skip_inv, reverse, base=inv)
            yield
        u = _dot(inv.astype(BF16), (ars[:rows] + zyh[:rows]).astype(BF16)).astype(BF16)
        yield
        fin = _dot(jnp.concatenate([a_rb, b_e], axis=0), u)
        yield
        y_bd = ars[rows:2 * rows] + fin[:rows] + zyh[rows:2 * rows]
        y_ref[bb, :, sl] = (y_bd[:ch] + y_bd[ch:]).astype(y_ref.dtype)
        s_ref[bb * n_pairs + p] = ars[2 * rows:] + fin[rows:] + zyh[2 * rows:]

    _run_chains([pair_steps(bb, p) for bb in range(nb) for p in range(n_pairs)], WKV_WAVE, WKV_SKEW)


def _wkv(rkv, lw, alr, kkp, kap, rkp, *, reverse):
    _, b, s, d = rkv.shape
    nc = s // CHUNK
    assert 2 * CHUNK == PAIR
    dr = int(reverse)
    tok = (lambda c: nc - 1 - c) if reverse else (lambda c: c)
    nb = math.gcd(b, WKV_SEQS)
    stacked = lambda j: pl.BlockSpec((None, nb, CHUNK, d), lambda bi, c: (j, bi, tok(c), 0))
    out_spec = pl.BlockSpec((nb, CHUNK, d), lambda bi, c: (bi, tok(c), 0))
    par_spec = pl.BlockSpec((1, d), lambda bi, c: (0, 0))
    return pl.pallas_call(
        functools.partial(_wkv_body, reverse=reverse),
        grid=(b // nb, nc),
        in_specs=[stacked(0), stacked(1), stacked(2), stacked(dr), stacked(dr), par_spec, par_spec, par_spec],
        out_specs=[out_spec, out_spec],
        out_shape=[jax.ShapeDtypeStruct((b, s, d), BF16), jax.ShapeDtypeStruct((b, s, d), BF16)],
        scratch_shapes=[pltpu.VMEM((nb * (d // PAIR), PAIR, PAIR), F32)],
        compiler_params=_cparams(("parallel", "arbitrary")),
        name="wkv_rev" if reverse else "wkv_fwd",
    )(rkv, rkv, rkv, lw, alr, kkp, kap, rkp)


def _out_body(x_ref, yf_ref, yr_ref, bf_ref, br_ref, gate_ref, lnw_ref, lnb_ref, wo_ref, o_ref):
    tm, d = x_ref.shape
    avg = _head_groups(GN_LANES, 1.0 / HEAD)

    def slab_steps(lo, hi):
        mixed = []
        for c0 in range(0, d, GN_LANES):
            sl = slice(c0, c0 + GN_LANES)
            y = yf_ref[lo:hi, sl].astype(F32) + yr_ref[lo:hi, sl].astype(F32)
            mean = _dot2(y, avg)
            dev = y - mean
            var = _dot((dev * dev).astype(BF16), avg)
            gn = dev * lax.rsqrt(var + GN_EPS) * lnw_ref[:, sl] + lnb_ref[:, sl]
            bonus = bf_ref[lo:hi, sl].astype(F32) + br_ref[lo:hi, sl].astype(F32)
            mixed.append(((gn + bonus) * gate_ref[lo:hi, sl].astype(F32)).astype(BF16))
            yield
        mixed = jnp.concatenate(mixed, axis=1)
        for c0 in range(0, d, GN_LANES):
            sl = slice(c0, c0 + GN_LANES)
            o_ref[lo:hi, sl] = x_ref[lo:hi, sl] + _dot(mixed, wo_ref[:, sl])
            yield

    slab = min(tm, ROW_SLAB)
    _run_chains([slab_steps(lo, lo + slab) for lo in range(0, tm, slab)], skew=d // GN_LANES)


def _rwkv_out(h, scans, gate, lnw, lnb, wo, *, tm):
    b, s, d = h.shape
    tok = pl.BlockSpec((None, tm, d), lambda bi, i: (bi, i, 0))
    par = pl.BlockSpec((1, d), lambda bi, i: (0, 0))
    return pl.pallas_call(
        _out_body,
        grid=(b, s // tm),
        in_specs=[tok] * 6 + [par, par, pl.BlockSpec((d, d), lambda bi, i: (0, 0))],
        out_specs=tok,
        out_shape=jax.ShapeDtypeStruct((b, s, d), F32),
        compiler_params=_cparams(("parallel", "parallel")),
        name="rwkv_out",
    )(h, *scans, gate, lnw, lnb, wo)


def _pad_lora(w_in, w_out):
    r = w_in.shape[-1]
    pad = (-r) % 128
    w_in = jnp.pad(w_in, [(0, 0)] * (w_in.ndim - 1) + [(0, pad)])
    w_out = jnp.pad(w_out, [(0, 0)] * (w_out.ndim - 2) + [(0, pad), (0, 0)])
    return w_in.astype(BF16), w_out.astype(BF16)


def _tile(n, want):
    t = min(n, want)
    while n % t:
        t //= 2
    return t


def _rwkv_mixer(h, norm_g, mu, w_r, w_k, w_v, w_o, w0, w1, w2, a0, a1, a2, g1, g2, k_k, k_a, r_k, ln_w, ln_b):
    b, s, d = h.shape
    row = lambda z: z.reshape(1, d)
    tm2 = _tile(s, 2 * ROW_SLAB)
    rkv = _rwkv_rkv(h, norm_g, mu[jnp.array([0, 2, 3])].reshape(3, 1, d), w_r, w_k, w_v, tm=tm2)
    w1p, w2p = _pad_lora(w1, w2)
    a1p, a2p = _pad_lora(a1, a2)
    lw, alr, gate = _rwkv_lora(h, norm_g, mu[jnp.array([1, 4, 5])].reshape(3, 1, d),
                               w1p, w2p, w0.reshape(2, 1, d), a1p, a2p, a0.reshape(2, 1, d),
                               g1.astype(BF16), g2.astype(BF16), tm=tm2)
    y_f, bonus_f = _wkv(rkv, lw, alr, row(k_k), row(k_a), row(r_k), reverse=False)
    y_r, bonus_r = _wkv(rkv, lw, alr, row(k_k), row(k_a), row(r_k), reverse=True)
    return _rwkv_out(h, (y_f, y_r, bonus_f, bonus_r), gate, row(ln_w), row(ln_b), w_o, tm=tm2)


def kernel(x, ffn1_norm, ffn1_gate, ffn1_up, ffn1_down, mix_norm, ffn2_norm, ffn2_gate, ffn2_up, ffn2_down, pool_w, pool_scale, rwkv_mu, rwkv_wr, rwkv_wk, rwkv_wv, rwkv_wo, rwkv_w0, rwkv_w1, rwkv_w2, rwkv_a0, rwkv_a1, rwkv_a2, rwkv_g1, rwkv_g2, rwkv_kk, rwkv_ka, rwkv_rk, rwkv_lnw, rwkv_lnb, final_norm):
    b, s, d = x.shape
    depth = ffn1_norm.shape[0]
    f = ffn1_gate.shape[2]
    row = lambda z: z.reshape(1, d)
    tm_ffn, tf = _tile(b * s, 1024), _tile(f, 512)
    ffn_params = ((ffn1_norm, ffn1_gate, ffn1_up, ffn1_down), (ffn2_norm, ffn2_gate, ffn2_up, ffn2_down))
    order = [(i, half) for i in range(depth) for half in (0, 1)]

    weights = [ffn_params[0][k][0].astype(BF16) for k in (1, 2, 3)]
    rwkv_w = None
    h = x
    for q, (i, half) in enumerate(order):
        j = i // 2
        if half == 1:
            if i % 2 == 0:
                h = _pool(h, row(mix_norm[i]), pool_w[j].astype(BF16), row(pool_scale[j]), ts=_tile(s, 1024))
            else:
                h = _rwkv_mixer(h, row(mix_norm[i]), rwkv_mu[j], *rwkv_w,
                                rwkv_w0[j], rwkv_w1[j], rwkv_w2[j], rwkv_a0[j], rwkv_a1[j], rwkv_a2[j],
                                rwkv_g1[j], rwkv_g2[j], rwkv_kk[j], rwkv_ka[j], rwkv_rk[j],
                                rwkv_lnw[j], rwkv_lnb[j])
        casts = []
        if q + 1 < len(order):
            ni, nhalf = order[q + 1]
            casts += [(ffn_params[nhalf][k], ni) for k in (1, 2, 3)]
        if half == 0 and i % 2 == 1:
            casts += [(w, j) for w in (rwkv_wr, rwkv_wk, rwkv_wv, rwkv_wo)]
        out, cast = _ffn(h.reshape(b * s, d), row(ffn_params[half][0][i]), *weights, row(final_norm), casts,
                         final_norm=q == len(order) - 1, tm=tm_ffn, tf=tf)
        h = out.reshape(b, s, d)
        if q + 1 < len(order):
            weights, cast = cast[:3], cast[3:]
        if cast:
            rwkv_w = cast
    return h
```

```python
import functools
import math

import jax
import jax.numpy as jnp
from jax import lax
from jax.experimental import pallas as pl
from jax.experimental.pallas import tpu as pltpu

F32 = jnp.float32
BF16 = jnp.bfloat16

RMS_EPS = 1e-6
GN_EPS = 64e-5
HEAD = 64
PAIR = 2 * HEAD
CHUNK = 64
POOL_WINDOWS = (2, 4, 8, 16)
POOL_HALO = 16
SHIFT_HALO = 8
INV_STEPS = (CHUNK - 1).bit_length()
GN_LANES = 256
WKV_SEQS = 4
WKV_WAVE = 2
WKV_SKEW = 1
ROW_GROUP = 16
ROW_SLAB = 256
_DONE = object()
FFN_DOWN_COLS = 512
VMEM_LIMIT = 60 * 1024 * 1024


def _cparams(sem):
    return pltpu.CompilerParams(dimension_semantics=sem, vmem_limit_bytes=VMEM_LIMIT)


def _rms(x, g):
    return x * lax.rsqrt(jnp.mean(x * x, axis=-1, keepdims=True) + RMS_EPS) * g


def _sigmoid(x):
    return 0.5 * jnp.tanh(0.5 * x) + 0.5


def _dot(a, b):
    return jnp.dot(a, b, preferred_element_type=F32)


def _dot_nt(a, b):
    return lax.dot_general(a, b, (((1,), (1,)), ((), ())), preferred_element_type=F32)


def _dot_tn(a, b):
    return lax.dot_general(a, b, (((0,), (0,)), ((), ())), preferred_element_type=F32)


def _split2(x):
    hi = x.astype(BF16)
    lo = (x - hi.astype(F32)).astype(BF16)
    return hi, lo


def _split3(x):
    hi = x.astype(BF16)
    r1 = x - hi.astype(F32)
    mid = r1.astype(BF16)
    lo = (r1 - mid.astype(F32)).astype(BF16)
    return hi, mid, lo


def _dot2(x, m_bf16):
    hi, lo = _split2(x)
    return _dot(hi, m_bf16) + _dot(lo, m_bf16)


def _ffn_body(*refs, final_norm, n_cast):
    x_ref, g_ref, wg_ref, wu_ref, wd_ref, fg_ref = refs[:6]
    cast_src = refs[6:6 + n_cast]
    o_ref = refs[6 + n_cast]
    cast_dst = refs[7 + n_cast:7 + 2 * n_cast]
    n_ref = refs[7 + 2 * n_cast]
    f = pl.program_id(1)
    d = o_ref.shape[1]

    @pl.when(f == 0)
    def _():
        x = x_ref[...]
        n_ref[...] = _rms(x, g_ref[...]).astype(BF16)
        o_ref[...] = x

    n = n_ref[...]
    gate = _dot(n, wg_ref[...])
    up = _dot(n, wu_ref[...])
    act = (0.5 * gate * jax.nn.sigmoid(gate) * up).astype(BF16)
    for c0 in range(0, d, FFN_DOWN_COLS):
        cols = slice(c0, c0 + FFN_DOWN_COLS)
        o_ref[:, cols] += _dot(act, wd_ref[:, cols])

    for src, dst in zip(cast_src, cast_dst):
        dst[...] = src[...].astype(dst.dtype)

    if final_norm:
        @pl.when(f == pl.num_programs(1) - 1)
        def _():
            o_ref[...] = _rms(o_ref[...], fg_ref[...])


def _cast_specs(stacked, layer, ni, nj):
    _, rows, cols = stacked.shape
    ok = lambda n, parts, unit: n % parts == 0 and (n // parts) % unit == 0
    if ok(rows, ni, 16) and ok(cols, nj, 128):
        blk, imap = (rows // ni, cols // nj), lambda i, j: (i, j)
    elif ok(rows, nj, 16) and ok(cols, ni, 128):
        blk, imap = (rows // nj, cols // ni), lambda i, j: (j, i)
    else:
        assert ok(rows, ni, 16), (rows, ni)
        nc = 1
        while nc * 2 <= nj and ok(cols, nc * 2, 128):
            nc *= 2
        blk, imap = (rows // ni, cols // nc), lambda i, j: (i, jnp.minimum(j, nc - 1))
    in_spec = pl.BlockSpec((None,) + blk, lambda i, j: (layer,) + imap(i, j))
    return in_spec, pl.BlockSpec(blk, imap), jax.ShapeDtypeStruct((rows, cols), BF16)


def _ffn(h, norm_g, wg, wu, wd, final_g, casts, *, final_norm, tm, tf):
    t, d = h.shape
    f = wg.shape[1]
    ni, nj = t // tm, f // tf
    specs = [_cast_specs(w, layer, ni, nj) for w, layer in casts]
    outs = pl.pallas_call(
        functools.partial(_ffn_body, final_norm=final_norm, n_cast=len(casts)),
        grid=(ni, nj),
        in_specs=[
            pl.BlockSpec((tm, d), lambda i, j: (i, 0)),
            pl.BlockSpec((1, d), lambda i, j: (0, 0)),
            pl.BlockSpec((d, tf), lambda i, j: (0, j)),
            pl.BlockSpec((d, tf), lambda i, j: (0, j)),
            pl.BlockSpec((tf, d), lambda i, j: (j, 0)),
            pl.BlockSpec((1, d), lambda i, j: (0, 0)),
        ] + [sp[0] for sp in specs],
        out_specs=[pl.BlockSpec((tm, d), lambda i, j: (i, 0))] + [sp[1] for sp in specs],
        out_shape=[jax.ShapeDtypeStruct((t, d), F32)] + [sp[2] for sp in specs],
        scratch_shapes=[pltpu.VMEM((tm, d), BF16)],
        compiler_params=_cparams(("parallel", "arbitrary")),
        name="ffn",
    )(h, norm_g, wg, wu, wd, final_g, *[w for w, _ in casts])
    return outs[0], list(outs[1:])


def _pool_body(x_ref, xp_ref, xn_ref, g_ref, w_ref, sc_ref, o_ref, ext_ref, *, seq):
    i = pl.program_id(1)
    ts = x_ref.shape[0]
    cg = w_ref.shape[1]
    g = g_ref[...]
    first = i == 0
    last = i == pl.num_programs(1) - 1
    ext_ref[0:POOL_HALO, :] = jnp.where(first, 0.0, _rms(xp_ref[...], g))
    ext_ref[POOL_HALO:POOL_HALO + ts, :] = _rms(x_ref[...], g)
    ext_ref[POOL_HALO + ts:POOL_HALO + ts + POOL_HALO, :] = jnp.where(last, 0.0, _rms(xn_ref[...], g))

    t_glob = i * ts + lax.broadcasted_iota(jnp.int32, (ts, 1), 0)
    for gi, w in enumerate(POOL_WINDOWS):
        lanes = slice(gi * cg, (gi + 1) * cg)
        acc = ext_ref[POOL_HALO - w // 2:POOL_HALO - w // 2 + ts, lanes]
        for o in range(1, w):
            acc = acc + ext_ref[POOL_HALO - w // 2 + o:POOL_HALO - w // 2 + o + ts, lanes]
        lo = jnp.clip(t_glob - w // 2, 0, seq)
        hi = jnp.clip(t_glob - w // 2 + w, 0, seq)
        cnt = (hi - lo).astype(F32)
        pooled = acc / cnt - ext_ref[POOL_HALO:POOL_HALO + ts, lanes]
        y = _dot(pooled.astype(BF16), w_ref[gi])
        o_ref[:, lanes] = x_ref[:, lanes] + y * sc_ref[:, lanes]


def _pool(h, norm_g, w_grp, scale, *, ts):
    b, s, d = h.shape
    ng, cg, _ = w_grp.shape
    nb = ts // POOL_HALO
    last_blk = s // POOL_HALO - 1
    return pl.pallas_call(
        functools.partial(_pool_body, seq=s),
        grid=(b, s // ts),
        in_specs=[
            pl.BlockSpec((None, ts, d), lambda bi, i: (bi, i, 0)),
            pl.BlockSpec((None, POOL_HALO, d), lambda bi, i: (bi, jnp.maximum(i * nb - 1, 0), 0)),
            pl.BlockSpec((None, POOL_HALO, d), lambda bi, i: (bi, jnp.minimum((i + 1) * nb, last_blk), 0)),
            pl.BlockSpec((1, d), lambda bi, i: (0, 0)),
            pl.BlockSpec((ng, cg, cg), lambda bi, i: (0, 0, 0)),
            pl.BlockSpec((1, d), lambda bi, i: (0, 0)),
        ],
        out_specs=pl.BlockSpec((None, ts, d), lambda bi, i: (bi, i, 0)),
        out_shape=jax.ShapeDtypeStruct((b, s, d), F32),
        scratch_shapes=[pltpu.VMEM((ts + 2 * POOL_HALO, d), F32)],
        compiler_params=_cparams(("parallel", "parallel")),
        name="pool",
    )(h, h, h, norm_g, w_grp, scale)


def _norm_and_shift(x_ref, xp_ref, xn_ref, g, i, n_tiles, lo=0, hi=None):
    tm = x_ref.shape[0]
    hi = tm if hi is None else hi
    n = hi - lo
    hn = _rms(x_ref[lo:hi, :], g)
    if lo == 0:
        prev_row = jnp.where(i == 0, 0.0, _rms(xp_ref[...], g)[SHIFT_HALO - 1:SHIFT_HALO])
    else:
        prev_row = _rms(x_ref[lo - SHIFT_HALO:lo, :], g)[SHIFT_HALO - 1:SHIFT_HALO]
    if hi == tm:
        next_row = jnp.where(i == n_tiles - 1, 0.0, _rms(xn_ref[...], g)[0:1])
    else:
        next_row = _rms(x_ref[hi:hi + SHIFT_HALO, :], g)[0:1]
    row = lax.broadcasted_iota(jnp.int32, (n, 1), 0)
    prev = jnp.where(row == 0, prev_row, pltpu.roll(hn, 1, axis=0))
    nxt = jnp.where(row == n - 1, next_row, pltpu.roll(hn, n - 1, axis=0))
    return hn, 0.5 * (prev + nxt) - hn


def _run_chains(chains, wave=1, skew=1):
    live = list(range(len(chains)))
    tick = 0
    while live:
        live = [i for i in live if (i // wave) * skew > tick or next(chains[i], _DONE) is not _DONE]
        tick += 1


def _shift_specs(s, d, tm):
    nb = tm // SHIFT_HALO
    last_blk = s // SHIFT_HALO - 1
    return [
        pl.BlockSpec((None, tm, d), lambda bi, i, *_: (bi, i, 0)),
        pl.BlockSpec((None, SHIFT_HALO, d), lambda bi, i, *_: (bi, jnp.maximum(i * nb - 1, 0), 0)),
        pl.BlockSpec((None, SHIFT_HALO, d), lambda bi, i, *_: (bi, jnp.minimum((i + 1) * nb, last_blk), 0)),
    ]


def _rkv_body(x_ref, xp_ref, xn_ref, g_ref, mu_ref, wr_ref, wk_ref, wv_ref, o_ref):
    tm = x_ref.shape[0]

    def slab_steps(lo, hi):
        hn, xx = _norm_and_shift(x_ref, xp_ref, xn_ref, g_ref[...], pl.program_id(1), pl.num_programs(1), lo, hi)
        yield
        for j, w_ref in enumerate((wr_ref, wk_ref, wv_ref)):
            xm = (hn + xx * mu_ref[j]).astype(BF16)
            o_ref[j, lo:hi, :] = _dot(xm, w_ref[...]).astype(o_ref.dtype)
            yield

    slab = min(tm, ROW_SLAB)
    _run_chains([slab_steps(lo, lo + slab) for lo in range(0, tm, slab)])


def _rwkv_rkv(h, norm_g, mu3, w_r, w_k, w_v, *, tm):
    b, s, d = h.shape
    return pl.pallas_call(
        _rkv_body,
        grid=(b, s // tm),
        in_specs=_shift_specs(s, d, tm) + [
            pl.BlockSpec((1, d), lambda bi, i: (0, 0)),
            pl.BlockSpec((3, 1, d), lambda bi, i: (0, 0, 0)),
        ] + [pl.BlockSpec((d, d), lambda bi, i: (0, 0))] * 3,
        out_specs=pl.BlockSpec((3, None, tm, d), lambda bi, i: (0, bi, i, 0)),
        out_shape=jax.ShapeDtypeStruct((3, b, s, d), BF16),
        compiler_params=_cparams(("parallel", "parallel")),
        name="rwkv_rkv",
    )(h, h, h, norm_g, mu3, w_r, w_k, w_v)


def _lora_body(x_ref, xp_ref, xn_ref, g_ref, mu_ref, w1_ref, w2_ref, w0_ref, a1_ref, a2_ref, a0_ref,
               g1_ref, g2_ref, lw_ref, alr_ref, gate_ref):
    tm = x_ref.shape[0]
    neg_rate = -math.exp(-0.5)

    def slab_steps(lo, hi):
        hn, xx = _norm_and_shift(x_ref, xp_ref, xn_ref, g_ref[...], pl.program_id(1), pl.num_programs(1), lo, hi)
        xw = (hn + xx * mu_ref[0]).astype(BF16)
        xa = (hn + xx * mu_ref[1]).astype(BF16)
        xg = (hn + xx * mu_ref[2]).astype(BF16)
        tw = [_dot(xw, w1_ref[dr]) for dr in range(2)]
        ta = [_dot(xa, a1_ref[dr]) for dr in range(2)]
        tg = _dot(xg, g1_ref[...])
        yield
        z = [_dot(jnp.tanh(tw[dr]).astype(BF16), w2_ref[dr]) for dr in range(2)]
        av = [_dot(ta[dr].astype(BF16), a2_ref[dr]) for dr in range(2)]
        gate = _dot(_sigmoid(tg).astype(BF16), g2_ref[...])
        yield
        for dr in range(2):
            lw_ref[dr, lo:hi, :] = neg_rate * _sigmoid(w0_ref[dr] + z[dr])
            alr_ref[dr, lo:hi, :] = _sigmoid(a0_ref[dr] + av[dr]).astype(alr_ref.dtype)
        gate_ref[lo:hi, :] = gate.astype(gate_ref.dtype)

    slab = min(tm, ROW_SLAB)
    _run_chains([slab_steps(lo, lo + slab) for lo in range(0, tm, slab)])


def _rwkv_lora(h, norm_g, mu3, w1, w2, w0, a1, a2, a0, g1, g2, *, tm):
    b, s, d = h.shape
    lw_, la_, lg_ = w1.shape[2], a1.shape[2], g1.shape[1]
    full = lambda *shape: pl.BlockSpec(shape, lambda bi, i: (0,) * len(shape))
    return pl.pallas_call(
        _lora_body,
        grid=(b, s // tm),
        in_specs=_shift_specs(s, d, tm) + [
            full(1, d), full(3, 1, d),
            full(2, d, lw_), full(2, lw_, d), full(2, 1, d),
            full(2, d, la_), full(2, la_, d), full(2, 1, d),
            full(d, lg_), full(lg_, d),
        ],
        out_specs=[
            pl.BlockSpec((2, None, tm, d), lambda bi, i: (0, bi, i, 0)),
            pl.BlockSpec((2, None, tm, d), lambda bi, i: (0, bi, i, 0)),
            pl.BlockSpec((None, tm, d), lambda bi, i: (bi, i, 0)),
        ],
        out_shape=[
            jax.ShapeDtypeStruct((2, b, s, d), F32),
            jax.ShapeDtypeStruct((2, b, s, d), BF16),
            jax.ShapeDtypeStruct((b, s, d), BF16),
        ],
        compiler_params=_cparams(("parallel", "parallel")),
        name="rwkv_lora",
    )(h, h, h, norm_g, mu3, w1, w2, w0, a1, a2, a0, g1, g2)


def _prefix_sum_rows(x, row_idx):
    shift = 1
    while shift < x.shape[0]:
        x = x + jnp.where(row_idx >= shift, pltpu.roll(x, shift, axis=0), 0.0)
        shift *= 2
    return x


def _head_groups(lanes, value):
    li = lax.broadcasted_iota(jnp.int32, (lanes, lanes), 0) // HEAD
    lj = lax.broadcasted_iota(jnp.int32, (lanes, lanes), 1) // HEAD
    return jnp.where(li == lj, value, 0.0).astype(BF16)


def _stack(x, lane_head):
    return jnp.concatenate([jnp.where(lane_head == 0, x, 0.0), jnp.where(lane_head == 1, x, 0.0)], axis=0)


def _keep_rows(x, ch, skip, reverse):
    if skip == 0:
        return x
    blocks = [x[h * ch:(h + 1) * ch - skip] if reverse else x[h * ch + skip:(h + 1) * ch] for h in range(x.shape[0] // ch)]
    return jnp.concatenate(blocks, axis=0)


def _restore_rows(y, ch, skip, reverse, base=None):
    if skip == 0:
        return y if base is None else base + y
    keep = ch - skip
    out = []
    for h in range(y.shape[0] // keep):
        part = y[h * keep:(h + 1) * keep]
        if base is None:
            fill = jnp.zeros((skip, y.shape[1]), y.dtype)
        else:
            blk = base[h * ch:(h + 1) * ch]
            fill = blk[keep:] if reverse else blk[:skip]
            part = part + (blk[:keep] if reverse else blk[skip:])
        out += [part, fill] if reverse else [fill, part]
    return jnp.concatenate(out, axis=0)


def _wkv_body(r_ref, k_ref, v_ref, lw_ref, alr_ref, kkp_ref, kap_ref, rkp_ref, y_ref, bonus_ref, s_ref, *, reverse):
    c = pl.program_id(1)
    nb, ch, d = r_ref.shape
    rows = 2 * ch
    n_pairs = d // PAIR

    @pl.when(c == 0)
    def _():
        s_ref[...] = jnp.zeros_like(s_ref)

    t_loc = lax.broadcasted_iota(jnp.int32, (ch, rows), 0)
    s_loc = lax.broadcasted_iota(jnp.int32, (ch, rows), 1) % ch
    before = (s_loc > t_loc) if reverse else (s_loc < t_loc)
    same = s_loc == t_loc
    incl = before | same
    eye = lax.broadcasted_iota(jnp.int32, (rows, rows), 0) == lax.broadcasted_iota(jnp.int32, (rows, rows), 1)
    tok_row = lax.broadcasted_iota(jnp.int32, (ch, PAIR), 0)
    lane_head = lax.broadcasted_iota(jnp.int32, (ch, PAIR), 1) // HEAD
    group = _head_groups(2 * PAIR, 1.0)
    bd = lambda x: _stack(x, lane_head).astype(BF16)

    def pair_steps(bb, p):
        sl = slice(p * PAIR, (p + 1) * PAIR)
        state = bb * n_pairs + p
        r, k, v = r_ref[bb, :, sl].astype(F32), k_ref[bb, :, sl].astype(F32), v_ref[bb, :, sl].astype(F32)
        lw, alr = lw_ref[bb, :, sl], alr_ref[bb, :, sl].astype(F32)
        run = _prefix_sum_rows(lw, tok_row)
        cl_end = run[ch - 1:ch]
        cl = cl_end - run + lw if reverse else run
        kk_raw = k * kkp_ref[:, sl]
        kd = k * (1.0 + (alr - 1.0) * kap_ref[:, sl])
        sums = _dot(jnp.concatenate([kk_raw * kk_raw, r * kd * rkp_ref[:, sl]], axis=1).astype(BF16), group)
        yield
        kk = kk_raw / jnp.maximum(jnp.sqrt(sums[:, :PAIR]), 1e-12)
        bv = kk * alr
        bonus_ref[bb, :, sl] = (sums[:, PAIR:] * v).astype(bonus_ref.dtype)

        e_neg = jnp.exp(-cl)
        e_end = jnp.exp(cl_end - cl)
        ar = jnp.concatenate([-kk * jnp.exp(cl - lw), r * jnp.exp(cl)], axis=0).astype(BF16)
        gram = _dot_nt(ar, jnp.concatenate([bd(bv * e_neg), bd(kd * e_neg)], axis=0))
        b_e = _stack(bv * e_end, lane_head).T.astype(BF16)
        k_e = _stack(kd * e_end, lane_head).T.astype(BF16)
        v_bd = bd(v)
        decay_end = jnp.where(eye, jnp.exp(cl_end), 0.0).astype(BF16)
        ars = _dot(jnp.concatenate([ar, decay_end], axis=0), s_ref[state].astype(BF16))
        yield
        n_pow = jnp.where(before, gram[:ch, :rows], 0.0)
        a_ak = jnp.where(before, gram[:ch, rows:], 0.0).astype(BF16)
        a_rb = jnp.where(incl, gram[ch:, :rows], 0.0).astype(BF16)
        a_rk = jnp.where(incl, gram[ch:, rows:], 0.0).astype(BF16)
        zyh = _dot(jnp.concatenate([a_ak, a_rk, k_e], axis=0), v_bd)

        inv = jnp.where(same, 1.0, 0.0) + n_pow
        for it in range(1, INV_STEPS + 1):
            power = 2 ** (it - 1)
            nw = bd(n_pow)
            skip_inv = min(ch, power // ROW_GROUP * ROW_GROUP)
            skip_n = min(ch, 2 * power // ROW_GROUP * ROW_GROUP)
            if it == 1:
                n_pow = _dot(n_pow.astype(BF16), nw)
            elif it < INV_STEPS:
                lhs = jnp.concatenate([_keep_rows(inv, ch, skip_inv, reverse), _keep_rows(n_pow, ch, skip_n, reverse)], axis=0)
                both = _dot(lhs.astype(BF16), nw)
                inv = _restore_rows(both[:ch - skip_inv], ch, skip_inv, reverse, base=inv)
                n_pow = _restore_rows(both[ch - skip_inv:], ch, skip_n, reverse)
            else:
                inv = _restore_rows(_dot(_keep_rows(inv, ch, skip_inv, reverse).astype(BF16), nw), ch, skip_inv, reverse, base=inv)
            yield
        u = _dot(inv.astype(BF16), bd(ars[:ch] + zyh[:ch]))
        yield
        fin = _dot(jnp.concatenate([a_rb, b_e], axis=0), bd(u))
        yield
        y_ref[bb, :, sl] = (ars[ch:rows] + fin[:ch] + zyh[ch:rows]).astype(y_ref.dtype)
        s_ref[state] = ars[rows:] + fin[ch:] + zyh[rows:]

    _run_chains([pair_steps(bb, p) for bb in range(nb) for p in range(n_pairs)], WKV_WAVE, WKV_SKEW)


def _wkv(rkv, lw, alr, kkp, kap, rkp, *, reverse):
    _, b, s, d = rkv.shape
    nc = s // CHUNK
    assert 2 * CHUNK == PAIR
    dr = int(reverse)
    tok = (lambda c: nc - 1 - c) if reverse else (lambda c: c)
    nb = math.gcd(b, WKV_SEQS)
    stacked = lambda j: pl.BlockSpec((None, nb, CHUNK, d), lambda bi, c: (j, bi, tok(c), 0))
    out_spec = pl.BlockSpec((nb, CHUNK, d), lambda bi, c: (bi, tok(c), 0))
    par_spec = pl.BlockSpec((1, d), lambda bi, c: (0, 0))
    return pl.pallas_call(
        functools.partial(_wkv_body, reverse=reverse),
        grid=(b // nb, nc),
        in_specs=[stacked(0), stacked(1), stacked(2), stacked(dr), stacked(dr), par_spec, par_spec, par_spec],
        out_specs=[out_spec, out_spec],
        out_shape=[jax.ShapeDtypeStruct((b, s, d), BF16), jax.ShapeDtypeStruct((b, s, d), BF16)],
        scratch_shapes=[pltpu.VMEM((nb * (d // PAIR), PAIR, PAIR), F32)],
        compiler_params=_cparams(("parallel", "arbitrary")),
        name="wkv_rev" if reverse else "wkv_fwd",
    )(rkv, rkv, rkv, lw, alr, kkp, kap, rkp)


def _out_body(x_ref, yf_ref, yr_ref, bf_ref, br_ref, gate_ref, lnw_ref, lnb_ref, wo_ref, o_ref):
    tm, d = x_ref.shape
    avg = _head_groups(GN_LANES, 1.0 / HEAD)

    def slab_steps(lo, hi):
        mixed = []
        for c0 in range(0, d, GN_LANES):
            sl = slice(c0, c0 + GN_LANES)
            y = yf_ref[lo:hi, sl].astype(F32) + yr_ref[lo:hi, sl].astype(F32)
            mean = _dot2(y, avg)
            dev = y - mean
            var = _dot((dev * dev).astype(BF16), avg)
            gn = dev * lax.rsqrt(var + GN_EPS) * lnw_ref[:, sl] + lnb_ref[:, sl]
            bonus = bf_ref[lo:hi, sl].astype(F32) + br_ref[lo:hi, sl].astype(F32)
            mixed.append(((gn + bonus) * gate_ref[lo:hi, sl].astype(F32)).astype(BF16))
            yield
        mixed = jnp.concatenate(mixed, axis=1)
        for c0 in range(0, d, GN_LANES):
            sl = slice(c0, c0 + GN_LANES)
            o_ref[lo:hi, sl] = x_ref[lo:hi, sl] + _dot(mixed, wo_ref[:, sl])
            yield

    slab = min(tm, ROW_SLAB)
    _run_chains([slab_steps(lo, lo + slab) for lo in range(0, tm, slab)], skew=d // GN_LANES)


def _rwkv_out(h, scans, gate, lnw, lnb, wo, *, tm):
    b, s, d = h.shape
    tok = pl.BlockSpec((None, tm, d), lambda bi, i: (bi, i, 0))
    par = pl.BlockSpec((1, d), lambda bi, i: (0, 0))
    return pl.pallas_call(
        _out_body,
        grid=(b, s // tm),
        in_specs=[tok] * 6 + [par, par, pl.BlockSpec((d, d), lambda bi, i: (0, 0))],
        out_specs=tok,
        out_shape=jax.ShapeDtypeStruct((b, s, d), F32),
        compiler_params=_cparams(("parallel", "parallel")),
        name="rwkv_out",
    )(h, *scans, gate, lnw, lnb, wo)


def _pad_lora(w_in, w_out):
    r = w_in.shape[-1]
    pad = (-r) % 128
    w_in = jnp.pad(w_in, [(0, 0)] * (w_in.ndim - 1) + [(0, pad)])
    w_out = jnp.pad(w_out, [(0, 0)] * (w_out.ndim - 2) + [(0, pad), (0, 0)])
    return w_in.astype(BF16), w_out.astype(BF16)


def _tile(n, want):
    t = min(n, want)
    while n % t:
        t //= 2
    return t


def _rwkv_mixer(h, norm_g, mu, w_r, w_k, w_v, w_o, w0, w1, w2, a0, a1, a2, g1, g2, k_k, k_a, r_k, ln_w, ln_b):
    b, s, d = h.shape
    row = lambda z: z.reshape(1, d)
    tm2 = _tile(s, 2 * ROW_SLAB)
    rkv = _rwkv_rkv(h, norm_g, mu[jnp.array([0, 2, 3])].reshape(3, 1, d), w_r, w_k, w_v, tm=tm2)
    w1p, w2p = _pad_lora(w1, w2)
    a1p, a2p = _pad_lora(a1, a2)
    lw, alr, gate = _rwkv_lora(h, norm_g, mu[jnp.array([1, 4, 5])].reshape(3, 1, d),
                               w1p, w2p, w0.reshape(2, 1, d), a1p, a2p, a0.reshape(2, 1, d),
                               g1.astype(BF16), g2.astype(BF16), tm=tm2)
    y_f, bonus_f = _wkv(rkv, lw, alr, row(k_k), row(k_a), row(r_k), reverse=False)
    y_r, bonus_r = _wkv(rkv, lw, alr, row(k_k), row(k_a), row(r_k), reverse=True)
    return _rwkv_out(h, (y_f, y_r, bonus_f, bonus_r), gate, row(ln_w), row(ln_b), w_o, tm=tm2)


def kernel(x, ffn1_norm, ffn1_gate, ffn1_up, ffn1_down, mix_norm, ffn2_norm, ffn2_gate, ffn2_up, ffn2_down, pool_w, pool_scale, rwkv_mu, rwkv_wr, rwkv_wk, rwkv_wv, rwkv_wo, rwkv_w0, rwkv_w1, rwkv_w2, rwkv_a0, rwkv_a1, rwkv_a2, rwkv_g1, rwkv_g2, rwkv_kk, rwkv_ka, rwkv_rk, rwkv_lnw, rwkv_lnb, final_norm):
    b, s, d = x.shape
    depth = ffn1_norm.shape[0]
    f = ffn1_gate.shape[2]
    row = lambda z: z.reshape(1, d)
    tm_ffn, tf = _tile(b * s, 1024), _tile(f, 512)
    ffn_params = ((ffn1_norm, ffn1_gate, ffn1_up, ffn1_down), (ffn2_norm, ffn2_gate, ffn2_up, ffn2_down))
    order = [(i, half) for i in range(depth) for half in (0, 1)]

    weights = [ffn_params[0][k][0].astype(BF16) for k in (1, 2, 3)]
    rwkv_w = None
    h = x
    for q, (i, half) in enumerate(order):
        j = i // 2
        if half == 1:
            if i % 2 == 0:
                h = _pool(h, row(mix_norm[i]), pool_w[j].astype(BF16), row(pool_scale[j]), ts=_tile(s, 1024))
            else:
                h = _rwkv_mixer(h, row(mix_norm[i]), rwkv_mu[j], *rwkv_w,
                                rwkv_w0[j], rwkv_w1[j], rwkv_w2[j], rwkv_a0[j], rwkv_a1[j], rwkv_a2[j],
                                rwkv_g1[j], rwkv_g2[j], rwkv_kk[j], rwkv_ka[j], rwkv_rk[j],
                                rwkv_lnw[j], rwkv_lnb[j])
        casts = []
        if q + 1 < len(order):
            ni, nhalf = order[q + 1]
            casts += [(ffn_params[nhalf][k], ni) for k in (1, 2, 3)]
        if half == 0 and i % 2 == 1:
            casts += [(w, j) for w in (rwkv_wr, rwkv_wk, rwkv_wv, rwkv_wo)]
        out, cast = _ffn(h.reshape(b * s, d), row(ffn_params[half][0][i]), *weights, row(final_norm), casts,
                         final_norm=q == len(order) - 1, tm=tm_ffn, tf=tf)
        h = out.reshape(b, s, d)
        if q + 1 < len(order):
            weights, cast = cast[:3], cast[3:]
        if cast:
            rwkv_w = cast
    return h
```

```python
import functools
import math

import jax
import jax.numpy as jnp
from jax import lax
from jax.experimental import pallas as pl
from jax.experimental.pallas import tpu as pltpu

F32 = jnp.float32
BF16 = jnp.bfloat16

V7X_LANES = 128
V7X_SUBLANES_F32 = 8
V7X_SUBLANES_BF16 = 16
V7X_MXU_DIM = 256
V7X_VMEM_BYTES = 64 * 1024 * 1024
VMEM_LIMIT = V7X_VMEM_BYTES - 4 * 1024 * 1024

RMS_EPS = 1e-6
GN_EPS = 64e-5
HEAD = 64
PAIR = 2 * HEAD
assert PAIR == V7X_LANES
CHUNK = 64
INV_STEPS = (CHUNK - 1).bit_length()
ROW_GROUP = V7X_SUBLANES_BF16
WKV_SEQS = 4
WKV_WAVE = 2
WKV_SKEW = 1
POOL_WINDOWS = (2, 4, 8, 16)
POOL_HALO = 32
POOL_LEVEL_PAD = V7X_SUBLANES_F32
assert max(POOL_WINDOWS) <= 2 * POOL_LEVEL_PAD <= POOL_HALO
SHIFT_HALO = V7X_SUBLANES_F32
GN_LANES = V7X_MXU_DIM
ROW_SLAB = V7X_MXU_DIM
FFN_DOWN_COLS = 512
_DONE = object()


def _cparams(sem):
    return pltpu.CompilerParams(dimension_semantics=sem, vmem_limit_bytes=VMEM_LIMIT)


def _rms(x, g):
    return x * lax.rsqrt(jnp.mean(x * x, axis=-1, keepdims=True) + RMS_EPS) * g


def _sigmoid(x):
    return 0.5 * jnp.tanh(0.5 * x) + 0.5


def _dot(a, b):
    return jnp.dot(a, b, preferred_element_type=F32)


def _dot_nt(a, b):
    return lax.dot_general(a, b, (((1,), (1,)), ((), ())), preferred_element_type=F32)


def _dot2(x, m_bf16):
    hi = x.astype(BF16)
    lo = (x - hi.astype(F32)).astype(BF16)
    return _dot(hi, m_bf16) + _dot(lo, m_bf16)


def _ffn_body(*refs, final_norm, n_cast):
    x_ref, g_ref, wg_ref, wu_ref, wd_ref, fg_ref = refs[:6]
    cast_src = refs[6:6 + n_cast]
    o_ref = refs[6 + n_cast]
    cast_dst = refs[7 + n_cast:7 + 2 * n_cast]
    n_ref = refs[7 + 2 * n_cast]
    f = pl.program_id(1)
    d = o_ref.shape[1]

    @pl.when(f == 0)
    def _():
        x = x_ref[...]
        n_ref[...] = _rms(x, g_ref[...]).astype(BF16)
        o_ref[...] = x

    n = n_ref[...]
    gate = _dot(n, wg_ref[...])
    up = _dot(n, wu_ref[...])
    act = (0.5 * gate * jax.nn.sigmoid(gate) * up).astype(BF16)
    for c0 in range(0, d, FFN_DOWN_COLS):
        cols = slice(c0, c0 + FFN_DOWN_COLS)
        o_ref[:, cols] += _dot(act, wd_ref[:, cols])

    for src, dst in zip(cast_src, cast_dst):
        dst[...] = src[...].astype(dst.dtype)

    if final_norm:
        @pl.when(f == pl.num_programs(1) - 1)
        def _():
            o_ref[...] = _rms(o_ref[...], fg_ref[...])


def _cast_specs(stacked, layer, ni, nj):
    _, rows, cols = stacked.shape
    ok = lambda n, parts, unit: n % parts == 0 and (n // parts) % unit == 0
    row_ok = lambda parts: ok(rows, parts, V7X_SUBLANES_BF16)
    col_ok = lambda parts: ok(cols, parts, V7X_LANES)
    if row_ok(ni) and col_ok(nj):
        blk, imap = (rows // ni, cols // nj), lambda i, j: (i, j)
    elif row_ok(nj) and col_ok(ni):
        blk, imap = (rows // nj, cols // ni), lambda i, j: (j, i)
    else:
        assert row_ok(ni), (rows, ni)
        nc = 1
        while nc * 2 <= nj and col_ok(nc * 2):
            nc *= 2
        blk, imap = (rows // ni, cols // nc), lambda i, j: (i, jnp.minimum(j, nc - 1))
    in_spec = pl.BlockSpec((None,) + blk, lambda i, j: (layer,) + imap(i, j))
    return in_spec, pl.BlockSpec(blk, imap), jax.ShapeDtypeStruct((rows, cols), BF16)


def _ffn(h, norm_g, wg, wu, wd, final_g, casts, *, final_norm, tm, tf):
    t, d = h.shape
    f = wg.shape[1]
    ni, nj = t // tm, f // tf
    specs = [_cast_specs(w, layer, ni, nj) for w, layer in casts]
    outs = pl.pallas_call(
        functools.partial(_ffn_body, final_norm=final_norm, n_cast=len(casts)),
        grid=(ni, nj),
        in_specs=[
            pl.BlockSpec((tm, d), lambda i, j: (i, 0)),
            pl.BlockSpec((1, d), lambda i, j: (0, 0)),
            pl.BlockSpec((d, tf), lambda i, j: (0, j)),
            pl.BlockSpec((d, tf), lambda i, j: (0, j)),
            pl.BlockSpec((tf, d), lambda i, j: (j, 0)),
            pl.BlockSpec((1, d), lambda i, j: (0, 0)),
        ] + [sp[0] for sp in specs],
        out_specs=[pl.BlockSpec((tm, d), lambda i, j: (i, 0))] + [sp[1] for sp in specs],
        out_shape=[jax.ShapeDtypeStruct((t, d), F32)] + [sp[2] for sp in specs],
        scratch_shapes=[pltpu.VMEM((tm, d), BF16)],
        compiler_params=_cparams(("parallel", "arbitrary")),
        name="ffn",
    )(h, norm_g, wg, wu, wd, final_g, *[w for w, _ in casts])
    return outs[0], list(outs[1:])


def _pool_body(x_ref, xp_ref, xn_ref, g_ref, w_ref, sc_ref, o_ref, ext_ref, pa_ref, pb_ref, *, seq):
    i = pl.program_id(1)
    ts = x_ref.shape[0]
    cg = w_ref.shape[1]
    g = g_ref[...]
    first = i == 0
    last = i == pl.num_programs(1) - 1
    ext_ref[0:POOL_HALO, :] = jnp.where(first, 0.0, _rms(xp_ref[...], g))
    ext_ref[POOL_HALO:POOL_HALO + ts, :] = _rms(x_ref[...], g)
    ext_ref[POOL_HALO + ts:POOL_HALO + ts + POOL_HALO, :] = jnp.where(last, 0.0, _rms(xn_ref[...], g))

    t_glob = i * ts + lax.broadcasted_iota(jnp.int32, (ts, 1), 0)
    for gi, w in enumerate(POOL_WINDOWS):
        lanes = slice(gi * cg, (gi + 1) * cg)
        levels = w.bit_length() - 1
        src, lane_sel, start = ext_ref, lanes, POOL_HALO - w // 2
        for j in range(1, levels + 1):
            n = ts + POOL_LEVEL_PAD * (levels - j)
            step = 2 ** (j - 1)
            acc = src[start:start + n, lane_sel] + src[start + step:start + step + n, lane_sel]
            if j < levels:
                dst = (pa_ref, pb_ref)[j % 2]
                dst[0:n, :] = acc
                src, lane_sel, start = dst, slice(None), 0
        lo = jnp.clip(t_glob - w // 2, 0, seq)
        hi = jnp.clip(t_glob - w // 2 + w, 0, seq)
        inv_cnt = 1.0 / (hi - lo).astype(F32)
        pooled = acc * inv_cnt - ext_ref[POOL_HALO:POOL_HALO + ts, lanes]
        y = _dot(pooled.astype(BF16), w_ref[gi])
        o_ref[:, lanes] = x_ref[:, lanes] + y * sc_ref[:, lanes]


def _pool(h, norm_g, w_grp, scale, *, ts):
    b, s, d = h.shape
    ng, cg, _ = w_grp.shape
    nb = ts // POOL_HALO
    last_blk = s // POOL_HALO - 1
    return pl.pallas_call(
        functools.partial(_pool_body, seq=s),
        grid=(b, s // ts),
        in_specs=[
            pl.BlockSpec((None, ts, d), lambda bi, i: (bi, i, 0)),
            pl.BlockSpec((None, POOL_HALO, d), lambda bi, i: (bi, jnp.maximum(i * nb - 1, 0), 0)),
            pl.BlockSpec((None, POOL_HALO, d), lambda bi, i: (bi, jnp.minimum((i + 1) * nb, last_blk), 0)),
            pl.BlockSpec((1, d), lambda bi, i: (0, 0)),
            pl.BlockSpec((ng, cg, cg), lambda bi, i: (0, 0, 0)),
            pl.BlockSpec((1, d), lambda bi, i: (0, 0)),
        ],
        out_specs=pl.BlockSpec((None, ts, d), lambda bi, i: (bi, i, 0)),
        out_shape=jax.ShapeDtypeStruct((b, s, d), F32),
        scratch_shapes=[pltpu.VMEM((ts + 2 * POOL_HALO, d), F32)]
        + [pltpu.VMEM((ts + POOL_LEVEL_PAD * (max(POOL_WINDOWS).bit_length() - 2), cg), F32)] * 2,
        compiler_params=_cparams(("parallel", "parallel")),
        name="pool",
    )(h, h, h, norm_g, w_grp, scale)


def _norm_and_shift(x_ref, xp_ref, xn_ref, g, i, n_tiles, lo=0, hi=None):
    tm = x_ref.shape[0]
    hi = tm if hi is None else hi
    n = hi - lo
    hn = _rms(x_ref[lo:hi, :], g)
    if lo == 0:
        prev_row = jnp.where(i == 0, 0.0, _rms(xp_ref[...], g)[SHIFT_HALO - 1:SHIFT_HALO])
    else:
        prev_row = _rms(x_ref[lo - SHIFT_HALO:lo, :], g)[SHIFT_HALO - 1:SHIFT_HALO]
    if hi == tm:
        next_row = jnp.where(i == n_tiles - 1, 0.0, _rms(xn_ref[...], g)[0:1])
    else:
        next_row = _rms(x_ref[hi:hi + SHIFT_HALO, :], g)[0:1]
    row = lax.broadcasted_iota(jnp.int32, (n, 1), 0)
    prev = jnp.where(row == 0, prev_row, pltpu.roll(hn, 1, axis=0))
    nxt = jnp.where(row == n - 1, next_row, pltpu.roll(hn, n - 1, axis=0))
    return hn, 0.5 * (prev + nxt) - hn


def _run_chains(chains, wave=1, skew=1):
    live = list(range(len(chains)))
    tick = 0
    while live:
        live = [i for i in live if (i // wave) * skew > tick or next(chains[i], _DONE) is not _DONE]
        tick += 1


def _shift_specs(s, d, tm):
    nb = tm // SHIFT_HALO
    last_blk = s // SHIFT_HALO - 1
    return [
        pl.BlockSpec((None, tm, d), lambda bi, i, *_: (bi, i, 0)),
        pl.BlockSpec((None, SHIFT_HALO, d), lambda bi, i, *_: (bi, jnp.maximum(i * nb - 1, 0), 0)),
        pl.BlockSpec((None, SHIFT_HALO, d), lambda bi, i, *_: (bi, jnp.minimum((i + 1) * nb, last_blk), 0)),
    ]


def _rkv_body(x_ref, xp_ref, xn_ref, g_ref, mu_ref, wr_ref, wk_ref, wv_ref, o_ref):
    tm = x_ref.shape[0]

    def slab_steps(lo, hi):
        hn, xx = _norm_and_shift(x_ref, xp_ref, xn_ref, g_ref[...], pl.program_id(1), pl.num_programs(1), lo, hi)
        yield
        for j, w_ref in enumerate((wr_ref, wk_ref, wv_ref)):
            xm = (hn + xx * mu_ref[j]).astype(BF16)
            o_ref[j, lo:hi, :] = _dot(xm, w_ref[...]).astype(o_ref.dtype)
            yield

    slab = min(tm, ROW_SLAB)
    _run_chains([slab_steps(lo, lo + slab) for lo in range(0, tm, slab)])


def _rwkv_rkv(h, norm_g, mu3, w_r, w_k, w_v, *, tm):
    b, s, d = h.shape
    return pl.pallas_call(
        _rkv_body,
        grid=(b, s // tm),
        in_specs=_shift_specs(s, d, tm) + [
            pl.BlockSpec((1, d), lambda bi, i: (0, 0)),
            pl.BlockSpec((3, 1, d), lambda bi, i: (0, 0, 0)),
        ] + [pl.BlockSpec((d, d), lambda bi, i: (0, 0))] * 3,
        out_specs=pl.BlockSpec((3, None, tm, d), lambda bi, i: (0, bi, i, 0)),
        out_shape=jax.ShapeDtypeStruct((3, b, s, d), BF16),
        compiler_params=_cparams(("parallel", "parallel")),
        name="rwkv_rkv",
    )(h, h, h, norm_g, mu3, w_r, w_k, w_v)


def _lora_body(x_ref, xp_ref, xn_ref, g_ref, mu_ref, w1_ref, w2_ref, w0_ref, a1_ref, a2_ref, a0_ref,
               g1_ref, g2_ref, lw_ref, alr_ref, gate_ref):
    tm = x_ref.shape[0]
    neg_rate = -math.exp(-0.5)

    def slab_steps(lo, hi):
        hn, xx = _norm_and_shift(x_ref, xp_ref, xn_ref, g_ref[...], pl.program_id(1), pl.num_programs(1), lo, hi)
        xw = (hn + xx * mu_ref[0]).astype(BF16)
        xa = (hn + xx * mu_ref[1]).astype(BF16)
        xg = (hn + xx * mu_ref[2]).astype(BF16)
        tw = [_dot(xw, w1_ref[dr]) for dr in range(2)]
        ta = [_dot(xa, a1_ref[dr]) for dr in range(2)]
        tg = _dot(xg, g1_ref[...])
        yield
        z = [_dot(jnp.tanh(tw[dr]).astype(BF16), w2_ref[dr]) for dr in range(2)]
        av = [_dot(ta[dr].astype(BF16), a2_ref[dr]) for dr in range(2)]
        gate = _dot(_sigmoid(tg).astype(BF16), g2_ref[...])
        yield
        for dr in range(2):
            lw_ref[dr, lo:hi, :] = neg_rate * _sigmoid(w0_ref[dr] + z[dr])
            alr_ref[dr, lo:hi, :] = _sigmoid(a0_ref[dr] + av[dr]).astype(alr_ref.dtype)
        gate_ref[lo:hi, :] = gate.astype(gate_ref.dtype)

    slab = min(tm, ROW_SLAB)
    _run_chains([slab_steps(lo, lo + slab) for lo in range(0, tm, slab)])


def _rwkv_lora(h, norm_g, mu3, w1, w2, w0, a1, a2, a0, g1, g2, *, tm):
    b, s, d = h.shape
    lw_, la_, lg_ = w1.shape[2], a1.shape[2], g1.shape[1]
    full = lambda *shape: pl.BlockSpec(shape, lambda bi, i: (0,) * len(shape))
    return pl.pallas_call(
        _lora_body,
        grid=(b, s // tm),
        in_specs=_shift_specs(s, d, tm) + [
            full(1, d), full(3, 1, d),
            full(2, d, lw_), full(2, lw_, d), full(2, 1, d),
            full(2, d, la_), full(2, la_, d), full(2, 1, d),
            full(d, lg_), full(lg_, d),
        ],
        out_specs=[
            pl.BlockSpec((2, None, tm, d), lambda bi, i: (0, bi, i, 0)),
            pl.BlockSpec((2, None, tm, d), lambda bi, i: (0, bi, i, 0)),
            pl.BlockSpec((None, tm, d), lambda bi, i: (bi, i, 0)),
        ],
        out_shape=[
            jax.ShapeDtypeStruct((2, b, s, d), F32),
            jax.ShapeDtypeStruct((2, b, s, d), BF16),
            jax.ShapeDtypeStruct((b, s, d), BF16),
        ],
        compiler_params=_cparams(("parallel", "parallel")),
        name="rwkv_lora",
    )(h, h, h, norm_g, mu3, w1, w2, w0, a1, a2, a0, g1, g2)


def _prefix_sum_rows(x, row_idx):
    shift = 1
    while shift < x.shape[0]:
        x = x + jnp.where(row_idx >= shift, pltpu.roll(x, shift, axis=0), 0.0)
        shift *= 2
    return x


def _head_groups(lanes, value):
    li = lax.broadcasted_iota(jnp.int32, (lanes, lanes), 0) // HEAD
    lj = lax.broadcasted_iota(jnp.int32, (lanes, lanes), 1) // HEAD
    return jnp.where(li == lj, value, 0.0).astype(BF16)


def _stack(x, lane_head):
    return jnp.concatenate([jnp.where(lane_head == 0, x, 0.0), jnp.where(lane_head == 1, x, 0.0)], axis=0)


def _keep_rows(x, ch, skip, reverse):
    if skip == 0:
        return x
    blocks = [x[h * ch:(h + 1) * ch - skip] if reverse else x[h * ch + skip:(h + 1) * ch] for h in range(x.shape[0] // ch)]
    return jnp.concatenate(blocks, axis=0)


def _restore_rows(y, ch, skip, reverse, base=None):
    if skip == 0:
        return y if base is None else base + y
    keep = ch - skip
    out = []
    for h in range(y.shape[0] // keep):
        part = y[h * keep:(h + 1) * keep]
        if base is None:
            fill = jnp.zeros((skip, y.shape[1]), y.dtype)
        else:
            blk = base[h * ch:(h + 1) * ch]
            fill = blk[keep:] if reverse else blk[:skip]
            part = part + (blk[:keep] if reverse else blk[skip:])
        out += [part, fill] if reverse else [fill, part]
    return jnp.concatenate(out, axis=0)


def _wkv_body(r_ref, k_ref, v_ref, lw_ref, alr_ref, kkp_ref, kap_ref, rkp_ref, y_ref, bonus_ref, s_ref, *, reverse):
    c = pl.program_id(1)
    nb, ch, d = r_ref.shape
    rows = 2 * ch
    n_pairs = d // PAIR

    @pl.when(c == 0)
    def _():
        s_ref[...] = jnp.zeros_like(s_ref)

    t_loc = lax.broadcasted_iota(jnp.int32, (ch, rows), 0)
    s_loc = lax.broadcasted_iota(jnp.int32, (ch, rows), 1) % ch
    before = (s_loc > t_loc) if reverse else (s_loc < t_loc)
    same = s_loc == t_loc
    incl = before | same
    eye = lax.broadcasted_iota(jnp.int32, (rows, rows), 0) == lax.broadcasted_iota(jnp.int32, (rows, rows), 1)
    tok_row = lax.broadcasted_iota(jnp.int32, (ch, PAIR), 0)
    lane_head = lax.broadcasted_iota(jnp.int32, (ch, PAIR), 1) // HEAD
    group = _head_groups(2 * PAIR, 1.0)
    bd = lambda x: _stack(x, lane_head).astype(BF16)

    def pair_steps(bb, p):
        sl = slice(p * PAIR, (p + 1) * PAIR)
        state = bb * n_pairs + p
        r, k, v = r_ref[bb, :, sl].astype(F32), k_ref[bb, :, sl].astype(F32), v_ref[bb, :, sl].astype(F32)
        lw, alr = lw_ref[bb, :, sl], alr_ref[bb, :, sl].astype(F32)
        run = _prefix_sum_rows(lw, tok_row)
        cl_end = run[ch - 1:ch]
        cl = cl_end - run + lw if reverse else run
        kk_raw = k * kkp_ref[:, sl]
        kd = k * (1.0 + (alr - 1.0) * kap_ref[:, sl])
        sums = _dot(jnp.concatenate([kk_raw * kk_raw, r * kd * rkp_ref[:, sl]], axis=1).astype(BF16), group)
        yield
        kk = kk_raw / jnp.maximum(jnp.sqrt(sums[:, :PAIR]), 1e-12)
        bv = kk * alr
        bonus_ref[bb, :, sl] = (sums[:, PAIR:] * v).astype(bonus_ref.dtype)

        e_neg = jnp.exp(-cl)
        e_end = jnp.exp(cl_end - cl)
        ar = jnp.concatenate([-kk * jnp.exp(cl - lw), r * jnp.exp(cl)], axis=0).astype(BF16)
        gram = _dot_nt(ar, jnp.concatenate([bd(bv * e_neg), bd(kd * e_neg)], axis=0))
        b_e = _stack(bv * e_end, lane_head).T.astype(BF16)
        k_e = _stack(kd * e_end, lane_head).T.astype(BF16)
        v_bd = bd(v)
        decay_end = jnp.where(eye, jnp.exp(cl_end), 0.0).astype(BF16)
        ars = _dot(jnp.concatenate([ar, decay_end], axis=0), s_ref[state].astype(BF16))
        yield
        n_pow = jnp.where(before, gram[:ch, :rows], 0.0)
        a_ak = jnp.where(before, gram[:ch, rows:], 0.0).astype(BF16)
        a_rb = jnp.where(incl, gram[ch:, :rows], 0.0).astype(BF16)
        a_rk = jnp.where(incl, gram[ch:, rows:], 0.0).astype(BF16)
        zyh = _dot(jnp.concatenate([a_ak, a_rk, k_e], axis=0), v_bd)

        inv = jnp.where(same, 1.0, 0.0) + n_pow
        for it in range(1, INV_STEPS + 1):
            power = 2 ** (it - 1)
            nw = bd(n_pow)
            skip_inv = min(ch, power // ROW_GROUP * ROW_GROUP)
            skip_n = min(ch, 2 * power // ROW_GROUP * ROW_GROUP)
            if it == 1:
                n_pow = _dot(n_pow.astype(BF16), nw)
            elif it < INV_STEPS:
                lhs = jnp.concatenate([_keep_rows(inv, ch, skip_inv, reverse), _keep_rows(n_pow, ch, skip_n, reverse)], axis=0)
                both = _dot(lhs.astype(BF16), nw)
                inv = _restore_rows(both[:ch - skip_inv], ch, skip_inv, reverse, base=inv)
                n_pow = _restore_rows(both[ch - skip_inv:], ch, skip_n, reverse)
            else:
                inv = _restore_rows(_dot(_keep_rows(inv, ch, skip_inv, reverse).astype(BF16), nw), ch, skip_inv, reverse, base=inv)
            yield
        u = _dot(inv.astype(BF16), bd(ars[:ch] + zyh[:ch]))
        yield
        fin = _dot(jnp.concatenate([a_rb, b_e], axis=0), bd(u))
        yield
        y_ref[bb, :, sl] = (ars[ch:rows] + fin[:ch] + zyh[ch:rows]).astype(y_ref.dtype)
        s_ref[state] = ars[rows:] + fin[ch:] + zyh[rows:]

    _run_chains([pair_steps(bb, p) for bb in range(nb) for p in range(n_pairs)], WKV_WAVE, WKV_SKEW)


def _wkv(rkv, lw, alr, kkp, kap, rkp, *, reverse):
    _, b, s, d = rkv.shape
    nc = s // CHUNK
    assert 2 * CHUNK == PAIR
    dr = int(reverse)
    tok = (lambda c: nc - 1 - c) if reverse else (lambda c: c)
    nb = math.gcd(b, WKV_SEQS)
    stacked = lambda j: pl.BlockSpec((None, nb, CHUNK, d), lambda bi, c: (j, bi, tok(c), 0))
    out_spec = pl.BlockSpec((nb, CHUNK, d), lambda bi, c: (bi, tok(c), 0))
    par_spec = pl.BlockSpec((1, d), lambda bi, c: (0, 0))
    return pl.pallas_call(
        functools.partial(_wkv_body, reverse=reverse),
        grid=(b // nb, nc),
        in_specs=[stacked(0), stacked(1), stacked(2), stacked(dr), stacked(dr), par_spec, par_spec, par_spec],
        out_specs=[out_spec, out_spec],
        out_shape=[jax.ShapeDtypeStruct((b, s, d), BF16), jax.ShapeDtypeStruct((b, s, d), BF16)],
        scratch_shapes=[pltpu.VMEM((nb * (d // PAIR), PAIR, PAIR), F32)],
        compiler_params=_cparams(("parallel", "arbitrary")),
        name="wkv_rev" if reverse else "wkv_fwd",
    )(rkv, rkv, rkv, lw, alr, kkp, kap, rkp)


def _out_body(x_ref, yf_ref, yr_ref, bf_ref, br_ref, gate_ref, lnw_ref, lnb_ref, wo_ref, o_ref):
    tm, d = x_ref.shape
    avg = _head_groups(GN_LANES, 1.0 / HEAD)

    def slab_steps(lo, hi):
        mixed = []
        for c0 in range(0, d, GN_LANES):
            sl = slice(c0, c0 + GN_LANES)
            y = yf_ref[lo:hi, sl].astype(F32) + yr_ref[lo:hi, sl].astype(F32)
            mean = _dot2(y, avg)
            dev = y - mean
            var = _dot((dev * dev).astype(BF16), avg)
            gn = dev * lax.rsqrt(var + GN_EPS) * lnw_ref[:, sl] + lnb_ref[:, sl]
            bonus = bf_ref[lo:hi, sl].astype(F32) + br_ref[lo:hi, sl].astype(F32)
            mixed.append(((gn + bonus) * gate_ref[lo:hi, sl].astype(F32)).astype(BF16))
            yield
        mixed = jnp.concatenate(mixed, axis=1)
        for c0 in range(0, d, GN_LANES):
            sl = slice(c0, c0 + GN_LANES)
            o_ref[lo:hi, sl] = x_ref[lo:hi, sl] + _dot(mixed, wo_ref[:, sl])
            yield

    slab = min(tm, ROW_SLAB)
    _run_chains([slab_steps(lo, lo + slab) for lo in range(0, tm, slab)], skew=d // GN_LANES)


def _rwkv_out(h, scans, gate, lnw, lnb, wo, *, tm):
    b, s, d = h.shape
    tok = pl.BlockSpec((None, tm, d), lambda bi, i: (bi, i, 0))
    par = pl.BlockSpec((1, d), lambda bi, i: (0, 0))
    return pl.pallas_call(
        _out_body,
        grid=(b, s // tm),
        in_specs=[tok] * 6 + [par, par, pl.BlockSpec((d, d), lambda bi, i: (0, 0))],
        out_specs=tok,
        out_shape=jax.ShapeDtypeStruct((b, s, d), F32),
        compiler_params=_cparams(("parallel", "parallel")),
        name="rwkv_out",
    )(h, *scans, gate, lnw, lnb, wo)


def _pad_lora(w_in, w_out):
    r = w_in.shape[-1]
    pad = (-r) % V7X_LANES
    w_in = jnp.pad(w_in, [(0, 0)] * (w_in.ndim - 1) + [(0, pad)])
    w_out = jnp.pad(w_out, [(0, 0)] * (w_out.ndim - 2) + [(0, pad), (0, 0)])
    return w_in.astype(BF16), w_out.astype(BF16)


def _tile(n, want):
    t = min(n, want)
    while n % t:
        t //= 2
    return t


def _tiles(b, s, f):
    return dict(
        ffn_rows=_tile(b * s, 4 * V7X_MXU_DIM),
        ffn_cols=_tile(f, 2 * V7X_MXU_DIM),
        pool_rows=_tile(s, 4 * V7X_MXU_DIM),
        rwkv_rows=_tile(s, 2 * ROW_SLAB),
    )


def _rwkv_mixer(h, norm_g, mu, w_r, w_k, w_v, w_o, w0, w1, w2, a0, a1, a2, g1, g2, k_k, k_a, r_k, ln_w, ln_b, *, tm2):
    b, s, d = h.shape
    row = lambda z: z.reshape(1, d)
    rkv = _rwkv_rkv(h, norm_g, mu[jnp.array([0, 2, 3])].reshape(3, 1, d), w_r, w_k, w_v, tm=tm2)
    w1p, w2p = _pad_lora(w1, w2)
    a1p, a2p = _pad_lora(a1, a2)
    lw, alr, gate = _rwkv_lora(h, norm_g, mu[jnp.array([1, 4, 5])].reshape(3, 1, d),
                               w1p, w2p, w0.reshape(2, 1, d), a1p, a2p, a0.reshape(2, 1, d),
                               g1.astype(BF16), g2.astype(BF16), tm=tm2)
    y_f, bonus_f = _wkv(rkv, lw, alr, row(k_k), row(k_a), row(r_k), reverse=False)
    y_r, bonus_r = _wkv(rkv, lw, alr, row(k_k), row(k_a), row(r_k), reverse=True)
    return _rwkv_out(h, (y_f, y_r, bonus_f, bonus_r), gate, row(ln_w), row(ln_b), w_o, tm=tm2)


def kernel(x, ffn1_norm, ffn1_gate, ffn1_up, ffn1_down, mix_norm, ffn2_norm, ffn2_gate, ffn2_up, ffn2_down, pool_w, pool_scale, rwkv_mu, rwkv_wr, rwkv_wk, rwkv_wv, rwkv_wo, rwkv_w0, rwkv_w1, rwkv_w2, rwkv_a0, rwkv_a1, rwkv_a2, rwkv_g1, rwkv_g2, rwkv_kk, rwkv_ka, rwkv_rk, rwkv_lnw, rwkv_lnb, final_norm):
    b, s, d = x.shape
    depth = ffn1_norm.shape[0]
    f = ffn1_gate.shape[2]
    row = lambda z: z.reshape(1, d)
    tiles = _tiles(b, s, f)
    ffn_params = ((ffn1_norm, ffn1_gate, ffn1_up, ffn1_down), (ffn2_norm, ffn2_gate, ffn2_up, ffn2_down))
    order = [(i, half) for i in range(depth) for half in (0, 1)]

    weights = [ffn_params[0][k][0].astype(BF16) for k in (1, 2, 3)]
    rwkv_w = None
    h = x
    for q, (i, half) in enumerate(order):
        j = i // 2
        if half == 1:
            if i % 2 == 0:
                h = _pool(h, row(mix_norm[i]), pool_w[j].astype(BF16), row(pool_scale[j]), ts=tiles["pool_rows"])
            else:
                h = _rwkv_mixer(h, row(mix_norm[i]), rwkv_mu[j], *rwkv_w,
                                rwkv_w0[j], rwkv_w1[j], rwkv_w2[j], rwkv_a0[j], rwkv_a1[j], rwkv_a2[j],
                                rwkv_g1[j], rwkv_g2[j], rwkv_kk[j], rwkv_ka[j], rwkv_rk[j],
                                rwkv_lnw[j], rwkv_lnb[j], tm2=tiles["rwkv_rows"])
        casts = []
        if q + 1 < len(order):
            ni, nhalf = order[q + 1]
            casts += [(ffn_params[nhalf][k], ni) for k in (1, 2, 3)]
        if half == 0 and i % 2 == 1:
            casts += [(w, j) for w in (rwkv_wr, rwkv_wk, rwkv_wv, rwkv_wo)]
        out, cast = _ffn(h.reshape(b * s, d), row(ffn_params[half][0][i]), *weights, row(final_norm), casts,
                         final_norm=q == len(order) - 1, tm=tiles["ffn_rows"], tf=tiles["ffn_cols"])
        h = out.reshape(b, s, d)
        if q + 1 < len(order):
            weights, cast = cast[:3], cast[3:]
        if cast:
            rwkv_w = cast
    return h
```

```python
import functools
import math

import jax
import jax.numpy as jnp
from jax import lax
from jax.experimental import pallas as pl
from jax.experimental.pallas import tpu as pltpu

F32 = jnp.float32
BF16 = jnp.bfloat16

V7X_LANES = 128
V7X_SUBLANES_F32 = 8
V7X_SUBLANES_BF16 = 16
V7X_MXU_DIM = 256
V7X_VMEM_BYTES = 64 * 1024 * 1024
VMEM_LIMIT = V7X_VMEM_BYTES - 4 * 1024 * 1024

RMS_EPS = 1e-6
GN_EPS = 64e-5
HEAD = 64
PAIR = 2 * HEAD
assert PAIR == V7X_LANES
CHUNK = 64
INV_STEPS = (CHUNK - 1).bit_length()
ROW_GROUP = V7X_SUBLANES_BF16
WKV_SEQS = 4
WKV_WAVE = 2
WKV_SKEW = 1
POOL_WINDOWS = (2, 4, 8, 16)
POOL_HALO = 32
POOL_LEVEL_PAD = V7X_SUBLANES_F32
assert max(POOL_WINDOWS) <= 2 * POOL_LEVEL_PAD <= POOL_HALO
SHIFT_HALO = V7X_SUBLANES_F32
GN_LANES = V7X_MXU_DIM
ROW_SLAB = V7X_MXU_DIM
FFN_DOWN_COLS = 512
_DONE = object()


def _cparams(sem):
    return pltpu.CompilerParams(dimension_semantics=sem, vmem_limit_bytes=VMEM_LIMIT)


def _rms(x, g):
    return x * lax.rsqrt(jnp.mean(x * x, axis=-1, keepdims=True) + RMS_EPS) * g


def _sigmoid(x):
    return 0.5 * jnp.tanh(0.5 * x) + 0.5


def _dot(a, b):
    return jnp.dot(a, b, preferred_element_type=F32)


def _dot_nt(a, b):
    return lax.dot_general(a, b, (((1,), (1,)), ((), ())), preferred_element_type=F32)


def _dot2(x, m_bf16):
    hi = x.astype(BF16)
    lo = (x - hi.astype(F32)).astype(BF16)
    return _dot(hi, m_bf16) + _dot(lo, m_bf16)


def _ffn_body(*refs, final_norm, n_cast):
    x_ref, g_ref, wg_ref, wu_ref, wd_ref, fg_ref = refs[:6]
    cast_src = refs[6:6 + n_cast]
    o_ref = refs[6 + n_cast]
    cast_dst = refs[7 + n_cast:7 + 2 * n_cast]
    n_ref = refs[7 + 2 * n_cast]
    f = pl.program_id(1)
    d = o_ref.shape[1]

    @pl.when(f == 0)
    def _():
        x = x_ref[...]
        n_ref[...] = _rms(x, g_ref[...]).astype(BF16)
        o_ref[...] = x

    n = n_ref[...]
    gate = _dot(n, wg_ref[...])
    up = _dot(n, wu_ref[...])
    act = (0.5 * gate * jax.nn.sigmoid(gate) * up).astype(BF16)
    for c0 in range(0, d, FFN_DOWN_COLS):
        cols = slice(c0, c0 + FFN_DOWN_COLS)
        o_ref[:, cols] += _dot(act, wd_ref[:, cols])

    for src, dst in zip(cast_src, cast_dst):
        dst[...] = src[...].astype(dst.dtype)

    if final_norm:
        @pl.when(f == pl.num_programs(1) - 1)
        def _():
            o_ref[...] = _rms(o_ref[...], fg_ref[...])


def _cast_specs(stacked, layer, ni, nj):
    _, rows, cols = stacked.shape
    ok = lambda n, parts, unit: n % parts == 0 and (n // parts) % unit == 0
    row_ok = lambda parts: ok(rows, parts, V7X_SUBLANES_BF16)
    col_ok = lambda parts: ok(cols, parts, V7X_LANES)
    if row_ok(ni) and col_ok(nj):
        blk, imap = (rows // ni, cols // nj), lambda i, j: (i, j)
    elif row_ok(nj) and col_ok(ni):
        blk, imap = (rows // nj, cols // ni), lambda i, j: (j, i)
    else:
        assert row_ok(ni), (rows, ni)
        nc = 1
        while nc * 2 <= nj and col_ok(nc * 2):
            nc *= 2
        blk, imap = (rows // ni, cols // nc), lambda i, j: (i, jnp.minimum(j, nc - 1))
    in_spec = pl.BlockSpec((None,) + blk, lambda i, j: (layer,) + imap(i, j))
    return in_spec, pl.BlockSpec(blk, imap), jax.ShapeDtypeStruct((rows, cols), BF16)


def _ffn(h, norm_g, wg, wu, wd, final_g, casts, *, final_norm, tm, tf):
    t, d = h.shape
    f = wg.shape[1]
    ni, nj = t // tm, f // tf
    specs = [_cast_specs(w, layer, ni, nj) for w, layer in casts]
    outs = pl.pallas_call(
        functools.partial(_ffn_body, final_norm=final_norm, n_cast=len(casts)),
        grid=(ni, nj),
        in_specs=[
            pl.BlockSpec((tm, d), lambda i, j: (i, 0)),
            pl.BlockSpec((1, d), lambda i, j: (0, 0)),
            pl.BlockSpec((d, tf), lambda i, j: (0, j)),
            pl.BlockSpec((d, tf), lambda i, j: (0, j)),
            pl.BlockSpec((tf, d), lambda i, j: (j, 0)),
            pl.BlockSpec((1, d), lambda i, j: (0, 0)),
        ] + [sp[0] for sp in specs],
        out_specs=[pl.BlockSpec((tm, d), lambda i, j: (i, 0))] + [sp[1] for sp in specs],
        out_shape=[jax.ShapeDtypeStruct((t, d), F32)] + [sp[2] for sp in specs],
        scratch_shapes=[pltpu.VMEM((tm, d), BF16)],
        compiler_params=_cparams(("parallel", "arbitrary")),
        name="ffn",
    )(h, norm_g, wg, wu, wd, final_g, *[w for w, _ in casts])
    return outs[0], list(outs[1:])


def _pool_body(x_ref, xp_ref, xn_ref, g_ref, w_ref, sc_ref, o_ref, ext_ref, pa_ref, pb_ref, *, seq):
    i = pl.program_id(1)
    ts = x_ref.shape[0]
    cg = w_ref.shape[1]
    g = g_ref[...]
    first = i == 0
    last = i == pl.num_programs(1) - 1
    ext_ref[0:POOL_HALO, :] = jnp.where(first, 0.0, _rms(xp_ref[...], g))
    ext_ref[POOL_HALO:POOL_HALO + ts, :] = _rms(x_ref[...], g)
    ext_ref[POOL_HALO + ts:POOL_HALO + ts + POOL_HALO, :] = jnp.where(last, 0.0, _rms(xn_ref[...], g))

    t_glob = i * ts + lax.broadcasted_iota(jnp.int32, (ts, 1), 0)
    for gi, w in enumerate(POOL_WINDOWS):
        lanes = slice(gi * cg, (gi + 1) * cg)
        levels = w.bit_length() - 1
        src, lane_sel, start = ext_ref, lanes, POOL_HALO - w // 2
        for j in range(1, levels + 1):
            n = ts + POOL_LEVEL_PAD * (levels - j)
            step = 2 ** (j - 1)
            acc = src[start:start + n, lane_sel] + src[start + step:start + step + n, lane_sel]
            if j < levels:
                dst = (pa_ref, pb_ref)[j % 2]
                dst[0:n, :] = acc
                src, lane_sel, start = dst, slice(None), 0
        lo = jnp.clip(t_glob - w // 2, 0, seq)
        hi = jnp.clip(t_glob - w // 2 + w, 0, seq)
        inv_cnt = 1.0 / (hi - lo).astype(F32)
        pooled = acc * inv_cnt - ext_ref[POOL_HALO:POOL_HALO + ts, lanes]
        y = _dot(pooled.astype(BF16), w_ref[gi])
        o_ref[:, lanes] = x_ref[:, lanes] + y * sc_ref[:, lanes]


def _pool(h, norm_g, w_grp, scale, *, ts):
    b, s, d = h.shape
    ng, cg, _ = w_grp.shape
    nb = ts // POOL_HALO
    last_blk = s // POOL_HALO - 1
    return pl.pallas_call(
        functools.partial(_pool_body, seq=s),
        grid=(b, s // ts),
        in_specs=[
            pl.BlockSpec((None, ts, d), lambda bi, i: (bi, i, 0)),
            pl.BlockSpec((None, POOL_HALO, d), lambda bi, i: (bi, jnp.maximum(i * nb - 1, 0), 0)),
            pl.BlockSpec((None, POOL_HALO, d), lambda bi, i: (bi, jnp.minimum((i + 1) * nb, last_blk), 0)),
            pl.BlockSpec((1, d), lambda bi, i: (0, 0)),
            pl.BlockSpec((ng, cg, cg), lambda bi, i: (0, 0, 0)),
            pl.BlockSpec((1, d), lambda bi, i: (0, 0)),
        ],
        out_specs=pl.BlockSpec((None, ts, d), lambda bi, i: (bi, i, 0)),
        out_shape=jax.ShapeDtypeStruct((b, s, d), F32),
        scratch_shapes=[pltpu.VMEM((ts + 2 * POOL_HALO, d), F32)]
        + [pltpu.VMEM((ts + POOL_LEVEL_PAD * (max(POOL_WINDOWS).bit_length() - 2), cg), F32)] * 2,
        compiler_params=_cparams(("parallel", "parallel")),
        name="pool",
    )(h, h, h, norm_g, w_grp, scale)


def _norm_and_shift(x_ref, xp_ref, xn_ref, g, i, n_tiles, lo=0, hi=None):
    tm = x_ref.shape[0]
    hi = tm if hi is None else hi
    n = hi - lo
    hn = _rms(x_ref[lo:hi, :], g)
    if lo == 0:
        prev_row = jnp.where(i == 0, 0.0, _rms(xp_ref[...], g)[SHIFT_HALO - 1:SHIFT_HALO])
    else:
        prev_row = _rms(x_ref[lo - SHIFT_HALO:lo, :], g)[SHIFT_HALO - 1:SHIFT_HALO]
    if hi == tm:
        next_row = jnp.where(i == n_tiles - 1, 0.0, _rms(xn_ref[...], g)[0:1])
    else:
        next_row = _rms(x_ref[hi:hi + SHIFT_HALO, :], g)[0:1]
    row = lax.broadcasted_iota(jnp.int32, (n, 1), 0)
    prev = jnp.where(row == 0, prev_row, pltpu.roll(hn, 1, axis=0))
    nxt = jnp.where(row == n - 1, next_row, pltpu.roll(hn, n - 1, axis=0))
    return hn, 0.5 * (prev + nxt) - hn


def _run_chains(chains, wave=1, skew=1):
    live = list(range(len(chains)))
    tick = 0
    while live:
        live = [i for i in live if (i // wave) * skew > tick or next(chains[i], _DONE) is not _DONE]
        tick += 1


def _shift_specs(s, d, tm):
    nb = tm // SHIFT_HALO
    last_blk = s // SHIFT_HALO - 1
    return [
        pl.BlockSpec((None, tm, d), lambda bi, i, *_: (bi, i, 0)),
        pl.BlockSpec((None, SHIFT_HALO, d), lambda bi, i, *_: (bi, jnp.maximum(i * nb - 1, 0), 0)),
        pl.BlockSpec((None, SHIFT_HALO, d), lambda bi, i, *_: (bi, jnp.minimum((i + 1) * nb, last_blk), 0)),
    ]


def _rkv_body(x_ref, xp_ref, xn_ref, g_ref, mu_ref, wr_ref, wk_ref, wv_ref, o_ref):
    tm = x_ref.shape[0]

    def slab_steps(lo, hi):
        hn, xx = _norm_and_shift(x_ref, xp_ref, xn_ref, g_ref[...], pl.program_id(1), pl.num_programs(1), lo, hi)
        yield
        for j, w_ref in enumerate((wr_ref, wk_ref, wv_ref)):
            xm = (hn + xx * mu_ref[j]).astype(BF16)
            o_ref[j, lo:hi, :] = _dot(xm, w_ref[...]).astype(o_ref.dtype)
            yield

    slab = min(tm, ROW_SLAB)
    _run_chains([slab_steps(lo, lo + slab) for lo in range(0, tm, slab)])


def _rwkv_rkv(h, norm_g, mu3, w_r, w_k, w_v, *, tm):
    b, s, d = h.shape
    return pl.pallas_call(
        _rkv_body,
        grid=(b, s // tm),
        in_specs=_shift_specs(s, d, tm) + [
            pl.BlockSpec((1, d), lambda bi, i: (0, 0)),
            pl.BlockSpec((3, 1, d), lambda bi, i: (0, 0, 0)),
        ] + [pl.BlockSpec((d, d), lambda bi, i: (0, 0))] * 3,
        out_specs=pl.BlockSpec((3, None, tm, d), lambda bi, i: (0, bi, i, 0)),
        out_shape=jax.ShapeDtypeStruct((3, b, s, d), BF16),
        compiler_params=_cparams(("parallel", "parallel")),
        name="rwkv_rkv",
    )(h, h, h, norm_g, mu3, w_r, w_k, w_v)


def _lora_body(x_ref, xp_ref, xn_ref, g_ref, mu_ref, w1_ref, w2_ref, w0_ref, a1_ref, a2_ref, a0_ref,
               g1_ref, g2_ref, lw_ref, alr_ref, gate_ref):
    tm = x_ref.shape[0]
    neg_rate = -math.exp(-0.5)

    def slab_steps(lo, hi):
        hn, xx = _norm_and_shift(x_ref, xp_ref, xn_ref, g_ref[...], pl.program_id(1), pl.num_programs(1), lo, hi)
        xw = (hn + xx * mu_ref[0]).astype(BF16)
        xa = (hn + xx * mu_ref[1]).astype(BF16)
        xg = (hn + xx * mu_ref[2]).astype(BF16)
        tw = [_dot(xw, w1_ref[dr]) for dr in range(2)]
        ta = [_dot(xa, a1_ref[dr]) for dr in range(2)]
        tg = _dot(xg, g1_ref[...])
        yield
        z = [_dot(jnp.tanh(tw[dr]).astype(BF16), w2_ref[dr]) for dr in range(2)]
        av = [_dot(ta[dr].astype(BF16), a2_ref[dr]) for dr in range(2)]
        gate = _dot(_sigmoid(tg).astype(BF16), g2_ref[...])
        yield
        for dr in range(2):
            lw_ref[dr, lo:hi, :] = neg_rate * _sigmoid(w0_ref[dr] + z[dr])
            alr_ref[dr, lo:hi, :] = _sigmoid(a0_ref[dr] + av[dr]).astype(alr_ref.dtype)
        gate_ref[lo:hi, :] = gate.astype(gate_ref.dtype)

    slab = min(tm, ROW_SLAB)
    _run_chains([slab_steps(lo, lo + slab) for lo in range(0, tm, slab)])


def _rwkv_lora(h, norm_g, mu3, w1, w2, w0, a1, a2, a0, g1, g2, *, tm):
    b, s, d = h.shape
    lw_, la_, lg_ = w1.shape[2], a1.shape[2], g1.shape[1]
    full = lambda *shape: pl.BlockSpec(shape, lambda bi, i: (0,) * len(shape))
    return pl.pallas_call(
        _lora_body,
        grid=(b, s // tm),
        in_specs=_shift_specs(s, d, tm) + [
            full(1, d), full(3, 1, d),
            full(2, d, lw_), full(2, lw_, d), full(2, 1, d),
            full(2, d, la_), full(2, la_, d), full(2, 1, d),
            full(d, lg_), full(lg_, d),
        ],
        out_specs=[
            pl.BlockSpec((2, None, tm, d), lambda bi, i: (0, bi, i, 0)),
            pl.BlockSpec((2, None, tm, d), lambda bi, i: (0, bi, i, 0)),
            pl.BlockSpec((None, tm, d), lambda bi, i: (bi, i, 0)),
        ],
        out_shape=[
            jax.ShapeDtypeStruct((2, b, s, d), F32),
            jax.ShapeDtypeStruct((2, b, s, d), BF16),
            jax.ShapeDtypeStruct((b, s, d), BF16),
        ],
        compiler_params=_cparams(("parallel", "parallel")),
        name="rwkv_lora",
    )(h, h, h, norm_g, mu3, w1, w2, w0, a1, a2, a0, g1, g2)


def _prefix_sum_rows(x, row_idx):
    shift = 1
    while shift < x.shape[0]:
        x = x + jnp.where(row_idx >= shift, pltpu.roll(x, shift, axis=0), 0.0)
        shift *= 2
    return x


def _head_groups(lanes, value):
    li = lax.broadcasted_iota(jnp.int32, (lanes, lanes), 0) // HEAD
    lj = lax.broadcasted_iota(jnp.int32, (lanes, lanes), 1) // HEAD
    return jnp.where(li == lj, value, 0.0).astype(BF16)


def _stack(x, lane_head):
    return jnp.concatenate([jnp.where(lane_head == 0, x, 0.0), jnp.where(lane_head == 1, x, 0.0)], axis=0)


def _keep_rows(x, ch, skip, reverse):
    if skip == 0:
        return x
    blocks = [x[h * ch:(h + 1) * ch - skip] if reverse else x[h * ch + skip:(h + 1) * ch] for h in range(x.shape[0] // ch)]
    return jnp.concatenate(blocks, axis=0)


def _restore_rows(y, ch, skip, reverse, base=None):
    if skip == 0:
        return y if base is None else base + y
    keep = ch - skip
    out = []
    for h in range(y.shape[0] // keep):
        part = y[h * keep:(h + 1) * keep]
        if base is None:
            fill = jnp.zeros((skip, y.shape[1]), y.dtype)
        else:
            blk = base[h * ch:(h + 1) * ch]
            fill = blk[keep:] if reverse else blk[:skip]
            part = part + (blk[:keep] if reverse else blk[skip:])
        out += [part, fill] if reverse else [fill, part]
    return jnp.concatenate(out, axis=0)


def _wkv_body(r_ref, k_ref, v_ref, lw_ref, alr_ref, kkp_ref, kap_ref, rkp_ref, y_ref, bonus_ref, s_ref, *, reverse):
    c = pl.program_id(1)
    nb, ch, d = r_ref.shape
    rows = 2 * ch
    n_pairs = d // PAIR

    @pl.when(c == 0)
    def _():
        s_ref[...] = jnp.zeros_like(s_ref)

    t_loc = lax.broadcasted_iota(jnp.int32, (ch, rows), 0)
    s_loc = lax.broadcasted_iota(jnp.int32, (ch, rows), 1) % ch
    before = (s_loc > t_loc) if reverse else (s_loc < t_loc)
    same = s_loc == t_loc
    incl = before | same
    eye = lax.broadcasted_iota(jnp.int32, (rows, rows), 0) == lax.broadcasted_iota(jnp.int32, (rows, rows), 1)
    tok_row = lax.broadcasted_iota(jnp.int32, (ch, PAIR), 0)
    lane_head = lax.broadcasted_iota(jnp.int32, (ch, PAIR), 1) // HEAD
    group = _head_groups(2 * PAIR, 1.0)
    bd = lambda x: _stack(x, lane_head).astype(BF16)

    def pair_steps(bb, p):
        sl = slice(p * PAIR, (p + 1) * PAIR)
        state = bb * n_pairs + p
        r, k, v = r_ref[bb, :, sl].astype(F32), k_ref[bb, :, sl].astype(F32), v_ref[bb, :, sl].astype(F32)
        lw, alr = lw_ref[bb, :, sl], alr_ref[bb, :, sl].astype(F32)
        run = _prefix_sum_rows(lw, tok_row)
        cl_end = run[ch - 1:ch]
        cl = cl_end - run + lw if reverse else run
        kk_raw = k * kkp_ref[:, sl]
        kd = k * (1.0 + (alr - 1.0) * kap_ref[:, sl])
        sums = _dot(jnp.concatenate([kk_raw * kk_raw, r * kd * rkp_ref[:, sl]], axis=1).astype(BF16), group)
        yield
        kk = kk_raw / jnp.maximum(jnp.sqrt(sums[:, :PAIR]), 1e-12)
        bv = kk * alr
        bonus_ref[bb, :, sl] = (sums[:, PAIR:] * v).astype(bonus_ref.dtype)

        e_neg = jnp.exp(-cl)
        e_end = jnp.exp(cl_end - cl)
        ar = jnp.concatenate([-kk * jnp.exp(cl - lw), r * jnp.exp(cl)], axis=0).astype(BF16)
        gram = _dot_nt(ar, jnp.concatenate([bd(bv * e_neg), bd(kd * e_neg)], axis=0))
        b_e = _stack(bv * e_end, lane_head).T.astype(BF16)
        k_e = _stack(kd * e_end, lane_head).T.astype(BF16)
        v_bd = bd(v)
        decay_end = jnp.where(eye, jnp.exp(cl_end), 0.0).astype(BF16)
        ars = _dot(jnp.concatenate([ar, decay_end], axis=0), s_ref[state].astype(BF16))
        yield
        n_pow = jnp.where(before, gram[:ch, :rows], 0.0)
        a_ak = jnp.where(before, gram[:ch, rows:], 0.0).astype(BF16)
        a_rb = jnp.where(incl, gram[ch:, :rows], 0.0).astype(BF16)
        a_rk = jnp.where(incl, gram[ch:, rows:], 0.0).astype(BF16)
        zyh = _dot(jnp.concatenate([a_ak, a_rk, k_e], axis=0), v_bd)

        inv = jnp.where(same, 1.0, 0.0) + n_pow
        for it in range(1, INV_STEPS + 1):
            power = 2 ** (it - 1)
            nw = bd(n_pow)
            skip_inv = min(ch, power // ROW_GROUP * ROW_GROUP)
            skip_n = min(ch, 2 * power // ROW_GROUP * ROW_GROUP)
            if it == 1:
                n_pow = _dot(n_pow.astype(BF16), nw)
            elif it < INV_STEPS:
                lhs = jnp.concatenate([_keep_rows(inv, ch, skip_inv, reverse), _keep_rows(n_pow, ch, skip_n, reverse)], axis=0)
                both = _dot(lhs.astype(BF16), nw)
                inv = _restore_rows(both[:ch - skip_inv], ch, skip_inv, reverse, base=inv)
                n_pow = _restore_rows(both[ch - skip_inv:], ch, skip_n, reverse)
            else:
                inv = _restore_rows(_dot(_keep_rows(inv, ch, skip_inv, reverse).astype(BF16), nw), ch, skip_inv, reverse, base=inv)
            yield
        u = _dot(inv.astype(BF16), bd(ars[:ch] + zyh[:ch]))
        yield
        fin = _dot(jnp.concatenate([a_rb, b_e], axis=0), bd(u))
        yield
        y_ref[bb, :, sl] = (ars[ch:rows] + fin[:ch] + zyh[ch:rows]).astype(y_ref.dtype)
        s_ref[state] = ars[rows:] + fin[ch:] + zyh[rows:]

    _run_chains([pair_steps(bb, p) for bb in range(nb) for p in range(n_pairs)], WKV_WAVE, WKV_SKEW)


def _wkv(rkv, lw, alr, kkp, kap, rkp, *, reverse):
    _, b, s, d = rkv.shape
    nc = s // CHUNK
    assert CHUNK == HEAD and s % CHUNK == 0 and d % PAIR == 0
    dr = int(reverse)
    tok = (lambda c: nc - 1 - c) if reverse else (lambda c: c)
    nb = math.gcd(b, WKV_SEQS)
    stacked = lambda j: pl.BlockSpec((None, nb, CHUNK, d), lambda bi, c: (j, bi, tok(c), 0))
    out_spec = pl.BlockSpec((nb, CHUNK, d), lambda bi, c: (bi, tok(c), 0))
    par_spec = pl.BlockSpec((1, d), lambda bi, c: (0, 0))
    return pl.pallas_call(
        functools.partial(_wkv_body, reverse=reverse),
        grid=(b // nb, nc),
        in_specs=[stacked(0), stacked(1), stacked(2), stacked(dr), stacked(dr), par_spec, par_spec, par_spec],
        out_specs=[out_spec, out_spec],
        out_shape=[jax.ShapeDtypeStruct((b, s, d), BF16), jax.ShapeDtypeStruct((b, s, d), BF16)],
        scratch_shapes=[pltpu.VMEM((nb * (d // PAIR), PAIR, PAIR), F32)],
        compiler_params=_cparams(("parallel", "arbitrary")),
        name="wkv_rev" if reverse else "wkv_fwd",
    )(rkv, rkv, rkv, lw, alr, kkp, kap, rkp)


def _out_body(x_ref, yf_ref, yr_ref, bf_ref, br_ref, gate_ref, lnw_ref, lnb_ref, wo_ref, o_ref):
    tm, d = x_ref.shape
    avg = _head_groups(GN_LANES, 1.0 / HEAD)

    def slab_steps(lo, hi):
        mixed = []
        for c0 in range(0, d, GN_LANES):
            sl = slice(c0, c0 + GN_LANES)
            y = yf_ref[lo:hi, sl].astype(F32) + yr_ref[lo:hi, sl].astype(F32)
            mean = _dot2(y, avg)
            dev = y - mean
            var = _dot((dev * dev).astype(BF16), avg)
            gn = dev * lax.rsqrt(var + GN_EPS) * lnw_ref[:, sl] + lnb_ref[:, sl]
            bonus = bf_ref[lo:hi, sl].astype(F32) + br_ref[lo:hi, sl].astype(F32)
            mixed.append(((gn + bonus) * gate_ref[lo:hi, sl].astype(F32)).astype(BF16))
            yield
        mixed = jnp.concatenate(mixed, axis=1)
        for c0 in range(0, d, GN_LANES):
            sl = slice(c0, c0 + GN_LANES)
            o_ref[lo:hi, sl] = x_ref[lo:hi, sl] + _dot(mixed, wo_ref[:, sl])
            yield

    slab = min(tm, ROW_SLAB)
    _run_chains([slab_steps(lo, lo + slab) for lo in range(0, tm, slab)], skew=d // GN_LANES)


def _rwkv_out(h, scans, gate, lnw, lnb, wo, *, tm):
    b, s, d = h.shape
    tok = pl.BlockSpec((None, tm, d), lambda bi, i: (bi, i, 0))
    par = pl.BlockSpec((1, d), lambda bi, i: (0, 0))
    return pl.pallas_call(
        _out_body,
        grid=(b, s // tm),
        in_specs=[tok] * 6 + [par, par, pl.BlockSpec((d, d), lambda bi, i: (0, 0))],
        out_specs=tok,
        out_shape=jax.ShapeDtypeStruct((b, s, d), F32),
        compiler_params=_cparams(("parallel", "parallel")),
        name="rwkv_out",
    )(h, *scans, gate, lnw, lnb, wo)


def _pad_lora(w_in, w_out):
    r = w_in.shape[-1]
    pad = (-r) % V7X_LANES
    w_in = jnp.pad(w_in, [(0, 0)] * (w_in.ndim - 1) + [(0, pad)])
    w_out = jnp.pad(w_out, [(0, 0)] * (w_out.ndim - 2) + [(0, pad), (0, 0)])
    return w_in.astype(BF16), w_out.astype(BF16)


def _tile(n, want):
    t = min(n, want)
    while n % t:
        t //= 2
    return t


def _tiles(b, s, f):
    return dict(
        ffn_rows=_tile(b * s, 4 * V7X_MXU_DIM),
        ffn_cols=_tile(f, 2 * V7X_MXU_DIM),
        pool_rows=_tile(s, 4 * V7X_MXU_DIM),
        rwkv_rows=_tile(s, 2 * ROW_SLAB),
    )


def _rwkv_mixer(h, norm_g, mu, w_r, w_k, w_v, w_o, w0, w1, w2, a0, a1, a2, g1, g2, k_k, k_a, r_k, ln_w, ln_b, *, tm2):
    b, s, d = h.shape
    row = lambda z: z.reshape(1, d)
    rkv = _rwkv_rkv(h, norm_g, mu[jnp.array([0, 2, 3])].reshape(3, 1, d), w_r, w_k, w_v, tm=tm2)
    w1p, w2p = _pad_lora(w1, w2)
    a1p, a2p = _pad_lora(a1, a2)
    lw, alr, gate = _rwkv_lora(h, norm_g, mu[jnp.array([1, 4, 5])].reshape(3, 1, d),
                               w1p, w2p, w0.reshape(2, 1, d), a1p, a2p, a0.reshape(2, 1, d),
                               g1.astype(BF16), g2.astype(BF16), tm=tm2)
    y_f, bonus_f = _wkv(rkv, lw, alr, row(k_k), row(k_a), row(r_k), reverse=False)
    y_r, bonus_r = _wkv(rkv, lw, alr, row(k_k), row(k_a), row(r_k), reverse=True)
    return _rwkv_out(h, (y_f, y_r, bonus_f, bonus_r), gate, row(ln_w), row(ln_b), w_o, tm=tm2)


def kernel(x, ffn1_norm, ffn1_gate, ffn1_up, ffn1_down, mix_norm, ffn2_norm, ffn2_gate, ffn2_up, ffn2_down, pool_w, pool_scale, rwkv_mu, rwkv_wr, rwkv_wk, rwkv_wv, rwkv_wo, rwkv_w0, rwkv_w1, rwkv_w2, rwkv_a0, rwkv_a1, rwkv_a2, rwkv_g1, rwkv_g2, rwkv_kk, rwkv_ka, rwkv_rk, rwkv_lnw, rwkv_lnb, final_norm):
    b, s, d = x.shape
    depth = ffn1_norm.shape[0]
    f = ffn1_gate.shape[2]
    row = lambda z: z.reshape(1, d)
    assert x.dtype == F32 and d % V7X_MXU_DIM == 0 and d % len(POOL_WINDOWS) == 0, (x.dtype, d)
    assert s % max(CHUNK, POOL_HALO) == 0 and f % V7X_LANES == 0, (s, f)
    tiles = _tiles(b, s, f)
    ffn_params = ((ffn1_norm, ffn1_gate, ffn1_up, ffn1_down), (ffn2_norm, ffn2_gate, ffn2_up, ffn2_down))
    order = [(i, half) for i in range(depth) for half in (0, 1)]

    weights = [ffn_params[0][k][0].astype(BF16) for k in (1, 2, 3)]
    rwkv_w = None
    h = x
    for q, (i, half) in enumerate(order):
        j = i // 2
        if half == 1:
            if i % 2 == 0:
                h = _pool(h, row(mix_norm[i]), pool_w[j].astype(BF16), row(pool_scale[j]), ts=tiles["pool_rows"])
            else:
                h = _rwkv_mixer(h, row(mix_norm[i]), rwkv_mu[j], *rwkv_w,
                                rwkv_w0[j], rwkv_w1[j], rwkv_w2[j], rwkv_a0[j], rwkv_a1[j], rwkv_a2[j],
                                rwkv_g1[j], rwkv_g2[j], rwkv_kk[j], rwkv_ka[j], rwkv_rk[j],
                                rwkv_lnw[j], rwkv_lnb[j], tm2=tiles["rwkv_rows"])
        casts = []
        if q + 1 < len(order):
            ni, nhalf = order[q + 1]
            casts += [(ffn_params[nhalf][k], ni) for k in (1, 2, 3)]
        if half == 0 and i % 2 == 1:
            casts += [(w, j) for w in (rwkv_wr, rwkv_wk, rwkv_wv, rwkv_wo)]
        out, cast = _ffn(h.reshape(b * s, d), row(ffn_params[half][0][i]), *weights, row(final_norm), casts,
                         final_norm=q == len(order) - 1, tm=tiles["ffn_rows"], tf=tiles["ffn_cols"])
        h = out.reshape(b, s, d)
        if q + 1 < len(order):
            weights, cast = cast[:3], cast[3:]
        if cast:
            rwkv_w = cast
    return h
```

```python
import functools
import math

import jax
import jax.numpy as jnp
from jax import lax
from jax.experimental import pallas as pl
from jax.experimental.pallas import tpu as pltpu

F32 = jnp.float32
BF16 = jnp.bfloat16

V7X_LANES = 128
V7X_SUBLANES_F32 = 8
V7X_SUBLANES_BF16 = 16
V7X_MXU_DIM = 256
V7X_VMEM_BYTES = 64 * 1024 * 1024
VMEM_LIMIT = V7X_VMEM_BYTES - 4 * 1024 * 1024

RMS_EPS = 1e-6
GN_EPS = 64e-5
HEAD = 64
PAIR = 2 * HEAD
assert PAIR == V7X_LANES
CHUNK = 64
INV_STEPS = (CHUNK - 1).bit_length()
ROW_GROUP = V7X_SUBLANES_BF16
WKV_SEQS = 4
WKV_WAVE = 2
WKV_SKEW = 1
POOL_WINDOWS = (2, 4, 8, 16)
POOL_HALO = 32
POOL_LEVEL_PAD = V7X_SUBLANES_F32
assert max(POOL_WINDOWS) <= 2 * POOL_LEVEL_PAD <= POOL_HALO
SHIFT_HALO = V7X_SUBLANES_F32
GN_LANES = V7X_MXU_DIM
ROW_SLAB = V7X_MXU_DIM
FFN_DOWN_COLS = 512
_DONE = object()


def _cparams(sem):
    return pltpu.CompilerParams(dimension_semantics=sem, vmem_limit_bytes=VMEM_LIMIT)


def _rms(x, g):
    return x * lax.rsqrt(jnp.mean(x * x, axis=-1, keepdims=True) + RMS_EPS) * g


def _sigmoid(x):
    return 0.5 * jnp.tanh(0.5 * x) + 0.5


def _dot(a, b):
    return jnp.dot(a, b, preferred_element_type=F32)


def _dot_nt(a, b):
    return lax.dot_general(a, b, (((1,), (1,)), ((), ())), preferred_element_type=F32)


def _dot2(x, m_bf16):
    hi = x.astype(BF16)
    lo = (x - hi.astype(F32)).astype(BF16)
    return _dot(hi, m_bf16) + _dot(lo, m_bf16)


def _ffn_body(*refs, final_norm, n_cast):
    x_ref, g_ref, wg_ref, wu_ref, wd_ref, fg_ref = refs[:6]
    cast_src = refs[6:6 + n_cast]
    o_ref = refs[6 + n_cast]
    cast_dst = refs[7 + n_cast:7 + 2 * n_cast]
    n_ref = refs[7 + 2 * n_cast]
    f = pl.program_id(1)
    d = o_ref.shape[1]

    @pl.when(f == 0)
    def _():
        x = x_ref[...]
        n_ref[...] = _rms(x, g_ref[...]).astype(BF16)
        o_ref[...] = x

    n = n_ref[...]
    gate = _dot(n, wg_ref[...])
    up = _dot(n, wu_ref[...])
    act = (0.5 * gate * jax.nn.sigmoid(gate) * up).astype(BF16)
    for c0 in range(0, d, FFN_DOWN_COLS):
        cols = slice(c0, c0 + FFN_DOWN_COLS)
        o_ref[:, cols] += _dot(act, wd_ref[:, cols])

    for src, dst in zip(cast_src, cast_dst):
        dst[...] = src[...].astype(dst.dtype)

    if final_norm:
        @pl.when(f == pl.num_programs(1) - 1)
        def _():
            o_ref[...] = _rms(o_ref[...], fg_ref[...])


def _cast_specs(stacked, layer, ni, nj):
    _, rows, cols = stacked.shape
    ok = lambda n, parts, unit: n % parts == 0 and (n // parts) % unit == 0
    row_ok = lambda parts: ok(rows, parts, V7X_SUBLANES_BF16)
    col_ok = lambda parts: ok(cols, parts, V7X_LANES)
    if row_ok(ni) and col_ok(nj):
        blk, imap = (rows // ni, cols // nj), lambda i, j: (i, j)
    elif row_ok(nj) and col_ok(ni):
        blk, imap = (rows // nj, cols // ni), lambda i, j: (j, i)
    else:
        assert row_ok(ni), (rows, ni)
        nc = 1
        while nc * 2 <= nj and col_ok(nc * 2):
            nc *= 2
        blk, imap = (rows // ni, cols // nc), lambda i, j: (i, jnp.minimum(j, nc - 1))
    in_spec = pl.BlockSpec((None,) + blk, lambda i, j: (layer,) + imap(i, j))
    return in_spec, pl.BlockSpec(blk, imap), jax.ShapeDtypeStruct((rows, cols), BF16)


def _ffn(h, norm_g, wg, wu, wd, final_g, casts, *, final_norm, tm, tf):
    t, d = h.shape
    f = wg.shape[1]
    ni, nj = t // tm, f // tf
    specs = [_cast_specs(w, layer, ni, nj) for w, layer in casts]
    outs = pl.pallas_call(
        functools.partial(_ffn_body, final_norm=final_norm, n_cast=len(casts)),
        grid=(ni, nj),
        in_specs=[
            pl.BlockSpec((tm, d), lambda i, j: (i, 0)),
            pl.BlockSpec((1, d), lambda i, j: (0, 0)),
            pl.BlockSpec((d, tf), lambda i, j: (0, j)),
            pl.BlockSpec((d, tf), lambda i, j: (0, j)),
            pl.BlockSpec((tf, d), lambda i, j: (j, 0)),
            pl.BlockSpec((1, d), lambda i, j: (0, 0)),
        ] + [sp[0] for sp in specs],
        out_specs=[pl.BlockSpec((tm, d), lambda i, j: (i, 0))] + [sp[1] for sp in specs],
        out_shape=[jax.ShapeDtypeStruct((t, d), F32)] + [sp[2] for sp in specs],
        scratch_shapes=[pltpu.VMEM((tm, d), BF16)],
        compiler_params=_cparams(("parallel", "arbitrary")),
        name="ffn",
    )(h, norm_g, wg, wu, wd, final_g, *[w for w, _ in casts])
    return outs[0], list(outs[1:])


def _pool_body(x_ref, xp_ref, xn_ref, g_ref, w_ref, sc_ref, o_ref, ext_ref, pa_ref, pb_ref, *, seq):
    i = pl.program_id(1)
    ts = x_ref.shape[0]
    cg = w_ref.shape[1]
    g = g_ref[...]
    first = i == 0
    last = i == pl.num_programs(1) - 1
    ext_ref[0:POOL_HALO, :] = jnp.where(first, 0.0, _rms(xp_ref[...], g))
    ext_ref[POOL_HALO:POOL_HALO + ts, :] = _rms(x_ref[...], g)
    ext_ref[POOL_HALO + ts:POOL_HALO + ts + POOL_HALO, :] = jnp.where(last, 0.0, _rms(xn_ref[...], g))

    t_glob = i * ts + lax.broadcasted_iota(jnp.int32, (ts, 1), 0)
    for gi, w in enumerate(POOL_WINDOWS):
        lanes = slice(gi * cg, (gi + 1) * cg)
        levels = w.bit_length() - 1
        src, lane_sel, start = ext_ref, lanes, POOL_HALO - w // 2
        for j in range(1, levels + 1):
            n = ts + POOL_LEVEL_PAD * (levels - j)
            step = 2 ** (j - 1)
            acc = src[start:start + n, lane_sel] + src[start + step:start + step + n, lane_sel]
            if j < levels:
                dst = (pa_ref, pb_ref)[j % 2]
                dst[0:n, :] = acc
                src, lane_sel, start = dst, slice(None), 0
        lo = jnp.clip(t_glob - w // 2, 0, seq)
        hi = jnp.clip(t_glob - w // 2 + w, 0, seq)
        inv_cnt = 1.0 / (hi - lo).astype(F32)
        pooled = acc * inv_cnt - ext_ref[POOL_HALO:POOL_HALO + ts, lanes]
        y = _dot(pooled.astype(BF16), w_ref[gi])
        o_ref[:, lanes] = x_ref[:, lanes] + y * sc_ref[:, lanes]


def _pool(h, norm_g, w_grp, scale, *, ts):
    b, s, d = h.shape
    ng, cg, _ = w_grp.shape
    nb = ts // POOL_HALO
    last_blk = s // POOL_HALO - 1
    return pl.pallas_call(
        functools.partial(_pool_body, seq=s),
        grid=(b, s // ts),
        in_specs=[
            pl.BlockSpec((None, ts, d), lambda bi, i: (bi, i, 0)),
            pl.BlockSpec((None, POOL_HALO, d), lambda bi, i: (bi, jnp.maximum(i * nb - 1, 0), 0)),
            pl.BlockSpec((None, POOL_HALO, d), lambda bi, i: (bi, jnp.minimum((i + 1) * nb, last_blk), 0)),
            pl.BlockSpec((1, d), lambda bi, i: (0, 0)),
            pl.BlockSpec((ng, cg, cg), lambda bi, i: (0, 0, 0)),
            pl.BlockSpec((1, d), lambda bi, i: (0, 0)),
        ],
        out_specs=pl.BlockSpec((None, ts, d), lambda bi, i: (bi, i, 0)),
        out_shape=jax.ShapeDtypeStruct((b, s, d), F32),
        scratch_shapes=[pltpu.VMEM((ts + 2 * POOL_HALO, d), F32)]
        + [pltpu.VMEM((ts + POOL_LEVEL_PAD * (max(POOL_WINDOWS).bit_length() - 2), cg), F32)] * 2,
        compiler_params=_cparams(("parallel", "parallel")),
        name="pool",
    )(h, h, h, norm_g, w_grp, scale)


def _norm_and_shift(x_ref, xp_ref, xn_ref, g, i, n_tiles, lo=0, hi=None):
    tm = x_ref.shape[0]
    hi = tm if hi is None else hi
    n = hi - lo
    hn = _rms(x_ref[lo:hi, :], g)
    if lo == 0:
        prev_row = jnp.where(i == 0, 0.0, _rms(xp_ref[...], g)[SHIFT_HALO - 1:SHIFT_HALO])
    else:
        prev_row = _rms(x_ref[lo - SHIFT_HALO:lo, :], g)[SHIFT_HALO - 1:SHIFT_HALO]
    if hi == tm:
        next_row = jnp.where(i == n_tiles - 1, 0.0, _rms(xn_ref[...], g)[0:1])
    else:
        next_row = _rms(x_ref[hi:hi + SHIFT_HALO, :], g)[0:1]
    row = lax.broadcasted_iota(jnp.int32, (n, 1), 0)
    prev = jnp.where(row == 0, prev_row, pltpu.roll(hn, 1, axis=0))
    nxt = jnp.where(row == n - 1, next_row, pltpu.roll(hn, n - 1, axis=0))
    return hn, 0.5 * (prev + nxt) - hn


def _run_chains(chains, wave=1, skew=1):
    live = list(range(len(chains)))
    tick = 0
    while live:
        live = [i for i in live if (i // wave) * skew > tick or next(chains[i], _DONE) is not _DONE]
        tick += 1


def _shift_specs(s, d, tm):
    nb = tm // SHIFT_HALO
    last_blk = s // SHIFT_HALO - 1
    return [
        pl.BlockSpec((None, tm, d), lambda bi, i, *_: (bi, i, 0)),
        pl.BlockSpec((None, SHIFT_HALO, d), lambda bi, i, *_: (bi, jnp.maximum(i * nb - 1, 0), 0)),
        pl.BlockSpec((None, SHIFT_HALO, d), lambda bi, i, *_: (bi, jnp.minimum((i + 1) * nb, last_blk), 0)),
    ]


def _rkv_body(x_ref, xp_ref, xn_ref, g_ref, mu_ref, wr_ref, wk_ref, wv_ref, o_ref):
    tm = x_ref.shape[0]

    def slab_steps(lo, hi):
        hn, xx = _norm_and_shift(x_ref, xp_ref, xn_ref, g_ref[...], pl.program_id(1), pl.num_programs(1), lo, hi)
        yield
        for j, w_ref in enumerate((wr_ref, wk_ref, wv_ref)):
            xm = (hn + xx * mu_ref[j]).astype(BF16)
            o_ref[j, lo:hi, :] = _dot(xm, w_ref[...]).astype(o_ref.dtype)
            yield

    slab = min(tm, ROW_SLAB)
    _run_chains([slab_steps(lo, lo + slab) for lo in range(0, tm, slab)])


def _rwkv_rkv(h, norm_g, mu3, w_r, w_k, w_v, *, tm):
    b, s, d = h.shape
    return pl.pallas_call(
        _rkv_body,
        grid=(b, s // tm),
        in_specs=_shift_specs(s, d, tm) + [
            pl.BlockSpec((1, d), lambda bi, i: (0, 0)),
            pl.BlockSpec((3, 1, d), lambda bi, i: (0, 0, 0)),
        ] + [pl.BlockSpec((d, d), lambda bi, i: (0, 0))] * 3,
        out_specs=pl.BlockSpec((3, None, tm, d), lambda bi, i: (0, bi, i, 0)),
        out_shape=jax.ShapeDtypeStruct((3, b, s, d), BF16),
        compiler_params=_cparams(("parallel", "parallel")),
        name="rwkv_rkv",
    )(h, h, h, norm_g, mu3, w_r, w_k, w_v)


def _lora_body(x_ref, xp_ref, xn_ref, g_ref, mu_ref, w1_ref, w2_ref, w0_ref, a1_ref, a2_ref, a0_ref,
               g1_ref, g2_ref, lw_ref, alr_ref, gate_ref):
    tm = x_ref.shape[0]
    neg_rate = -math.exp(-0.5)

    def slab_steps(lo, hi):
        hn, xx = _norm_and_shift(x_ref, xp_ref, xn_ref, g_ref[...], pl.program_id(1), pl.num_programs(1), lo, hi)
        xw = (hn + xx * mu_ref[0]).astype(BF16)
        xa = (hn + xx * mu_ref[1]).astype(BF16)
        xg = (hn + xx * mu_ref[2]).astype(BF16)
        tw = [_dot(xw, w1_ref[dr]) for dr in range(2)]
        ta = [_dot(xa, a1_ref[dr]) for dr in range(2)]
        tg = _dot(xg, g1_ref[...])
        yield
        z = [_dot(jnp.tanh(tw[dr]).astype(BF16), w2_ref[dr]) for dr in range(2)]
        av = [_dot(ta[dr].astype(BF16), a2_ref[dr]) for dr in range(2)]
        gate = _dot(_sigmoid(tg).astype(BF16), g2_ref[...])
        yield
        for dr in range(2):
            lw_ref[dr, lo:hi, :] = neg_rate * _sigmoid(w0_ref[dr] + z[dr])
            alr_ref[dr, lo:hi, :] = _sigmoid(a0_ref[dr] + av[dr]).astype(alr_ref.dtype)
        gate_ref[lo:hi, :] = gate.astype(gate_ref.dtype)

    slab = min(tm, ROW_SLAB)
    _run_chains([slab_steps(lo, lo + slab) for lo in range(0, tm, slab)])


def _rwkv_lora(h, norm_g, mu3, w1, w2, w0, a1, a2, a0, g1, g2, *, tm):
    b, s, d = h.shape
    lw_, la_, lg_ = w1.shape[2], a1.shape[2], g1.shape[1]
    full = lambda *shape: pl.BlockSpec(shape, lambda bi, i: (0,) * len(shape))
    return pl.pallas_call(
        _lora_body,
        grid=(b, s // tm),
        in_specs=_shift_specs(s, d, tm) + [
            full(1, d), full(3, 1, d),
            full(2, d, lw_), full(2, lw_, d), full(2, 1, d),
            full(2, d, la_), full(2, la_, d), full(2, 1, d),
            full(d, lg_), full(lg_, d),
        ],
        out_specs=[
            pl.BlockSpec((2, None, tm, d), lambda bi, i: (0, bi, i, 0)),
            pl.BlockSpec((2, None, tm, d), lambda bi, i: (0, bi, i, 0)),
            pl.BlockSpec((None, tm, d), lambda bi, i: (bi, i, 0)),
        ],
        out_shape=[
            jax.ShapeDtypeStruct((2, b, s, d), F32),
            jax.ShapeDtypeStruct((2, b, s, d), BF16),
            jax.ShapeDtypeStruct((b, s, d), BF16),
        ],
        compiler_params=_cparams(("parallel", "parallel")),
        name="rwkv_lora",
    )(h, h, h, norm_g, mu3, w1, w2, w0, a1, a2, a0, g1, g2)


def _prefix_sum_rows(x, row_idx):
    shift = 1
    while shift < x.shape[0]:
        x = x + jnp.where(row_idx >= shift, pltpu.roll(x, shift, axis=0), 0.0)
        shift *= 2
    return x


def _head_groups(lanes, value):
    li = lax.broadcasted_iota(jnp.int32, (lanes, lanes), 0) // HEAD
    lj = lax.broadcasted_iota(jnp.int32, (lanes, lanes), 1) // HEAD
    return jnp.where(li == lj, value, 0.0).astype(BF16)


def _stack(x, lane_head):
    return jnp.concatenate([jnp.where(lane_head == 0, x, 0.0), jnp.where(lane_head == 1, x, 0.0)], axis=0)


def _keep_rows(x, ch, skip, reverse):
    if skip == 0:
        return x
    blocks = [x[h * ch:(h + 1) * ch - skip] if reverse else x[h * ch + skip:(h + 1) * ch] for h in range(x.shape[0] // ch)]
    return jnp.concatenate(blocks, axis=0)


def _restore_rows(y, ch, skip, reverse, base=None):
    if skip == 0:
        return y if base is None else base + y
    keep = ch - skip
    out = []
    for h in range(y.shape[0] // keep):
        part = y[h * keep:(h + 1) * keep]
        if base is None:
            fill = jnp.zeros((skip, y.shape[1]), y.dtype)
        else:
            blk = base[h * ch:(h + 1) * ch]
            fill = blk[keep:] if reverse else blk[:skip]
            part = part + (blk[:keep] if reverse else blk[skip:])
        out += [part, fill] if reverse else [fill, part]
    return jnp.concatenate(out, axis=0)


def _wkv_chains(r_ref, k_ref, v_ref, lw_ref, alr_ref, kkp_ref, kap_ref, rkp_ref, y_ref, bonus_ref, s_ref, *, reverse, state0):
    nb, ch, d = r_ref.shape
    rows = 2 * ch
    n_pairs = d // PAIR

    t_loc = lax.broadcasted_iota(jnp.int32, (ch, rows), 0)
    s_loc = lax.broadcasted_iota(jnp.int32, (ch, rows), 1) % ch
    before = (s_loc > t_loc) if reverse else (s_loc < t_loc)
    same = s_loc == t_loc
    incl = before | same
    eye = lax.broadcasted_iota(jnp.int32, (rows, rows), 0) == lax.broadcasted_iota(jnp.int32, (rows, rows), 1)
    tok_row = lax.broadcasted_iota(jnp.int32, (ch, PAIR), 0)
    lane_head = lax.broadcasted_iota(jnp.int32, (ch, PAIR), 1) // HEAD
    group = _head_groups(2 * PAIR, 1.0)
    bd = lambda x: _stack(x, lane_head).astype(BF16)

    def pair_steps(bb, p):
        sl = slice(p * PAIR, (p + 1) * PAIR)
        state = state0 + bb * n_pairs + p
        r, k, v = r_ref[bb, :, sl].astype(F32), k_ref[bb, :, sl].astype(F32), v_ref[bb, :, sl].astype(F32)
        lw, alr = lw_ref[bb, :, sl], alr_ref[bb, :, sl].astype(F32)
        run = _prefix_sum_rows(lw, tok_row)
        cl_end = run[ch - 1:ch]
        cl = cl_end - run + lw if reverse else run
        kk_raw = k * kkp_ref[:, sl]
        kd = k * (1.0 + (alr - 1.0) * kap_ref[:, sl])
        sums = _dot(jnp.concatenate([kk_raw * kk_raw, r * kd * rkp_ref[:, sl]], axis=1).astype(BF16), group)
        yield
        kk = kk_raw / jnp.maximum(jnp.sqrt(sums[:, :PAIR]), 1e-12)
        bv = kk * alr
        bonus_ref[bb, :, sl] = (sums[:, PAIR:] * v).astype(bonus_ref.dtype)

        e_neg = jnp.exp(-cl)
        e_end = jnp.exp(cl_end - cl)
        ar = jnp.concatenate([-kk * jnp.exp(cl - lw), r * jnp.exp(cl)], axis=0).astype(BF16)
        gram = _dot_nt(ar, jnp.concatenate([bd(bv * e_neg), bd(kd * e_neg)], axis=0))
        b_e = _stack(bv * e_end, lane_head).T.astype(BF16)
        k_e = _stack(kd * e_end, lane_head).T.astype(BF16)
        v_bd = bd(v)
        decay_end = jnp.where(eye, jnp.exp(cl_end), 0.0).astype(BF16)
        ars = _dot(jnp.concatenate([ar, decay_end], axis=0), s_ref[state].astype(BF16))
        yield
        n_pow = jnp.where(before, gram[:ch, :rows], 0.0)
        a_ak = jnp.where(before, gram[:ch, rows:], 0.0).astype(BF16)
        a_rb = jnp.where(incl, gram[ch:, :rows], 0.0).astype(BF16)
        a_rk = jnp.where(incl, gram[ch:, rows:], 0.0).astype(BF16)
        zyh = _dot(jnp.concatenate([a_ak, a_rk, k_e], axis=0), v_bd)

        inv = jnp.where(same, 1.0, 0.0) + n_pow
        for it in range(1, INV_STEPS + 1):
            power = 2 ** (it - 1)
            nw = bd(n_pow)
            skip_inv = min(ch, power // ROW_GROUP * ROW_GROUP)
            skip_n = min(ch, 2 * power // ROW_GROUP * ROW_GROUP)
            if it == 1:
                n_pow = _dot(n_pow.astype(BF16), nw)
            elif it < INV_STEPS:
                lhs = jnp.concatenate([_keep_rows(inv, ch, skip_inv, reverse), _keep_rows(n_pow, ch, skip_n, reverse)], axis=0)
                both = _dot(lhs.astype(BF16), nw)
                inv = _restore_rows(both[:ch - skip_inv], ch, skip_inv, reverse, base=inv)
                n_pow = _restore_rows(both[ch - skip_inv:], ch, skip_n, reverse)
            else:
                inv = _restore_rows(_dot(_keep_rows(inv, ch, skip_inv, reverse).astype(BF16), nw), ch, skip_inv, reverse, base=inv)
            yield
        u = _dot(inv.astype(BF16), bd(ars[:ch] + zyh[:ch]))
        yield
        fin = _dot(jnp.concatenate([a_rb, b_e], axis=0), bd(u))
        yield
        y_ref[bb, :, sl] = (ars[ch:rows] + fin[:ch] + zyh[ch:rows]).astype(y_ref.dtype)
        s_ref[state] = ars[rows:] + fin[ch:] + zyh[rows:]

    return [pair_steps(bb, p) for bb in range(nb) for p in range(n_pairs)]


def _wkv_body(rf_ref, kf_ref, vf_ref, lwf_ref, af_ref, rr_ref, kr_ref, vr_ref, lwr_ref, ar_ref,
              kkp_ref, kap_ref, rkp_ref, yf_ref, bf_ref, yr_ref, br_ref, s_ref):
    @pl.when(pl.program_id(1) == 0)
    def _():
        s_ref[...] = jnp.zeros_like(s_ref)

    params = (kkp_ref, kap_ref, rkp_ref)
    n_states = s_ref.shape[0] // 2
    fwd = _wkv_chains(rf_ref, kf_ref, vf_ref, lwf_ref, af_ref, *params, yf_ref, bf_ref, s_ref, reverse=False, state0=0)
    rev = _wkv_chains(rr_ref, kr_ref, vr_ref, lwr_ref, ar_ref, *params, yr_ref, br_ref, s_ref, reverse=True,
                      state0=n_states)
    _run_chains([chain for both in zip(fwd, rev) for chain in both], WKV_WAVE, WKV_SKEW)


def _wkv(rkv, lw, alr, kkp, kap, rkp):
    _, b, s, d = rkv.shape
    nc = s // CHUNK
    assert CHUNK == HEAD and s % CHUNK == 0 and d % PAIR == 0
    nb = math.gcd(b, WKV_SEQS)
    toks = (lambda c: c, lambda c: nc - 1 - c)
    stacked = lambda j, tok: pl.BlockSpec((None, nb, CHUNK, d), lambda bi, c: (j, bi, tok(c), 0))
    out_spec = lambda tok: pl.BlockSpec((nb, CHUNK, d), lambda bi, c: (bi, tok(c), 0))
    par_spec = pl.BlockSpec((1, d), lambda bi, c: (0, 0))
    in_specs = [stacked(j, toks[dr]) for dr in (0, 1) for j in (0, 1, 2, dr, dr)]
    return pl.pallas_call(
        _wkv_body,
        grid=(b // nb, nc),
        in_specs=in_specs + [par_spec] * 3,
        out_specs=[out_spec(toks[0])] * 2 + [out_spec(toks[1])] * 2,
        out_shape=[jax.ShapeDtypeStruct((b, s, d), BF16)] * 4,
        scratch_shapes=[pltpu.VMEM((2 * nb * (d // PAIR), PAIR, PAIR), F32)],
        compiler_params=_cparams(("parallel", "arbitrary")),
        name="wkv",
    )(rkv, rkv, rkv, lw, alr, rkv, rkv, rkv, lw, alr, kkp, kap, rkp)


def _out_body(x_ref, yf_ref, yr_ref, bf_ref, br_ref, gate_ref, lnw_ref, lnb_ref, wo_ref, o_ref):
    tm, d = x_ref.shape
    avg = _head_groups(GN_LANES, 1.0 / HEAD)

    def slab_steps(lo, hi):
        mixed = []
        for c0 in range(0, d, GN_LANES):
            sl = slice(c0, c0 + GN_LANES)
            y = yf_ref[lo:hi, sl].astype(F32) + yr_ref[lo:hi, sl].astype(F32)
            mean = _dot2(y, avg)
            dev = y - mean
            var = _dot((dev * dev).astype(BF16), avg)
            gn = dev * lax.rsqrt(var + GN_EPS) * lnw_ref[:, sl] + lnb_ref[:, sl]
            bonus = bf_ref[lo:hi, sl].astype(F32) + br_ref[lo:hi, sl].astype(F32)
            mixed.append(((gn + bonus) * gate_ref[lo:hi, sl].astype(F32)).astype(BF16))
            yield
        mixed = jnp.concatenate(mixed, axis=1)
        for c0 in range(0, d, GN_LANES):
            sl = slice(c0, c0 + GN_LANES)
            o_ref[lo:hi, sl] = x_ref[lo:hi, sl] + _dot(mixed, wo_ref[:, sl])
            yield

    slab = min(tm, ROW_SLAB)
    _run_chains([slab_steps(lo, lo + slab) for lo in range(0, tm, slab)], skew=d // GN_LANES)


def _rwkv_out(h, scans, gate, lnw, lnb, wo, *, tm):
    b, s, d = h.shape
    tok = pl.BlockSpec((None, tm, d), lambda bi, i: (bi, i, 0))
    par = pl.BlockSpec((1, d), lambda bi, i: (0, 0))
    return pl.pallas_call(
        _out_body,
        grid=(b, s // tm),
        in_specs=[tok] * 6 + [par, par, pl.BlockSpec((d, d), lambda bi, i: (0, 0))],
        out_specs=tok,
        out_shape=jax.ShapeDtypeStruct((b, s, d), F32),
        compiler_params=_cparams(("parallel", "parallel")),
        name="rwkv_out",
    )(h, *scans, gate, lnw, lnb, wo)


def _pad_lora(w_in, w_out):
    r = w_in.shape[-1]
    pad = (-r) % V7X_LANES
    w_in = jnp.pad(w_in, [(0, 0)] * (w_in.ndim - 1) + [(0, pad)])
    w_out = jnp.pad(w_out, [(0, 0)] * (w_out.ndim - 2) + [(0, pad), (0, 0)])
    return w_in.astype(BF16), w_out.astype(BF16)


def _tile(n, want):
    t = min(n, want)
    while n % t:
        t //= 2
    return t


def _tiles(b, s, f):
    return dict(
        ffn_rows=_tile(b * s, 4 * V7X_MXU_DIM),
        ffn_cols=_tile(f, 2 * V7X_MXU_DIM),
        pool_rows=_tile(s, 4 * V7X_MXU_DIM),
        rwkv_rows=_tile(s, 2 * ROW_SLAB),
    )


def _rwkv_mixer(h, norm_g, mu, w_r, w_k, w_v, w_o, w0, w1, w2, a0, a1, a2, g1, g2, k_k, k_a, r_k, ln_w, ln_b, *, tm2):
    b, s, d = h.shape
    row = lambda z: z.reshape(1, d)
    rkv = _rwkv_rkv(h, norm_g, mu[jnp.array([0, 2, 3])].reshape(3, 1, d), w_r, w_k, w_v, tm=tm2)
    w1p, w2p = _pad_lora(w1, w2)
    a1p, a2p = _pad_lora(a1, a2)
    lw, alr, gate = _rwkv_lora(h, norm_g, mu[jnp.array([1, 4, 5])].reshape(3, 1, d),
                               w1p, w2p, w0.reshape(2, 1, d), a1p, a2p, a0.reshape(2, 1, d),
                               g1.astype(BF16), g2.astype(BF16), tm=tm2)
    y_f, bonus_f, y_r, bonus_r = _wkv(rkv, lw, alr, row(k_k), row(k_a), row(r_k))
    return _rwkv_out(h, (y_f, y_r, bonus_f, bonus_r), gate, row(ln_w), row(ln_b), w_o, tm=tm2)


def kernel(x, ffn1_norm, ffn1_gate, ffn1_up, ffn1_down, mix_norm, ffn2_norm, ffn2_gate, ffn2_up, ffn2_down, pool_w, pool_scale, rwkv_mu, rwkv_wr, rwkv_wk, rwkv_wv, rwkv_wo, rwkv_w0, rwkv_w1, rwkv_w2, rwkv_a0, rwkv_a1, rwkv_a2, rwkv_g1, rwkv_g2, rwkv_kk, rwkv_ka, rwkv_rk, rwkv_lnw, rwkv_lnb, final_norm):
    b, s, d = x.shape
    depth = ffn1_norm.shape[0]
    f = ffn1_gate.shape[2]
    row = lambda z: z.reshape(1, d)
    assert x.dtype == F32 and d % V7X_MXU_DIM == 0 and d % len(POOL_WINDOWS) == 0, (x.dtype, d)
    assert s % max(CHUNK, POOL_HALO) == 0 and f % V7X_LANES == 0, (s, f)
    tiles = _tiles(b, s, f)
    ffn_params = ((ffn1_norm, ffn1_gate, ffn1_up, ffn1_down), (ffn2_norm, ffn2_gate, ffn2_up, ffn2_down))
    order = [(i, half) for i in range(depth) for half in (0, 1)]

    weights = [ffn_params[0][k][0].astype(BF16) for k in (1, 2, 3)]
    rwkv_w = None
    h = x
    for q, (i, half) in enumerate(order):
        j = i // 2
        if half == 1:
            if i % 2 == 0:
                h = _pool(h, row(mix_norm[i]), pool_w[j].astype(BF16), row(pool_scale[j]), ts=tiles["pool_rows"])
            else:
                h = _rwkv_mixer(h, row(mix_norm[i]), rwkv_mu[j], *rwkv_w,
                                rwkv_w0[j], rwkv_w1[j], rwkv_w2[j], rwkv_a0[j], rwkv_a1[j], rwkv_a2[j],
                                rwkv_g1[j], rwkv_g2[j], rwkv_kk[j], rwkv_ka[j], rwkv_rk[j],
                                rwkv_lnw[j], rwkv_lnb[j], tm2=tiles["rwkv_rows"])
        casts = []
        if q + 1 < len(order):
            ni, nhalf = order[q + 1]
            casts += [(ffn_params[nhalf][k], ni) for k in (1, 2, 3)]
        if half == 0 and i % 2 == 1:
            casts += [(w, j) for w in (rwkv_wr, rwkv_wk, rwkv_wv, rwkv_wo)]
        out, cast = _ffn(h.reshape(b * s, d), row(ffn_params[half][0][i]), *weights, row(final_norm), casts,
                         final_norm=q == len(order) - 1, tm=tiles["ffn_rows"], tf=tiles["ffn_cols"])
        h = out.reshape(b, s, d)
        if q + 1 < len(order):
            weights, cast = cast[:3], cast[3:]
        if cast:
            rwkv_w = cast
    return h
```
